```python
import jax, jax.numpy as jnp
from jax import lax
import numpy as np

D_MODEL = 2048
BATCH = 4
SEQ = 2048
DEPTH = 1
DEC_BATCH = 128
DEC_SEQ = 4
PAST_LEN = 16384
PAGE_SIZE = 128

MEM_LEN = 256
D_A = D_MODEL
H_A = 4
DH_A = D_A // H_A
QKV_BLOCK = 4
CONV_A = 4
MLSTM_CHUNK = 128
D_B = D_MODEL // 2
G_B = 4
DG_B = D_B // G_B
CHUNK_B = 128
H_C = 4
DH_C = D_MODEL // 8
D_C = H_C * DH_C
N_BRANCH = 3
D_FF = 3 * D_MODEL
CONV_F = 3
N_IN = 2 * D_A + 2 * D_B + D_C + N_BRANCH * D_MODEL
EPS = 1e-6
NEG_INF = -1e30

kernel_name = 'hybrid_mlstm_gmlp_memxattn_step'


def rms_norm(x, g):
    xf = x.astype(jnp.float32)
    y = xf * lax.rsqrt(jnp.mean(xf * xf, axis=-1, keepdims=True) + EPS) * g.astype(jnp.float32)
    return y.astype(x.dtype)


def layer_norm(x, g):
    xf = x.astype(jnp.float32)
    xc = xf - jnp.mean(xf, axis=-1, keepdims=True)
    var = jnp.mean(xc * xc, axis=-1, keepdims=True)
    return (xc * lax.rsqrt(var + EPS) * g.astype(jnp.float32)).astype(x.dtype)


def causal_dwconv(x, prev, w, b):
    k_w = w.shape[0]
    seq = x.shape[1]
    xp = jnp.concatenate([prev.astype(x.dtype), x], axis=1)
    y = b + xp[:, 0:seq] * w[0]
    for j in range(1, k_w):
        y = y + xp[:, j:j + seq] * w[j]
    return y, xp[:, seq:]


def blockdiag_proj(x, w):
    bsz, seq, d = x.shape
    nb, blk, _ = w.shape
    return jnp.einsum('blni,nij->blnj', x.reshape(bsz, seq, nb, blk), w).reshape(bsz, seq, d)


def mlstm_chunkwise(q, k, v, log_i, log_f, c0, n0, m0):
    bsz, seq, nh, dh = q.shape
    lc = seq if seq <= MLSTM_CHUNK else MLSTM_CHUNK
    nc = seq // lc
    causal = jnp.tril(jnp.ones((lc, lc), dtype=bool))

    def to_chunks(a):
        return jnp.moveaxis(a.reshape((bsz, nc, lc) + a.shape[2:]), 1, 0)

    def step(carry, inp):
        c, n, m = carry
        qc, kc, vc, li, lf = inp
        b = jnp.cumsum(lf, axis=1)
        d = b[:, :, None, :] - b[:, None, :, :] + li[:, None, :, :]
        d = jnp.where(causal[None, :, :, None], d, NEG_INF)
        inter = m[:, None, :] + b
        m_t = jnp.maximum(jnp.max(d, axis=2), inter)
        w = jnp.exp(d - m_t[:, :, None, :])
        g = jnp.exp(inter - m_t)
        s = jnp.einsum('bthd,bshd->btsh', qc, kc) * w
        num = g[..., None] * jnp.einsum('bthd,bhde->bthe', qc, c) + jnp.einsum('btsh,bshe->bthe', s, vc)
        den = g * jnp.einsum('bthd,bhd->bth', qc, n) + jnp.sum(s, axis=2)
        h = num / jnp.maximum(jnp.abs(den), jnp.exp(-m_t))[..., None]
        b_last = b[:, -1, :]
        a = b_last[:, None, :] - b + li
        m_new = jnp.maximum(m + b_last, jnp.max(a, axis=1))
        decay = jnp.exp(m + b_last - m_new)
        ws = jnp.exp(a - m_new[:, None, :])
        c_new = decay[..., None, None] * c + jnp.einsum('bsh,bshd,bshe->bhde', ws, kc, vc)
        n_new = decay[..., None] * n + jnp.einsum('bsh,bshd->bhd', ws, kc)
        return (c_new, n_new, m_new), h

    (c, n, m), h = lax.scan(step, (c0, n0, m0),
                            (to_chunks(q), to_chunks(k), to_chunks(v), to_chunks(log_i), to_chunks(log_f)))
    h = jnp.moveaxis(h, 0, 1).reshape(bsz, seq, nh, dh)
    return h, c, n, m


def chunk_spatial_gate(u, vn, w_s, b_s):
    bsz, seq, _ = vn.shape
    lc = seq if seq <= CHUNK_B else CHUNK_B
    nc = seq // lc
    causal = jnp.tril(jnp.ones((lc, lc), dtype=bool))
    w = jnp.where(causal[None], w_s[:, :lc, :lc], 0.0).astype(vn.dtype)
    vg = vn.reshape(bsz, nc, lc, G_B, DG_B)
    mixed = jnp.einsum('gts,bnsgc->bntgc', w, vg) + b_s[:, :lc].T[None, None, :, :, None]
    return u * mixed.reshape(bsz, seq, D_B)


def mem_kv(mem, norm_mem, w_mem_kv):
    bsz, mlen, _ = mem.shape
    kv = rms_norm(mem, norm_mem) @ w_mem_kv
    k, v = jnp.split(kv, 2, axis=-1)
    return k.reshape(bsz, mlen, H_C, DH_C), v.reshape(bsz, mlen, H_C, DH_C)


def mem_cross_attend(q, mem_k, mem_v):
    s = jnp.einsum('blhd,bmhd->bhlm', q.astype(jnp.float32), mem_k.astype(jnp.float32)) * (DH_C ** -0.5)
    p = jax.nn.softmax(s, axis=-1)
    o = jnp.einsum('bhlm,bmhd->blhd', p, mem_v.astype(jnp.float32))
    return o.reshape(q.shape[0], q.shape[1], D_C).astype(q.dtype)


def trunk_layer(x, mem_k, mem_v, conv_a_prev, c0, n0, m0, ffn_prev, p):
    bsz, seq, _ = x.shape
    f32 = jnp.float32
    h = rms_norm(x, p['norm_mix_pre'])
    proj = h @ p['w_in']
    xm, z, u, vb, qc, gates = jnp.split(
        proj, [D_A, 2 * D_A, 2 * D_A + D_B, 2 * D_A + 2 * D_B, 2 * D_A + 2 * D_B + D_C], axis=-1)
    xconv, conv_a_new = causal_dwconv(xm, conv_a_prev, p['conv_a_w'], p['conv_a_b'])
    xconv = jax.nn.silu(xconv)
    qa = blockdiag_proj(xconv, p['w_q_a'])
    ka = blockdiag_proj(xconv, p['w_k_a'])
    va = blockdiag_proj(xm, p['w_v_a'])
    if_pre = (jnp.concatenate([qa, ka, va], axis=-1) @ p['w_if_a'] + p['b_if_a']).astype(f32)
    log_i = if_pre[..., :H_A]
    log_f = jax.nn.log_sigmoid(if_pre[..., H_A:])
    qh = qa.reshape(bsz, seq, H_A, DH_A).astype(f32)
    kh = ka.reshape(bsz, seq, H_A, DH_A).astype(f32) * (DH_A ** -0.5)
    vh = va.reshape(bsz, seq, H_A, DH_A).astype(f32)
    ht, c_new, n_new, m_new = mlstm_chunkwise(qh, kh, vh, log_i, log_f,
                                              c0.astype(f32), n0.astype(f32), m0.astype(f32))
    ht = layer_norm(ht, p['norm_h_a'].reshape(H_A, DH_A)).reshape(bsz, seq, D_A).astype(x.dtype)
    ya = jax.nn.sigmoid(z) * (ht + p['skip_a'] * xconv)
    branch_a = ya @ p['w_br_a']
    vn = layer_norm(jax.nn.gelu(vb), p['norm_v_b'])
    yb = chunk_spatial_gate(jax.nn.gelu(u), vn, p['w_s_b'], p['b_s_b'])
    branch_b = yb @ p['w_br_b']
    yc = mem_cross_attend(qc.reshape(bsz, seq, H_C, DH_C), mem_k, mem_v)
    branch_c = yc @ p['w_br_c']
    gate = jax.nn.sigmoid(gates.reshape(bsz, seq, N_BRANCH, D_MODEL))
    merged = gate[:, :, 0] * branch_a + gate[:, :, 1] * branch_b + gate[:, :, 2] * branch_c
    x = x + rms_norm(merged @ p['w_out'], p['norm_mix_post'])
    up = rms_norm(x, p['norm_ffn_pre']) @ p['w_ffn_up']
    upc, ffn_new = causal_dwconv(up, ffn_prev, p['conv_ffn_w'], p['conv_ffn_b'])
    gate_f, val_f = jnp.split(upc, 2, axis=-1)
    f = (jax.nn.gelu(gate_f) * val_f) @ p['w_ffn_down']
    y = x + rms_norm(f, p['norm_ffn_post'])
    dt = y.dtype
    return y, conv_a_new, c_new.astype(dt), n_new.astype(dt), m_new.astype(dt), vn, ffn_new


def setup_inputs(seed: int = 0) -> dict:
    key = jax.random.key(seed)
    keys = iter(jax.random.split(key, 64))

    def nrm(shape, scale):
        return jax.random.normal(next(keys), shape, jnp.float32) * scale

    def gain(n):
        return 1.0 + nrm((DEPTH, n), 0.02)

    L = DEPTH
    b_if = jnp.concatenate([nrm((L, H_A), 0.1),
                            jnp.linspace(3.0, 6.0, H_A)[None, :] + nrm((L, H_A), 0.1)], axis=-1)
    return {
        'x_prompt': nrm((BATCH, SEQ, D_MODEL), 1.0),
        'x_sample': nrm((DEC_BATCH, DEC_SEQ, D_MODEL), 1.0),
        'state_conv_a': nrm((L, DEC_BATCH, CONV_A - 1, D_A), 1.0),
        'state_mlstm_c': nrm((L, DEC_BATCH, H_A, DH_A, DH_A), DH_A ** -0.5),
        'state_mlstm_n': nrm((L, DEC_BATCH, H_A, DH_A), DH_A ** -0.5),
        'state_mlstm_m': nrm((L, DEC_BATCH, H_A), 1.0),
        'cache_mem_k': nrm((L, DEC_BATCH, MEM_LEN, H_C, DH_C), 1.0),
        'cache_mem_v': nrm((L, DEC_BATCH, MEM_LEN, H_C, DH_C), 1.0),
        'state_ffn_conv': nrm((L, DEC_BATCH, CONV_F - 1, 2 * D_FF), 1.0),
        'mem_prompt': nrm((BATCH, MEM_LEN, D_MODEL), 1.0),
        'norm_mix_pre': gain(D_MODEL),
        'w_in': nrm((L, D_MODEL, N_IN), D_MODEL ** -0.5),
        'conv_a_w': nrm((L, CONV_A, D_A), CONV_A ** -0.5),
        'conv_a_b': nrm((L, D_A), 0.02),
        'w_q_a': nrm((L, D_A // QKV_BLOCK, QKV_BLOCK, QKV_BLOCK), QKV_BLOCK ** -0.5),
        'w_k_a': nrm((L, D_A // QKV_BLOCK, QKV_BLOCK, QKV_BLOCK), QKV_BLOCK ** -0.5),
        'w_v_a': nrm((L, D_A // QKV_BLOCK, QKV_BLOCK, QKV_BLOCK), QKV_BLOCK ** -0.5),
        'w_if_a': nrm((L, 3 * D_A, 2 * H_A), (3 * D_A) ** -0.5),
        'b_if_a': b_if,
        'norm_h_a': gain(D_A),
        'skip_a': gain(D_A),
        'w_br_a': nrm((L, D_A, D_MODEL), D_A ** -0.5),
        'norm_v_b': gain(D_B),
        'w_s_b': nrm((L, G_B, CHUNK_B, CHUNK_B), CHUNK_B ** -0.5),
        'b_s_b': 1.0 + nrm((L, G_B, CHUNK_B), 0.1),
        'w_br_b': nrm((L, D_B, D_MODEL), D_B ** -0.5),
        'norm_mem': gain(D_MODEL),
        'w_mem_kv': nrm((L, D_MODEL, 2 * D_C), D_MODEL ** -0.5),
        'w_br_c': nrm((L, D_C, D_MODEL), D_C ** -0.5),
        'w_out': nrm((L, D_MODEL, D_MODEL), D_MODEL ** -0.5),
        'norm_mix_post': gain(D_MODEL),
        'norm_ffn_pre': gain(D_MODEL),
        'w_ffn_up': nrm((L, D_MODEL, 2 * D_FF), D_MODEL ** -0.5),
        'conv_ffn_w': nrm((L, CONV_F, 2 * D_FF), CONV_F ** -0.5),
        'conv_ffn_b': nrm((L, 2 * D_FF), 0.02),
        'w_ffn_down': nrm((L, D_FF, D_MODEL), D_FF ** -0.5),
        'norm_ffn_post': gain(D_MODEL),
    }


def reference(x_prompt, x_sample, state_conv_a, state_mlstm_c, state_mlstm_n, state_mlstm_m,
              cache_mem_k, cache_mem_v, state_ffn_conv, mem_prompt,
              norm_mix_pre, w_in, conv_a_w, conv_a_b, w_q_a, w_k_a, w_v_a, w_if_a, b_if_a,
              norm_h_a, skip_a, w_br_a, norm_v_b, w_s_b, b_s_b, w_br_b, norm_mem, w_mem_kv,
              w_br_c, w_out, norm_mix_post, norm_ffn_pre, w_ffn_up, conv_ffn_w, conv_ffn_b,
              w_ffn_down, norm_ffn_post):
    xp, xs = x_prompt, x_sample
    dt = x_prompt.dtype
    p_conv_a, p_c, p_n, p_m, p_k, p_v, p_ffn = [], [], [], [], [], [], []
    s_conv_a, s_c, s_n, s_m, s_gv, s_ffn = [], [], [], [], [], []
    for l in range(DEPTH):
        p = {
            'norm_mix_pre': norm_mix_pre[l], 'w_in': w_in[l],
            'conv_a_w': conv_a_w[l], 'conv_a_b': conv_a_b[l],
            'w_q_a': w_q_a[l], 'w_k_a': w_k_a[l], 'w_v_a': w_v_a[l],
            'w_if_a': w_if_a[l], 'b_if_a': b_if_a[l], 'norm_h_a': norm_h_a[l], 'skip_a': skip_a[l],
            'w_br_a': w_br_a[l], 'norm_v_b': norm_v_b[l], 'w_s_b': w_s_b[l], 'b_s_b': b_s_b[l],
            'w_br_b': w_br_b[l], 'w_br_c': w_br_c[l], 'w_out': w_out[l],
            'norm_mix_post': norm_mix_post[l], 'norm_ffn_pre': norm_ffn_pre[l],
            'w_ffn_up': w_ffn_up[l], 'conv_ffn_w': conv_ffn_w[l], 'conv_ffn_b': conv_ffn_b[l],
            'w_ffn_down': w_ffn_down[l], 'norm_ffn_post': norm_ffn_post[l],
        }
        mk, mv = mem_kv(mem_prompt, norm_mem[l], w_mem_kv[l])
        xp, a1, c1, n1, m1, _, f1 = trunk_layer(
            xp, mk, mv,
            jnp.zeros((BATCH, CONV_A - 1, D_A), dt),
            jnp.zeros((BATCH, H_A, DH_A, DH_A), jnp.float32),
            jnp.zeros((BATCH, H_A, DH_A), jnp.float32),
            jnp.zeros((BATCH, H_A), jnp.float32),
            jnp.zeros((BATCH, CONV_F - 1, 2 * D_FF), dt), p)
        xs, a2, c2, n2, m2, v2, f2 = trunk_layer(
            xs, cache_mem_k[l], cache_mem_v[l], state_conv_a[l], state_mlstm_c[l],
            state_mlstm_n[l], state_mlstm_m[l], state_ffn_conv[l], p)
        p_conv_a.append(a1); p_c.append(c1); p_n.append(n1); p_m.append(m1)
        p_k.append(mk); p_v.append(mv); p_ffn.append(f1)
        s_conv_a.append(a2); s_c.append(c2); s_n.append(n2); s_m.append(m2)
        s_gv.append(v2); s_ffn.append(f2)
    return (xp, xs,
            jnp.stack(p_conv_a), jnp.stack(p_c), jnp.stack(p_n), jnp.stack(p_m),
            jnp.stack(p_k), jnp.stack(p_v), jnp.stack(p_ffn),
            jnp.stack(s_conv_a), jnp.stack(s_c), jnp.stack(s_n), jnp.stack(s_m),
            jnp.stack(s_gv), jnp.stack(s_ffn))
```

```python
import functools

import jax
import jax.numpy as jnp
from jax import lax
from jax.experimental import pallas as pl
from jax.experimental.pallas import tpu as pltpu

F32 = jnp.float32
BF16 = jnp.bfloat16

D_MODEL = 2048
H_A = 4
DH_A = 512
QKV_BLOCK = 4
CONV_A = 4
CONV_F = 3
D_B = 1024
G_B = 4
DG_B = 256
H_C = 4
DH_C = 256
D_C = 1024
MEM_LEN = 256
D_FF = 6144
N_IN = 13312
EPS = 1e-6
NEG_INF = -1e30

CHUNK = 128
SAMPLE_ROWS = 8
BD_BLOCK = 256
GATE_LANES = 128
V7X_VMEM_BYTES = 64 * 1024 * 1024
VMEM_LIMIT = V7X_VMEM_BYTES - 8 * 1024 * 1024


def _cparams(*sem):
    return pltpu.CompilerParams(dimension_semantics=sem, vmem_limit_bytes=VMEM_LIMIT)


def _rms_kernel(x_ref, g_ref, o_ref):
    x = x_ref[...]
    ms = jnp.mean(x * x, axis=-1, keepdims=True)
    o_ref[...] = (x * lax.rsqrt(ms + EPS) * g_ref[...]).astype(o_ref.dtype)


def rms_cast(x, g, tile=512):
    r, d = x.shape
    return pl.pallas_call(
        _rms_kernel,
        grid=(r // tile,),
        in_specs=[pl.BlockSpec((tile, d), lambda i: (i, 0)),
                  pl.BlockSpec((1, d), lambda i: (0, 0))],
        out_specs=pl.BlockSpec((tile, d), lambda i: (i, 0)),
        out_shape=jax.ShapeDtypeStruct((r, d), BF16),
        compiler_params=_cparams("parallel"),
        name="rms_cast",
    )(x, g.reshape(1, d))


def _mm_kernel(a_ref, w_ref, o_ref):
    o_ref[...] = jnp.dot(a_ref[...].astype(BF16), w_ref[...].astype(BF16),
                         preferred_element_type=F32)


def matmul(a, w, bm, bn):
    r, k = a.shape
    n = w.shape[1]
    bm = min(bm, r)
    return pl.pallas_call(
        _mm_kernel,
        grid=(n // bn, r // bm),
        in_specs=[pl.BlockSpec((bm, k), lambda j, i: (i, 0)),
                  pl.BlockSpec((k, bn), lambda j, i: (0, j))],
        out_specs=pl.BlockSpec((bm, bn), lambda j, i: (i, j)),
        out_shape=jax.ShapeDtypeStruct((r, n), F32),
        compiler_params=_cparams("parallel", "arbitrary"),
        name="matmul",
    )(a, w)


def _post_mix_kernel(x_ref, o_ref, g1_ref, g2_ref, x1_ref, h2_ref):
    o = o_ref[...]
    x1 = x_ref[...] + o * lax.rsqrt(jnp.mean(o * o, axis=-1, keepdims=True) + EPS) * g1_ref[...]
    x1_ref[...] = x1
    ms = jnp.mean(x1 * x1, axis=-1, keepdims=True)
    h2_ref[...] = (x1 * lax.rsqrt(ms + EPS) * g2_ref[...]).astype(h2_ref.dtype)


def post_mix(x, o, g_post, g_pre, tile=512):
    r, d = x.shape
    row = pl.BlockSpec((tile, d), lambda i: (i, 0))
    vec = pl.BlockSpec((1, d), lambda i: (0, 0))
    return pl.pallas_call(
        _post_mix_kernel,
        grid=(r // tile,),
        in_specs=[row, row, vec, vec],
        out_specs=[row, row],
        out_shape=[jax.ShapeDtypeStruct((r, d), F32), jax.ShapeDtypeStruct((r, d), BF16)],
        compiler_params=_cparams("parallel"),
        name="post_mix",
    )(x, o, g_post.reshape(1, d), g_pre.reshape(1, d))


def _post_ffn_kernel(x_ref, f_ref, g_ref, y_ref):
    f = f_ref[...]
    y_ref[...] = x_ref[...] + f * lax.rsqrt(jnp.mean(f * f, axis=-1, keepdims=True) + EPS) * g_ref[...]


def post_ffn(x, f, g, tile=512):
    r, d = x.shape
    row = pl.BlockSpec((tile, d), lambda i: (i, 0))
    return pl.pallas_call(
        _post_ffn_kernel,
        grid=(r // tile,),
        in_specs=[row, row, pl.BlockSpec((1, d), lambda i: (0, 0))],
        out_specs=row,
        out_shape=jax.ShapeDtypeStruct((r, d), F32),
        compiler_params=_cparams("parallel"),
        name="post_ffn",
    )(x, f, g.reshape(1, d))


def _log_sigmoid(x):
    return jnp.minimum(x, 0.0) - jnp.log1p(jnp.exp(-jnp.abs(x)))


def _layer_norm_rows(x):
    xc = x - jnp.mean(x, axis=-1, keepdims=True)
    return xc * lax.rsqrt(jnp.mean(xc * xc, axis=-1, keepdims=True) + EPS)


def _branch_a_kernel(*refs, rows_in, valid_len, zero_state):
    if zero_state:
        (xm_ref, z_ref, st_ref, cw_ref, cb_ref, wq_ref, wk_ref, wv_ref, wif_ref, bif_ref,
         nh_ref, sk_ref, ya_ref, c_ref, n_ref, m_ref, ext_ref) = refs
    else:
        (xm_ref, z_ref, st_ref, c0_ref, n0_ref, m0_ref, cw_ref, cb_ref, wq_ref, wk_ref, wv_ref,
         wif_ref, bif_ref, nh_ref, sk_ref, ya_ref, c_ref, n_ref, m_ref, ext_ref) = refs
    lc = CHUNK
    chunk = pl.program_id(1)

    @pl.when(chunk == 0)
    def _init():
        ext_ref[0:8, :] = st_ref[0]
        if zero_state:
            c_ref[...] = jnp.zeros(c_ref.shape, F32)
            n_ref[...] = jnp.zeros(n_ref.shape, F32)
            m_ref[...] = jnp.zeros(m_ref.shape, F32)
        else:
            c_ref[...] = c0_ref[...]
            n_ref[...] = n0_ref[...]
            m_ref[...] = m0_ref[...]

    if rows_in < lc:
        ext_ref[8 + rows_in:8 + lc, :] = jnp.zeros((lc - rows_in, D_MODEL), F32)
    ext_ref[8:8 + rows_in, :] = xm_ref[...]
    taps = [ext_ref[pl.ds(8 - (CONV_A - 1) + j, lc), :] for j in range(CONV_A)]
    cw = cw_ref[...]
    y = cb_ref[...] + taps[0] * cw[0:1, :]
    for j in range(1, CONV_A):
        y = y + taps[j] * cw[j:j + 1, :]
    xconv = y * jax.nn.sigmoid(y)
    xm = taps[CONV_A - 1]
    ext_ref[0:8, :] = ext_ref[lc:lc + 8, :]

    def blockdiag(xb, w_ref):
        nblk = D_MODEL // BD_BLOCK
        return jnp.concatenate(
            [jnp.dot(xb[:, j * BD_BLOCK:(j + 1) * BD_BLOCK], w_ref[j], preferred_element_type=F32)
             for j in range(nblk)], axis=1)

    xcb = xconv.astype(BF16)
    qb = blockdiag(xcb, wq_ref).astype(BF16)
    kb = blockdiag(xcb, wk_ref).astype(BF16)
    vb = blockdiag(xm.astype(BF16), wv_ref).astype(BF16)
    if_pre = (jnp.dot(qb, wif_ref[0:D_MODEL, :], preferred_element_type=F32)
              + jnp.dot(kb, wif_ref[D_MODEL:2 * D_MODEL, :], preferred_element_type=F32)
              + jnp.dot(vb, wif_ref[2 * D_MODEL:3 * D_MODEL, :], preferred_element_type=F32)
              + bif_ref[...])
    log_f_all = _log_sigmoid(if_pre)

    lane = lax.broadcasted_iota(jnp.int32, (1, GATE_LANES), 1)
    r = lax.broadcasted_iota(jnp.int32, (lc, lc), 0)
    s_idx = lax.broadcasted_iota(jnp.int32, (lc, lc), 1)
    tril = s_idx <= r
    eye = s_idx == r
    row_valid = lax.broadcasted_iota(jnp.int32, (lc, 1), 0) < valid_len
    scale = DH_A ** -0.5

    m_all = m_ref[0, 0:1, :]
    m_out = m_all
    for h in range(H_A):
        hs = slice(h * DH_A, (h + 1) * DH_A)
        qh, kh, vh = qb[:, hs], kb[:, hs], vb[:, hs]
        li_col = jnp.sum(jnp.where(lane == h, if_pre, 0.0), axis=1, keepdims=True)
        lf_col = jnp.sum(jnp.where(lane == H_A + h, log_f_all, 0.0), axis=1, keepdims=True)
        if valid_len < lc:
            li_col = jnp.where(row_valid, li_col, NEG_INF)
            lf_col = jnp.where(row_valid, lf_col, 0.0)
        m_prev = jnp.sum(jnp.where(lane == h, m_all, 0.0), axis=1, keepdims=True)

        b_row = jnp.sum(jnp.where(r <= s_idx, lf_col, 0.0), axis=0, keepdims=True)
        li_row = jnp.sum(jnp.where(eye, li_col, 0.0), axis=0, keepdims=True)
        b_col = jnp.sum(jnp.where(eye, b_row, 0.0), axis=1, keepdims=True)

        d = jnp.where(tril, b_col - b_row + li_row, NEG_INF)
        inter = m_prev + b_col
        m_t = jnp.maximum(jnp.max(d, axis=1, keepdims=True), inter)
        w = jnp.exp(d - m_t)
        g = jnp.exp(inter - m_t)
        qk = lax.dot_general(qh, kh, (((1,), (1,)), ((), ())), preferred_element_type=F32)
        s = qk * (w * scale)
        c_h = c_ref[0, h]
        n_h = n_ref[0, h:h + 1, :]
        num = (g * jnp.dot(qh, c_h.astype(BF16), preferred_element_type=F32)
               + jnp.dot(s.astype(BF16), vh, preferred_element_type=F32))
        qn = jnp.sum(qh.astype(F32) * n_h, axis=1, keepdims=True)
        den = g * qn + jnp.sum(s, axis=1, keepdims=True)
        ht = num / jnp.maximum(jnp.abs(den), jnp.exp(-m_t))

        b_last = jnp.sum(lf_col, axis=0, keepdims=True)
        a_col = b_last - b_col + li_col
        m_new = jnp.maximum(m_prev + b_last, jnp.max(a_col, axis=0, keepdims=True))
        decay = jnp.exp(m_prev + b_last - m_new)
        kw = kh.astype(F32) * (jnp.exp(a_col - m_new) * scale)
        kwt = jnp.transpose(kw).astype(BF16)
        c_ref[0, h] = decay * c_h + jnp.dot(kwt, vh, preferred_element_type=F32)
        n_ref[0, h:h + 1, :] = decay * n_h + jnp.sum(kw, axis=0, keepdims=True)
        m_out = jnp.where(lane == h, m_new, m_out)

        hn = _layer_norm_rows(ht[0:rows_in]) * nh_ref[:, hs]
        ya = jax.nn.sigmoid(z_ref[:, hs]) * (hn + sk_ref[:, hs] * xconv[0:rows_in, hs])
        ya_ref[:, hs] = ya.astype(ya_ref.dtype)
    m_ref[0, 0:1, :] = m_out


def branch_a(proj, st8, states, p, *, nseq, nchunk, rows_in, valid_len, out_dtype):
    r = proj.shape[0]
    zero_state = states is None
    row = lambda col: pl.BlockSpec((rows_in, D_MODEL), lambda b, c, col=col: (b * nchunk + c, col))
    full = lambda shape: pl.BlockSpec(shape, lambda b, c: (0,) * len(shape))
    c_spec = pl.BlockSpec((1, H_A, DH_A, DH_A), lambda b, c: (b, 0, 0, 0))
    n_spec = pl.BlockSpec((1, H_A, DH_A), lambda b, c: (b, 0, 0))
    m_spec = pl.BlockSpec((1, 8, GATE_LANES), lambda b, c: (b, 0, 0))
    in_specs = [row(0), row(1), pl.BlockSpec((1, 8, D_MODEL), lambda b, c: (b, 0, 0))]
    args = [proj, proj, st8]
    if not zero_state:
        in_specs += [c_spec, n_spec, m_spec]
        args += list(states)
    in_specs += [full((CONV_A, D_MODEL)), full((1, D_MODEL)),
                 full((D_MODEL // BD_BLOCK, BD_BLOCK, BD_BLOCK)),
                 full((D_MODEL // BD_BLOCK, BD_BLOCK, BD_BLOCK)),
                 full((D_MODEL // BD_BLOCK, BD_BLOCK, BD_BLOCK)),
                 full((3 * D_MODEL, GATE_LANES)), full((1, GATE_LANES)),
                 full((1, D_MODEL)), full((1, D_MODEL))]
    args += [p["conv_a_w"], p["conv_a_b"], p["wq_bd"], p["wk_bd"], p["wv_bd"], p["w_if"], p["b_if"],
             p["norm_h_a"], p["skip_a"]]
    kern = functools.partial(_branch_a_kernel, rows_in=rows_in, valid_len=valid_len,
                             zero_state=zero_state)
    return pl.pallas_call(
        kern,
        grid=(nseq, nchunk),
        in_specs=in_specs,
        out_specs=[pl.BlockSpec((rows_in, D_MODEL), lambda b, c: (b * nchunk + c, 0)),
                   c_spec, n_spec, m_spec],
        out_shape=[jax.ShapeDtypeStruct((r, D_MODEL), out_dtype),
                   jax.ShapeDtypeStruct((nseq, H_A, DH_A, DH_A), F32),
                   jax.ShapeDtypeStruct((nseq, H_A, DH_A), F32),
                   jax.ShapeDtypeStruct((nseq, 8, GATE_LANES), F32)],
        scratch_shapes=[pltpu.VMEM((8 + CHUNK, D_MODEL), F32)],
        compiler_params=_cparams("parallel", "arbitrary"),
        name="branch_a",
    )(*args)


def _gmlp_kernel(u_ref, v_ref, nv_ref, wmix_ref, bmix_ref, yb_ref, vn_ref):
    vn = _layer_norm_rows(jax.nn.gelu(v_ref[...])) * nv_ref[...]
    vn_ref[...] = vn
    vnb = vn.astype(BF16)
    bias = bmix_ref[...]
    lane = lax.broadcasted_iota(jnp.int32, (1, GATE_LANES), 1)
    mixed = []
    for g in range(G_B):
        b_col = jnp.sum(jnp.where(lane == g, bias, 0.0), axis=1, keepdims=True)
        mixed.append(jnp.dot(wmix_ref[g], vnb[:, g * DG_B:(g + 1) * DG_B],
                             preferred_element_type=F32) + b_col)
    yb_ref[...] = (jax.nn.gelu(u_ref[...]) * jnp.concatenate(mixed, axis=1)).astype(yb_ref.dtype)


def gmlp(proj, norm_v, wmix, bmix):
    r = proj.shape[0]
    return pl.pallas_call(
        _gmlp_kernel,
        grid=(r // CHUNK,),
        in_specs=[pl.BlockSpec((CHUNK, D_B), lambda i: (i, 4)),
                  pl.BlockSpec((CHUNK, D_B), lambda i: (i, 5)),
                  pl.BlockSpec((1, D_B), lambda i: (0, 0)),
                  pl.BlockSpec((G_B, CHUNK, CHUNK), lambda i: (0, 0, 0)),
                  pl.BlockSpec((CHUNK, GATE_LANES), lambda i: (0, 0))],
        out_specs=[pl.BlockSpec((CHUNK, D_B), lambda i: (i, 0)),
                   pl.BlockSpec((CHUNK, D_B), lambda i: (i, 0))],
        out_shape=[jax.ShapeDtypeStruct((r, D_B), BF16), jax.ShapeDtypeStruct((r, D_B), F32)],
        compiler_params=_cparams("parallel"),
        name="gmlp",
    )(proj, proj, norm_v.reshape(1, D_B), wmix, bmix)


def _xattn_kernel(q_ref, k_ref, v_ref, o_ref, *, nseq, rq):
    for i in range(nseq):
        q = q_ref[i * rq:(i + 1) * rq, :].astype(BF16)
        k = k_ref[i * MEM_LEN:(i + 1) * MEM_LEN, :].astype(BF16)
        v = v_ref[i * MEM_LEN:(i + 1) * MEM_LEN, :].astype(BF16)
        outs = []
        for h in range(H_C):
            hs = slice(h * DH_C, (h + 1) * DH_C)
            s = lax.dot_general(q[:, hs], k[:, hs], (((1,), (1,)), ((), ())),
                                preferred_element_type=F32) * (DH_C ** -0.5)
            e = jnp.exp(s - jnp.max(s, axis=-1, keepdims=True))
            prob = e / jnp.sum(e, axis=-1, keepdims=True)
            outs.append(jnp.dot(prob.astype(BF16), v[:, hs], preferred_element_type=F32))
        o_ref[i * rq:(i + 1) * rq, :] = jnp.concatenate(outs, axis=1).astype(o_ref.dtype)


def xattn(proj, k_arr, v_arr, k_col, v_col, *, ngroup, nseq, nq, rq, out_dtype):
    r = proj.shape[0]
    kern = functools.partial(_xattn_kernel, nseq=nseq, rq=rq)
    return pl.pallas_call(
        kern,
        grid=(ngroup, nq),
        in_specs=[pl.BlockSpec((nseq * rq, D_C), lambda g, t: (g * nq + t, 6)),
                  pl.BlockSpec((nseq * MEM_LEN, D_C), lambda g, t: (g, k_col)),
                  pl.BlockSpec((nseq * MEM_LEN, D_C), lambda g, t: (g, v_col))],
        out_specs=pl.BlockSpec((nseq * rq, D_C), lambda g, t: (g * nq + t, 0)),
        out_shape=jax.ShapeDtypeStruct((r, D_C), out_dtype),
        compiler_params=_cparams("parallel", "arbitrary"),
        name="xattn",
    )(proj, k_arr, v_arr)


def _merge_kernel(ya_ref, yb_ref, yc_ref, ga_ref, gb_ref, gc_ref, wa_ref, wb_ref, wc_ref, o_ref):
    def br(y_ref, w_ref):
        return jnp.dot(y_ref[...].astype(BF16), w_ref[...].astype(BF16), preferred_element_type=F32)

    merged = (jax.nn.sigmoid(ga_ref[...]) * br(ya_ref, wa_ref)
              + jax.nn.sigmoid(gb_ref[...]) * br(yb_ref, wb_ref)
              + jax.nn.sigmoid(gc_ref[...]) * br(yc_ref, wc_ref))
    o_ref[...] = merged.astype(o_ref.dtype)


def merge(ya, yb, yc, proj, w_a, w_b, w_c, bm=512, bn=512):
    r = ya.shape[0]
    bm = min(bm, r)
    gate0 = (2 * D_MODEL + 2 * D_B + D_C) // bn
    per = D_MODEL // bn
    gate = lambda br: pl.BlockSpec((bm, bn), lambda j, i, br=br: (i, gate0 + br * per + j))
    act = lambda k: pl.BlockSpec((bm, k), lambda j, i: (i, 0))
    wgt = lambda k: pl.BlockSpec((k, bn), lambda j, i: (0, j))
    return pl.pallas_call(
        _merge_kernel,
        grid=(per, r // bm),
        in_specs=[act(D_MODEL), act(D_B), act(D_C), gate(0), gate(1), gate(2),
                  wgt(D_MODEL), wgt(D_B), wgt(D_C)],
        out_specs=pl.BlockSpec((bm, bn), lambda j, i: (i, j)),
        out_shape=jax.ShapeDtypeStruct((r, D_MODEL), BF16),
        compiler_params=_cparams("parallel", "arbitrary"),
        name="merge",
    )(ya, yb, yc, proj, proj, proj, w_a, w_b, w_c)


def _ffn_act_kernel(g_ref, v_ref, sg_ref, sv_ref, wg_ref, wv_ref, bg_ref, bv_ref, o_ref, ext_ref,
                    *, nseq, tl, nchunk):
    chunk = pl.program_id(2)

    def conv(which, i, x_ref, st_ref, w_ref, b_ref):
        if nchunk == 1:
            ext_ref[which, 0:8, :] = st_ref[i]
        else:
            @pl.when(chunk == 0)
            def _():
                ext_ref[which, 0:8, :] = st_ref[i]
        ext_ref[which, 8:8 + tl, :] = x_ref[i * tl:(i + 1) * tl, :]
        w = w_ref[...]
        y = b_ref[...]
        for j in range(CONV_F):
            y = y + ext_ref[which, pl.ds(8 - (CONV_F - 1) + j, tl), :] * w[j:j + 1, :]
        ext_ref[which, 0:8, :] = ext_ref[which, tl:tl + 8, :]
        return y

    for i in range(nseq):
        gate = conv(0, i, g_ref, sg_ref, wg_ref, bg_ref)
        val = conv(1, i, v_ref, sv_ref, wv_ref, bv_ref)
        o_ref[i * tl:(i + 1) * tl, :] = (jax.nn.gelu(gate) * val).astype(o_ref.dtype)


def ffn_act(up, st8, conv_w, conv_b, *, ngroup, nseq, nchunk, tl, bc=2048):
    r = up.shape[0]
    ncol = D_FF // bc
    x_spec = lambda off: pl.BlockSpec((nseq * tl, bc), lambda g, j, c, off=off: (g * nchunk + c, off + j))
    s_spec = lambda off: pl.BlockSpec((nseq, 8, bc), lambda g, j, c, off=off: (g, 0, off + j))
    w_spec = lambda off: pl.BlockSpec((CONV_F, bc), lambda g, j, c, off=off: (0, off + j))
    b_spec = lambda off: pl.BlockSpec((1, bc), lambda g, j, c, off=off: (0, off + j))
    kern = functools.partial(_ffn_act_kernel, nseq=nseq, tl=tl, nchunk=nchunk)
    return pl.pallas_call(
        kern,
        grid=(ngroup, ncol, nchunk),
        in_specs=[x_spec(0), x_spec(ncol), s_spec(0), s_spec(ncol), w_spec(0), w_spec(ncol),
                  b_spec(0), b_spec(ncol)],
        out_specs=pl.BlockSpec((nseq * tl, bc), lambda g, j, c: (g * nchunk + c, j)),
        out_shape=jax.ShapeDtypeStruct((r, D_FF), BF16),
        scratch_shapes=[pltpu.VMEM((2, 8 + tl, bc), F32)],
        compiler_params=_cparams("parallel", "parallel", "arbitrary"),
        name="ffn_act",
    )(up, up, st8, st8, conv_w, conv_w, conv_b, conv_b)


def _blockdiag_tiles(w):
    per = BD_BLOCK // QKV_BLOCK
    w = w.reshape(D_MODEL // BD_BLOCK, per, QKV_BLOCK, QKV_BLOCK)
    eye = jnp.eye(per, dtype=w.dtype)
    t = jnp.einsum("jnik,nm->jnimk", w, eye)
    return t.reshape(D_MODEL // BD_BLOCK, BD_BLOCK, BD_BLOCK).astype(BF16)


def _prep_params(conv_a_w, conv_a_b, w_q_a, w_k_a, w_v_a, w_if_a, b_if_a, norm_h_a, skip_a):
    pad = GATE_LANES - 2 * H_A
    return {
        "conv_a_w": conv_a_w,
        "conv_a_b": conv_a_b.reshape(1, D_MODEL),
        "wq_bd": _blockdiag_tiles(w_q_a),
        "wk_bd": _blockdiag_tiles(w_k_a),
        "wv_bd": _blockdiag_tiles(w_v_a),
        "w_if": jnp.pad(w_if_a, ((0, 0), (0, pad))).astype(BF16),
        "b_if": jnp.pad(b_if_a, (0, pad)).reshape(1, GATE_LANES),
        "norm_h_a": norm_h_a.reshape(1, D_MODEL),
        "skip_a": skip_a.reshape(1, D_MODEL),
    }


def _tail_rows(state, k):
    return jnp.pad(state, ((0, 0), (8 - k, 0), (0, 0)))


def _trunk(x, mem, conv_prev, mlstm_state, ffn_prev, w, pa, *, nseq, seq, valid_len):
    r = nseq * seq
    prompt = seq >= CHUNK
    k_arr, v_arr, k_col, v_col = mem

    h = rms_cast(x, w["norm_mix_pre"])
    proj = matmul(h, w["w_in"], 1024, 1024)

    st8 = _tail_rows(conv_prev, CONV_A - 1)
    if prompt:
        ya, c_new, n_new, m_new = branch_a(proj, st8, mlstm_state, pa, nseq=nseq, nchunk=seq // CHUNK,
                                           rows_in=CHUNK, valid_len=CHUNK, out_dtype=BF16)
        tril = jnp.tril(jnp.ones((CHUNK, CHUNK), bool))
        wmix = jnp.where(tril[None], w["w_s_b"], 0.0).astype(BF16)
        bmix = jnp.pad(w["b_s_b"].T, ((0, 0), (0, GATE_LANES - G_B)))
        yb, vn = gmlp(proj, w["norm_v_b"], wmix, bmix)
        yc = xattn(proj, k_arr, v_arr, k_col, v_col, ngroup=nseq, nseq=1, nq=seq // 256, rq=256,
                   out_dtype=BF16)
    else:
        ya, c_new, n_new, m_new = branch_a(proj, st8, mlstm_state, pa, nseq=nseq, nchunk=1,
                                           rows_in=seq, valid_len=valid_len, out_dtype=F32)
        per = CHUNK // seq
        tril = jnp.tril(jnp.ones((seq, seq), bool))
        w_small = jnp.where(tril[None], w["w_s_b"][:, :seq, :seq], 0.0)
        wmix = jnp.einsum("ab,gts->gatbs", jnp.eye(per, dtype=F32), w_small)
        wmix = wmix.reshape(G_B, CHUNK, CHUNK).astype(BF16)
        bmix = jnp.pad(jnp.tile(w["b_s_b"][:, :seq].T, (per, 1)), ((0, 0), (0, GATE_LANES - G_B)))
        yb, vn = gmlp(proj, w["norm_v_b"], wmix, bmix)
        yc = xattn(proj, k_arr, v_arr, k_col, v_col, ngroup=nseq // 4, nseq=4, nq=1, rq=seq,
                   out_dtype=F32)

    merged = merge(ya, yb, yc, proj, w["w_br_a"], w["w_br_b"], w["w_br_c"])
    o = matmul(merged, w["w_out"], 1024, 1024)
    x1, h2 = post_mix(x, o, w["norm_mix_post"], w["norm_ffn_pre"])

    up = matmul(h2, w["w_ffn_up"], 1024, 1024)
    fst8 = _tail_rows(ffn_prev, CONV_F - 1)
    conv_b = w["conv_ffn_b"].reshape(1, 2 * D_FF)
    if prompt:
        act = ffn_act(up, fst8, w["conv_ffn_w"], conv_b, ngroup=nseq, nseq=1, nchunk=seq // 256, tl=256)
    else:
        act = ffn_act(up, fst8, w["conv_ffn_w"], conv_b, ngroup=nseq // 16, nseq=16, nchunk=1, tl=seq)
    f = matmul(act, w["w_ffn_down"], 512, 512)
    y = post_ffn(x1, f, w["norm_ffn_post"])
    return y, proj, up, vn, c_new, n_new, m_new[:, 0, :H_A]


def kernel(x_prompt, x_sample, state_conv_a, state_mlstm_c, state_mlstm_n, state_mlstm_m, cache_mem_k, cache_mem_v, state_ffn_conv, mem_prompt, norm_mix_pre, w_in, conv_a_w, conv_a_b, w_q_a, w_k_a, w_v_a, w_if_a, b_if_a, norm_h_a, skip_a, w_br_a, norm_v_b, w_s_b, b_s_b, w_br_b, norm_mem, w_mem_kv, w_br_c, w_out, norm_mix_post, norm_ffn_pre, w_ffn_up, conv_ffn_w, conv_ffn_b, w_ffn_down, norm_ffn_post):
    nb, seq, _ = x_prompt.shape
    db, dseq, _ = x_sample.shape
    w = {
        "norm_mix_pre": norm_mix_pre[0], "w_in": w_in[0], "w_br_a": w_br_a[0], "norm_v_b": norm_v_b[0],
        "w_s_b": w_s_b[0], "b_s_b": b_s_b[0], "w_br_b": w_br_b[0], "w_br_c": w_br_c[0], "w_out": w_out[0],
        "norm_mix_post": norm_mix_post[0], "norm_ffn_pre": norm_ffn_pre[0], "w_ffn_up": w_ffn_up[0],
        "conv_ffn_w": conv_ffn_w[0], "conv_ffn_b": conv_ffn_b[0], "w_ffn_down": w_ffn_down[0],
        "norm_ffn_post": norm_ffn_post[0],
    }
    pa = _prep_params(conv_a_w[0], conv_a_b[0], w_q_a[0], w_k_a[0], w_v_a[0], w_if_a[0], b_if_a[0],
                      norm_h_a[0], skip_a[0])

    mem_h = rms_cast(mem_prompt.reshape(nb * MEM_LEN, D_MODEL), norm_mem[0])
    kv = matmul(mem_h, w_mem_kv[0], 1024, 1024)
    yp, proj_p, up_p, _, c_p, n_p, m_p = _trunk(
        x_prompt.reshape(nb * seq, D_MODEL), (kv, kv, 0, 1),
        jnp.zeros((nb, CONV_A - 1, D_MODEL), F32), None, jnp.zeros((nb, CONV_F - 1, 2 * D_FF), F32),
        w, pa, nseq=nb, seq=seq, valid_len=seq)
    proj_p = proj_p.reshape(nb, seq, N_IN)
    up_p = up_p.reshape(nb, seq, 2 * D_FF)

    xs = jnp.pad(x_sample, ((0, 0), (0, SAMPLE_ROWS - dseq), (0, 0))).reshape(db * SAMPLE_ROWS, D_MODEL)
    m0 = jnp.pad(state_mlstm_m[0], ((0, 0), (0, GATE_LANES - H_A)))
    m0 = jnp.pad(m0[:, None, :], ((0, 0), (0, 7), (0, 0)))
    ys, proj_s, up_s, vn_s, c_s, n_s, m_s = _trunk(
        xs, (cache_mem_k[0].reshape(db * MEM_LEN, D_C), cache_mem_v[0].reshape(db * MEM_LEN, D_C), 0, 0),
        state_conv_a[0], (state_mlstm_c[0], state_mlstm_n[0], m0), state_ffn_conv[0],
        w, pa, nseq=db, seq=SAMPLE_ROWS, valid_len=dseq)
    ys = ys.reshape(db, SAMPLE_ROWS, D_MODEL)[:, :dseq]
    proj_s = proj_s.reshape(db, SAMPLE_ROWS, N_IN)
    up_s = up_s.reshape(db, SAMPLE_ROWS, 2 * D_FF)
    vn_s = vn_s.reshape(db, SAMPLE_ROWS, D_B)[:, :dseq]

    kv = kv.reshape(nb, MEM_LEN, 2 * D_C)
    return (
        yp.reshape(nb, seq, D_MODEL), ys,
        proj_p[:, seq - (CONV_A - 1):, :D_MODEL][None], c_p[None], n_p[None], m_p[None],
        kv[:, :, :D_C].reshape(1, nb, MEM_LEN, H_C, DH_C), kv[:, :, D_C:].reshape(1, nb, MEM_LEN, H_C, DH_C),
        up_p[:, seq - (CONV_F - 1):, :][None],
        proj_s[:, dseq - (CONV_A - 1):dseq, :D_MODEL][None], c_s[None], n_s[None], m_s[None],
        vn_s[None], up_s[:, dseq - (CONV_F - 1):dseq, :][None],
    )
```

```python
import functools

import jax
import jax.numpy as jnp
from jax import lax
from jax.experimental import pallas as pl
from jax.experimental.pallas import tpu as pltpu

F32 = jnp.float32
BF16 = jnp.bfloat16

D_MODEL = 2048
H_A = 4
DH_A = 512
QKV_BLOCK = 4
CONV_A = 4
CONV_F = 3
D_B = 1024
G_B = 4
DG_B = 256
H_C = 4
DH_C = 256
D_C = 1024
MEM_LEN = 256
D_FF = 6144
N_IN = 13312
GATE_COL0 = 2 * D_MODEL + 2 * D_B + D_C
EPS = 1e-6
NEG_INF = -1e30

CHUNK = 128
BD_BLOCK = 256
GATE_LANES = 128
SUBLANES = 8
V7X_VMEM_BYTES = 64 * 1024 * 1024
VMEM_LIMIT = V7X_VMEM_BYTES - 8 * 1024 * 1024


def _cparams(*sem):
    return pltpu.CompilerParams(dimension_semantics=sem, vmem_limit_bytes=VMEM_LIMIT)


def _resident(shape):
    return pl.BlockSpec(shape, lambda *_: (0,) * len(shape), pipeline_mode=pl.Buffered(1))


def _rms(x):
    return x * lax.rsqrt(jnp.mean(x * x, axis=-1, keepdims=True) + EPS)


def _rms_kernel(x_ref, g_ref, o_ref):
    o_ref[...] = (_rms(x_ref[...]) * g_ref[...]).astype(o_ref.dtype)


def rms_cast(x, g, tile=512):
    r, d = x.shape
    return pl.pallas_call(
        _rms_kernel,
        grid=(r // tile,),
        in_specs=[pl.BlockSpec((tile, d), lambda i: (i, 0)),
                  pl.BlockSpec((1, d), lambda i: (0, 0))],
        out_specs=pl.BlockSpec((tile, d), lambda i: (i, 0)),
        out_shape=jax.ShapeDtypeStruct((r, d), BF16),
        compiler_params=_cparams("parallel"),
        name="rms_cast",
    )(x, g.reshape(1, d))


def _mm_kernel(a_ref, w_ref, o_ref):
    o_ref[...] = jnp.dot(a_ref[...].astype(BF16), w_ref[...].astype(BF16),
                         preferred_element_type=F32)


def matmul(a, w, bm, bn):
    r, k = a.shape
    n = w.shape[1]
    bm = min(bm, r)
    return pl.pallas_call(
        _mm_kernel,
        grid=(n // bn, r // bm),
        in_specs=[pl.BlockSpec((bm, k), lambda j, i: (i, 0)),
                  pl.BlockSpec((k, bn), lambda j, i: (0, j))],
        out_specs=pl.BlockSpec((bm, bn), lambda j, i: (i, j)),
        out_shape=jax.ShapeDtypeStruct((r, n), F32),
        compiler_params=_cparams("parallel", "arbitrary"),
        name="matmul",
    )(a, w)


def _log_sigmoid(x):
    return jnp.minimum(x, 0.0) - jnp.log1p(jnp.exp(-jnp.abs(x)))


def _layer_norm_rows(x):
    xc = x - jnp.mean(x, axis=-1, keepdims=True)
    return xc * lax.rsqrt(jnp.mean(xc * xc, axis=-1, keepdims=True) + EPS)


def _branch_a_kernel(*refs, rows_in, nchunk, zero_state):
    if zero_state:
        (xm_ref, z_ref, st_ref, cw_ref, cb_ref, wq_ref, wk_ref, wv_ref, wif_ref, bif_ref,
         nh_ref, sk_ref, ya_ref, c_ref, n_ref, m_ref, ext_ref) = refs
    else:
        (xm_ref, z_ref, st_ref, c0_ref, n0_ref, m0_ref, cw_ref, cb_ref, wq_ref, wk_ref, wv_ref,
         wif_ref, bif_ref, nh_ref, sk_ref, ya_ref, c_ref, n_ref, m_ref, ext_ref) = refs
    lc = CHUNK
    carried = nchunk > 1
    chunk = pl.program_id(1)

    def init_conv():
        ext_ref[0:SUBLANES, :] = st_ref[0]

    if carried:
        @pl.when(chunk == 0)
        def _init():
            init_conv()
            if zero_state:
                c_ref[...] = jnp.zeros(c_ref.shape, F32)
                n_ref[...] = jnp.zeros(n_ref.shape, F32)
                m_ref[...] = jnp.zeros(m_ref.shape, F32)
            else:
                c_ref[...] = c0_ref[...]
                n_ref[...] = n0_ref[...]
                m_ref[...] = m0_ref[...]
        cin_ref, nin_ref, min_ref = c_ref, n_ref, m_ref
    else:
        init_conv()
        cin_ref, nin_ref, min_ref = c0_ref, n0_ref, m0_ref

    if rows_in < lc:
        ext_ref[SUBLANES + rows_in:SUBLANES + lc, :] = jnp.zeros((lc - rows_in, D_MODEL), F32)
    ext_ref[SUBLANES:SUBLANES + rows_in, :] = xm_ref[0]
    taps = [ext_ref[pl.ds(SUBLANES - (CONV_A - 1) + j, lc), :] for j in range(CONV_A)]
    cw = cw_ref[...]
    y = cb_ref[...] + taps[0] * cw[0:1, :]
    for j in range(1, CONV_A):
        y = y + taps[j] * cw[j:j + 1, :]
    xconv = y * jax.nn.sigmoid(y)
    xm = taps[CONV_A - 1]
    if carried:
        ext_ref[0:SUBLANES, :] = ext_ref[lc:lc + SUBLANES, :]

    def blockdiag(xb, w_ref):
        nblk = D_MODEL // BD_BLOCK
        return jnp.concatenate(
            [jnp.dot(xb[:, j * BD_BLOCK:(j + 1) * BD_BLOCK], w_ref[j], preferred_element_type=F32)
             for j in range(nblk)], axis=1)

    xcb = xconv.astype(BF16)
    qb = blockdiag(xcb, wq_ref).astype(BF16)
    kb = blockdiag(xcb, wk_ref).astype(BF16)
    vb = blockdiag(xm.astype(BF16), wv_ref).astype(BF16)
    if_pre = (jnp.dot(qb, wif_ref[0:D_MODEL, :], preferred_element_type=F32)
              + jnp.dot(kb, wif_ref[D_MODEL:2 * D_MODEL, :], preferred_element_type=F32)
              + jnp.dot(vb, wif_ref[2 * D_MODEL:3 * D_MODEL, :], preferred_element_type=F32)
              + bif_ref[...])
    log_f_all = _log_sigmoid(if_pre)

    lane = lax.broadcasted_iota(jnp.int32, (1, GATE_LANES), 1)
    r = lax.broadcasted_iota(jnp.int32, (lc, lc), 0)
    s_idx = lax.broadcasted_iota(jnp.int32, (lc, lc), 1)
    tril = s_idx <= r
    eye = s_idx == r
    row_valid = lax.broadcasted_iota(jnp.int32, (lc, 1), 0) < rows_in
    scale = DH_A ** -0.5

    m_all = min_ref[0, 0:1, :]
    m_out = m_all
    for h in range(H_A):
        hs = slice(h * DH_A, (h + 1) * DH_A)
        qh, kh, vh = qb[:, hs], kb[:, hs], vb[:, hs]
        li_col = jnp.sum(jnp.where(lane == h, if_pre, 0.0), axis=1, keepdims=True)
        lf_col = jnp.sum(jnp.where(lane == H_A + h, log_f_all, 0.0), axis=1, keepdims=True)
        if rows_in < lc:
            li_col = jnp.where(row_valid, li_col, NEG_INF)
            lf_col = jnp.where(row_valid, lf_col, 0.0)
        m_prev = jnp.sum(jnp.where(lane == h, m_all, 0.0), axis=1, keepdims=True)

        b_row = jnp.sum(jnp.where(r <= s_idx, lf_col, 0.0), axis=0, keepdims=True)
        li_row = jnp.sum(jnp.where(eye, li_col, 0.0), axis=0, keepdims=True)
        b_col = jnp.sum(jnp.where(eye, b_row, 0.0), axis=1, keepdims=True)

        d = jnp.where(tril, b_col - b_row + li_row, NEG_INF)
        inter = m_prev + b_col
        m_t = jnp.maximum(jnp.max(d, axis=1, keepdims=True), inter)
        w = jnp.exp(d - m_t)
        g = jnp.exp(inter - m_t)
        qk = lax.dot_general(qh, kh, (((1,), (1,)), ((), ())), preferred_element_type=F32)
        s = qk * (w * scale)
        c_h = cin_ref[0, h]
        n_h = nin_ref[0, h:h + 1, :]
        num = (g * jnp.dot(qh, c_h.astype(BF16), preferred_element_type=F32)
               + jnp.dot(s.astype(BF16), vh, preferred_element_type=F32))
        den = (g * jnp.sum(qh.astype(F32) * n_h, axis=1, keepdims=True)
               + jnp.sum(s, axis=1, keepdims=True))
        ht = num / jnp.maximum(jnp.abs(den), jnp.exp(-m_t))

        b_last = jnp.sum(lf_col, axis=0, keepdims=True)
        a_col = b_last - b_col + li_col
        m_new = jnp.maximum(m_prev + b_last, jnp.max(a_col, axis=0, keepdims=True))
        decay = jnp.exp(m_prev + b_last - m_new)
        kw = kh.astype(F32) * (jnp.exp(a_col - m_new) * scale)
        kwt = jnp.transpose(kw).astype(BF16)
        c_ref[0, h] = decay * c_h + jnp.dot(kwt, vh, preferred_element_type=F32)
        n_ref[0, h:h + 1, :] = decay * n_h + jnp.sum(kw, axis=0, keepdims=True)
        m_out = jnp.where(lane == h, m_new, m_out)

        hn = _layer_norm_rows(ht[0:rows_in]) * nh_ref[:, hs]
        ya = jax.nn.sigmoid(z_ref[0, :, hs]) * (hn + sk_ref[:, hs] * xconv[0:rows_in, hs])
        ya_ref[0, :, hs] = ya.astype(ya_ref.dtype)
    m_ref[0, 0:1, :] = m_out


def branch_a(proj, st8, states, p, *, nseq, nchunk, rows_in, out_dtype):
    r = proj.shape[0]
    zero_state = states is None
    assert not (zero_state and nchunk == 1)
    proj3 = proj.reshape(nseq * nchunk, rows_in, N_IN)
    row = lambda col: pl.BlockSpec((1, rows_in, D_MODEL), lambda b, c, col=col: (b * nchunk + c, 0, col))
    full = lambda shape: pl.BlockSpec(shape, lambda b, c: (0,) * len(shape))
    c_spec = pl.BlockSpec((1, H_A, DH_A, DH_A), lambda b, c: (b, 0, 0, 0))
    n_spec = pl.BlockSpec((1, H_A, DH_A), lambda b, c: (b, 0, 0))
    m_spec = pl.BlockSpec((1, SUBLANES, GATE_LANES), lambda b, c: (b, 0, 0))
    in_specs = [row(0), row(1), pl.BlockSpec((1, SUBLANES, D_MODEL), lambda b, c: (b, 0, 0))]
    args = [proj3, proj3, st8]
    if not zero_state:
        in_specs += [c_spec, n_spec, m_spec]
        args += list(states)
    nblk = D_MODEL // BD_BLOCK
    in_specs += [full((CONV_A, D_MODEL)), full((1, D_MODEL)),
                 full((nblk, BD_BLOCK, BD_BLOCK)), full((nblk, BD_BLOCK, BD_BLOCK)),
                 full((nblk, BD_BLOCK, BD_BLOCK)),
                 full((3 * D_MODEL, GATE_LANES)), full((1, GATE_LANES)),
                 full((1, D_MODEL)), full((1, D_MODEL))]
    args += [p["conv_a_w"], p["conv_a_b"], p["wq_bd"], p["wk_bd"], p["wv_bd"], p["w_if"], p["b_if"],
             p["norm_h_a"], p["skip_a"]]
    kern = functools.partial(_branch_a_kernel, rows_in=rows_in, nchunk=nchunk, zero_state=zero_state)
    ya, c_new, n_new, m_new = pl.pallas_call(
        kern,
        grid=(nseq, nchunk),
        in_specs=in_specs,
        out_specs=[pl.BlockSpec((1, rows_in, D_MODEL), lambda b, c: (b * nchunk + c, 0, 0)),
                   c_spec, n_spec, m_spec],
        out_shape=[jax.ShapeDtypeStruct((nseq * nchunk, rows_in, D_MODEL), out_dtype),
                   jax.ShapeDtypeStruct((nseq, H_A, DH_A, DH_A), F32),
                   jax.ShapeDtypeStruct((nseq, H_A, DH_A), F32),
                   jax.ShapeDtypeStruct((nseq, SUBLANES, GATE_LANES), F32)],
        scratch_shapes=[pltpu.VMEM((SUBLANES + CHUNK, D_MODEL), F32)],
        compiler_params=_cparams("parallel", "arbitrary"),
        name="branch_a",
    )(*args)
    return ya.reshape(r, D_MODEL), c_new, n_new, m_new[:, 0, :H_A]


def _gmlp_kernel(u_ref, v_ref, nv_ref, wmix_ref, bmix_ref, yb_ref, vn_ref):
    vn = _layer_norm_rows(jax.nn.gelu(v_ref[...])) * nv_ref[...]
    vn_ref[...] = vn
    vnb = vn.astype(BF16)
    bias = bmix_ref[...]
    lane = lax.broadcasted_iota(jnp.int32, (1, GATE_LANES), 1)
    mixed = []
    for g in range(G_B):
        b_col = jnp.sum(jnp.where(lane == g, bias, 0.0), axis=1, keepdims=True)
        mixed.append(jnp.dot(wmix_ref[g], vnb[:, g * DG_B:(g + 1) * DG_B],
                             preferred_element_type=F32) + b_col)
    yb_ref[...] = (jax.nn.gelu(u_ref[...]) * jnp.concatenate(mixed, axis=1)).astype(yb_ref.dtype)


def gmlp(proj, norm_v, w_s, b_s, seq):
    r = proj.shape[0]
    lc = min(seq, CHUNK)
    per = CHUNK // lc
    tril = jnp.tril(jnp.ones((lc, lc), bool))
    w_small = jnp.where(tril[None], w_s[:, :lc, :lc], 0.0)
    wmix = jnp.einsum("ab,gts->gatbs", jnp.eye(per, dtype=F32), w_small)
    wmix = wmix.reshape(G_B, CHUNK, CHUNK).astype(BF16)
    bmix = jnp.pad(jnp.tile(b_s[:, :lc].T, (per, 1)), ((0, 0), (0, GATE_LANES - G_B)))
    u_col = 2 * D_MODEL // D_B
    return pl.pallas_call(
        _gmlp_kernel,
        grid=(r // CHUNK,),
        in_specs=[pl.BlockSpec((CHUNK, D_B), lambda i: (i, u_col)),
                  pl.BlockSpec((CHUNK, D_B), lambda i: (i, u_col + 1)),
                  pl.BlockSpec((1, D_B), lambda i: (0, 0)),
                  pl.BlockSpec((G_B, CHUNK, CHUNK), lambda i: (0, 0, 0)),
                  pl.BlockSpec((CHUNK, GATE_LANES), lambda i: (0, 0))],
        out_specs=[pl.BlockSpec((CHUNK, D_B), lambda i: (i, 0)),
                   pl.BlockSpec((CHUNK, D_B), lambda i: (i, 0))],
        out_shape=[jax.ShapeDtypeStruct((r, D_B), BF16), jax.ShapeDtypeStruct((r, D_B), F32)],
        compiler_params=_cparams("parallel"),
        name="gmlp",
    )(proj, proj, norm_v.reshape(1, D_B), wmix, bmix)


def _xattn_kernel(q_ref, k_ref, v_ref, o_ref, *scratch, nseq, rq):
    if rq < SUBLANES:
        (qpad_ref,) = scratch
        qpad_ref[...] = jnp.zeros(qpad_ref.shape, F32)
    for i in range(nseq):
        if rq < SUBLANES:
            qpad_ref[0:rq, :] = q_ref[i]
            q = qpad_ref[...].astype(BF16)
        else:
            q = q_ref[i].astype(BF16)
        k = k_ref[i].astype(BF16)
        v = v_ref[i].astype(BF16)
        outs = []
        for h in range(H_C):
            hs = slice(h * DH_C, (h + 1) * DH_C)
            s = lax.dot_general(q[:, hs], k[:, hs], (((1,), (1,)), ((), ())),
                                preferred_element_type=F32) * (DH_C ** -0.5)
            e = jnp.exp(s - jnp.max(s, axis=-1, keepdims=True))
            prob = e / jnp.sum(e, axis=-1, keepdims=True)
            outs.append(jnp.dot(prob.astype(BF16), v[:, hs], preferred_element_type=F32))
        o = jnp.concatenate(outs, axis=1)
        o_ref[i] = o[0:rq].astype(o_ref.dtype)


def xattn(proj, k_arr, v_arr, k_col, v_col, *, nbatch, nseq, rq, out_dtype):
    r = proj.shape[0]
    seq = r // nbatch
    nq = seq // rq
    assert nseq == 1 or nq == 1
    proj3 = proj.reshape(nbatch * nq, rq, N_IN)
    q_col = (2 * D_MODEL + 2 * D_B) // D_C
    kern = functools.partial(_xattn_kernel, nseq=nseq, rq=rq)
    scratch = [pltpu.VMEM((SUBLANES, D_C), F32)] if rq < SUBLANES else []
    out = pl.pallas_call(
        kern,
        grid=(nbatch // nseq, nq),
        in_specs=[pl.BlockSpec((nseq, rq, D_C), lambda g, t: (g * nq + t, 0, q_col)),
                  pl.BlockSpec((nseq, MEM_LEN, D_C), lambda g, t: (g, 0, k_col)),
                  pl.BlockSpec((nseq, MEM_LEN, D_C), lambda g, t: (g, 0, v_col))],
        out_specs=pl.BlockSpec((nseq, rq, D_C), lambda g, t: (g * nq + t, 0, 0)),
        out_shape=jax.ShapeDtypeStruct((nbatch * nq, rq, D_C), out_dtype),
        scratch_shapes=scratch,
        compiler_params=_cparams("parallel", "arbitrary"),
        name="xattn",
    )(proj3, k_arr, v_arr)
    return out.reshape(r, D_C)


def _mix_out_kernel(ya_ref, yb_ref, yc_ref, ga0, ga1, gb0, gb1, gc0, gc1, x_ref,
                    wa_ref, wb_ref, wc_ref, wo_ref, g1_ref, g2_ref, x1_ref, h2_ref):
    def branch(y_ref, w_ref, lo, hi):
        gate = jnp.concatenate([lo[...], hi[...]], axis=1)
        return jax.nn.sigmoid(gate) * jnp.dot(y_ref[...].astype(BF16), w_ref[...],
                                              preferred_element_type=F32)

    merged = (branch(ya_ref, wa_ref, ga0, ga1) + branch(yb_ref, wb_ref, gb0, gb1)
              + branch(yc_ref, wc_ref, gc0, gc1))
    o = jnp.dot(merged.astype(BF16), wo_ref[...], preferred_element_type=F32)
    x1 = x_ref[...] + _rms(o) * g1_ref[...]
    x1_ref[...] = x1
    h2_ref[...] = (_rms(x1) * g2_ref[...]).astype(h2_ref.dtype)


def mix_out(ya, yb, yc, proj, x, w_a, w_b, w_c, w_o, g_post, g_pre, bm=256):
    r = x.shape[0]
    half = D_MODEL // 2
    g0 = GATE_COL0 // half
    gate = lambda k: pl.BlockSpec((bm, half), lambda i, k=k: (i, g0 + k))
    rowspec = lambda d: pl.BlockSpec((bm, d), lambda i: (i, 0))
    vec = pl.BlockSpec((1, D_MODEL), lambda i: (0, 0))
    return pl.pallas_call(
        _mix_out_kernel,
        grid=(r // bm,),
        in_specs=[rowspec(D_MODEL), rowspec(D_B), rowspec(D_C)] + [gate(k) for k in range(6)]
                 + [rowspec(D_MODEL), _resident((D_MODEL, D_MODEL)), _resident((D_B, D_MODEL)),
                    _resident((D_C, D_MODEL)), _resident((D_MODEL, D_MODEL)), vec, vec],
        out_specs=[rowspec(D_MODEL), rowspec(D_MODEL)],
        out_shape=[jax.ShapeDtypeStruct((r, D_MODEL), F32), jax.ShapeDtypeStruct((r, D_MODEL), BF16)],
        compiler_params=_cparams("parallel"),
        name="mix_out",
    )(ya, yb, yc, proj, proj, proj, proj, proj, proj, x, w_a, w_b, w_c, w_o,
      g_post.reshape(1, D_MODEL), g_pre.reshape(1, D_MODEL))


def _gated_act(gate, val):
    return jax.nn.gelu(gate) * val


def _ffn_up_act_kernel(a_ref, wg_ref, wv_ref, cwg_ref, cwv_ref, cbg_ref, cbv_ref,
                       o_ref, sg_ref, sv_ref, ext_ref, *, bm):
    tile = pl.program_id(2)
    a = a_ref[...]

    def conv(which, w_ref, cw_ref, cb_ref, st_ref):
        u = jnp.dot(a, w_ref[...].astype(BF16), preferred_element_type=F32)

        @pl.when(tile == 0)
        def _():
            ext_ref[which, 0:SUBLANES, :] = jnp.zeros((SUBLANES, u.shape[1]), F32)

        ext_ref[which, SUBLANES:SUBLANES + bm, :] = u
        cw = cw_ref[...]
        y = cb_ref[...] + u * cw[CONV_F - 1:CONV_F, :]
        for j in range(CONV_F - 1):
            y = y + ext_ref[which, pl.ds(SUBLANES - (CONV_F - 1) + j, bm), :] * cw[j:j + 1, :]
        tail = u[bm - SUBLANES:bm, :]
        ext_ref[which, 0:SUBLANES, :] = tail
        st_ref[0] = tail
        return y

    gate = conv(0, wg_ref, cwg_ref, cbg_ref, sg_ref)
    val = conv(1, wv_ref, cwv_ref, cbv_ref, sv_ref)
    o_ref[...] = _gated_act(gate, val).astype(o_ref.dtype)


def ffn_up_act(h2, w_up, conv_w, conv_b, *, nseq, bm=1024, bn=512):
    r = h2.shape[0]
    tiles = r // nseq // bm
    ncol = D_FF // bn
    wspec = lambda off: pl.BlockSpec((D_MODEL, bn), lambda j, b, t, off=off: (0, off + j))
    cwspec = lambda off: pl.BlockSpec((CONV_F, bn), lambda j, b, t, off=off: (0, off + j))
    cbspec = lambda off: pl.BlockSpec((1, bn), lambda j, b, t, off=off: (0, off + j))
    st_spec = pl.BlockSpec((1, SUBLANES, bn), lambda j, b, t: (b, 0, j))
    kern = functools.partial(_ffn_up_act_kernel, bm=bm)
    return pl.pallas_call(
        kern,
        grid=(ncol, nseq, tiles),
        in_specs=[pl.BlockSpec((bm, D_MODEL), lambda j, b, t: (b * tiles + t, 0)),
                  wspec(0), wspec(ncol), cwspec(0), cwspec(ncol), cbspec(0), cbspec(ncol)],
        out_specs=[pl.BlockSpec((bm, bn), lambda j, b, t: (b * tiles + t, j)), st_spec, st_spec],
        out_shape=[jax.ShapeDtypeStruct((r, D_FF), BF16),
                   jax.ShapeDtypeStruct((nseq, SUBLANES, D_FF), F32),
                   jax.ShapeDtypeStruct((nseq, SUBLANES, D_FF), F32)],
        scratch_shapes=[pltpu.VMEM((2, SUBLANES + bm, bn), F32)],
        compiler_params=_cparams("parallel", "arbitrary", "arbitrary"),
        name="ffn_up_act",
    )(h2, w_up, w_up, conv_w, conv_w, conv_b, conv_b)


def _ffn_act_kernel(g_ref, v_ref, sg_ref, sv_ref, wg_ref, wv_ref, bg_ref, bv_ref, o_ref, ext_ref,
                    *, nseq, rows):
    def conv(which, i, x_ref, st_ref, w_ref, b_ref):
        ext_ref[which, 0:SUBLANES, :] = st_ref[i]
        ext_ref[which, SUBLANES:SUBLANES + rows, :] = x_ref[i]
        w = w_ref[...]
        y = b_ref[...]
        for j in range(CONV_F):
            y = y + ext_ref[which, pl.ds(SUBLANES - (CONV_F - 1) + j, rows), :] * w[j:j + 1, :]
        return y

    for i in range(nseq):
        gate = conv(0, i, g_ref, sg_ref, wg_ref, bg_ref)
        val = conv(1, i, v_ref, sv_ref, wv_ref, bv_ref)
        o_ref[i] = _gated_act(gate, val).astype(o_ref.dtype)


def ffn_act(up, st8, conv_w, conv_b, *, nbatch, nseq, bc=2048):
    r = up.shape[0]
    rows = r // nbatch
    up3 = up.reshape(nbatch, rows, 2 * D_FF)
    ncol = D_FF // bc
    x_spec = lambda off: pl.BlockSpec((nseq, rows, bc), lambda g, j, off=off: (g, 0, off + j))
    s_spec = lambda off: pl.BlockSpec((nseq, SUBLANES, bc), lambda g, j, off=off: (g, 0, off + j))
    w_spec = lambda off: pl.BlockSpec((CONV_F, bc), lambda g, j, off=off: (0, off + j))
    b_spec = lambda off: pl.BlockSpec((1, bc), lambda g, j, off=off: (0, off + j))
    kern = functools.partial(_ffn_act_kernel, nseq=nseq, rows=rows)
    out = pl.pallas_call(
        kern,
        grid=(nbatch // nseq, ncol),
        in_specs=[x_spec(0), x_spec(ncol), s_spec(0), s_spec(ncol), w_spec(0), w_spec(ncol),
                  b_spec(0), b_spec(ncol)],
        out_specs=pl.BlockSpec((nseq, rows, bc), lambda g, j: (g, 0, j)),
        out_shape=jax.ShapeDtypeStruct((nbatch, rows, D_FF), F32),
        scratch_shapes=[pltpu.VMEM((2, 2 * SUBLANES, bc), F32)],
        compiler_params=_cparams("parallel", "parallel"),
        name="ffn_act",
    )(up3, up3, st8, st8, conv_w, conv_w, conv_b, conv_b)
    return out.reshape(r, D_FF)


def _ffn_down_kernel(a_ref, w_ref, x_ref, g_ref, y_ref):
    f = jnp.dot(a_ref[...].astype(BF16), w_ref[...], preferred_element_type=F32)
    y_ref[...] = x_ref[...] + _rms(f) * g_ref[...]


def ffn_down_post(act, w_down, x1, g, bm=256):
    r = x1.shape[0]
    return pl.pallas_call(
        _ffn_down_kernel,
        grid=(r // bm,),
        in_specs=[pl.BlockSpec((bm, D_FF), lambda i: (i, 0)), _resident((D_FF, D_MODEL)),
                  pl.BlockSpec((bm, D_MODEL), lambda i: (i, 0)),
                  pl.BlockSpec((1, D_MODEL), lambda i: (0, 0))],
        out_specs=pl.BlockSpec((bm, D_MODEL), lambda i: (i, 0)),
        out_shape=jax.ShapeDtypeStruct((r, D_MODEL), F32),
        compiler_params=_cparams("parallel"),
        name="ffn_down_post",
    )(act, w_down, x1, g.reshape(1, D_MODEL))


def _blockdiag_tiles(w):
    per = BD_BLOCK // QKV_BLOCK
    w = w.reshape(D_MODEL // BD_BLOCK, per, QKV_BLOCK, QKV_BLOCK)
    eye = jnp.eye(per, dtype=w.dtype)
    t = jnp.einsum("jnik,nm->jnimk", w, eye)
    return t.reshape(D_MODEL // BD_BLOCK, BD_BLOCK, BD_BLOCK).astype(BF16)


def _prep_params(conv_a_w, conv_a_b, w_q_a, w_k_a, w_v_a, w_if_a, b_if_a, norm_h_a, skip_a):
    pad = GATE_LANES - 2 * H_A
    return {
        "conv_a_w": conv_a_w,
        "conv_a_b": conv_a_b.reshape(1, D_MODEL),
        "wq_bd": _blockdiag_tiles(w_q_a),
        "wk_bd": _blockdiag_tiles(w_k_a),
        "wv_bd": _blockdiag_tiles(w_v_a),
        "w_if": jnp.pad(w_if_a, ((0, 0), (0, pad))).astype(BF16),
        "b_if": jnp.pad(b_if_a, (0, pad)).reshape(1, GATE_LANES),
        "norm_h_a": norm_h_a.reshape(1, D_MODEL),
        "skip_a": skip_a.reshape(1, D_MODEL),
    }


def _tail_rows(state, k):
    return jnp.pad(state, ((0, 0), (SUBLANES - k, 0), (0, 0)))


def _trunk(x, mem, conv_prev, mlstm_state, ffn_prev, w, pa, *, nseq, seq):
    prompt = seq >= CHUNK
    k_arr, v_arr, k_col, v_col = mem

    h = rms_cast(x, w["norm_mix_pre"])
    proj = matmul(h, w["w_in"], 1024, 1024)

    st8 = _tail_rows(conv_prev, CONV_A - 1)
    yb, vn = gmlp(proj, w["norm_v_b"], w["w_s_b"], w["b_s_b"], seq)
    if prompt:
        ya, c_new, n_new, m_new = branch_a(proj, st8, mlstm_state, pa, nseq=nseq, nchunk=seq // CHUNK,
                                           rows_in=CHUNK, out_dtype=BF16)
        yc = xattn(proj, k_arr, v_arr, k_col, v_col, nbatch=nseq, nseq=1, rq=256, out_dtype=BF16)
    else:
        ya, c_new, n_new, m_new = branch_a(proj, st8, mlstm_state, pa, nseq=nseq, nchunk=1,
                                           rows_in=seq, out_dtype=F32)
        yc = xattn(proj, k_arr, v_arr, k_col, v_col, nbatch=nseq, nseq=4, rq=seq, out_dtype=F32)

    x1, h2 = mix_out(ya, yb, yc, proj, x, w["w_br_a"], w["w_br_b"], w["w_br_c"], w["w_out"],
                     w["norm_mix_post"], w["norm_ffn_pre"])

    conv_b = w["conv_ffn_b"].reshape(1, 2 * D_FF)
    if prompt:
        act, tail_g, tail_v = ffn_up_act(h2, w["w_ffn_up"], w["conv_ffn_w"], conv_b, nseq=nseq)
        ffn_new = jnp.concatenate([tail_g, tail_v], axis=-1)[:, SUBLANES - (CONV_F - 1):]
    else:
        up = matmul(h2, w["w_ffn_up"], 1024, 1024)
        act = ffn_act(up, _tail_rows(ffn_prev, CONV_F - 1), w["conv_ffn_w"], conv_b,
                      nbatch=nseq, nseq=32)
        ffn_new = up.reshape(nseq, seq, 2 * D_FF)[:, seq - (CONV_F - 1):]
    y = ffn_down_post(act, w["w_ffn_down"], x1, w["norm_ffn_post"])
    conv_new = proj.reshape(nseq, seq, N_IN)[:, seq - (CONV_A - 1):, :D_MODEL]
    return y, conv_new, ffn_new, vn, c_new, n_new, m_new


def kernel(x_prompt, x_sample, state_conv_a, state_mlstm_c, state_mlstm_n, state_mlstm_m, cache_mem_k, cache_mem_v, state_ffn_conv, mem_prompt, norm_mix_pre, w_in, conv_a_w, conv_a_b, w_q_a, w_k_a, w_v_a, w_if_a, b_if_a, norm_h_a, skip_a, w_br_a, norm_v_b, w_s_b, b_s_b, w_br_b, norm_mem, w_mem_kv, w_br_c, w_out, norm_mix_post, norm_ffn_pre, w_ffn_up, conv_ffn_w, conv_ffn_b, w_ffn_down, norm_ffn_post):
    nb, seq, _ = x_prompt.shape
    db, dseq, _ = x_sample.shape
    w = {
        "norm_mix_pre": norm_mix_pre[0], "w_in": w_in[0], "norm_v_b": norm_v_b[0],
        "w_s_b": w_s_b[0], "b_s_b": b_s_b[0],
        "w_br_a": w_br_a[0].astype(BF16), "w_br_b": w_br_b[0].astype(BF16),
        "w_br_c": w_br_c[0].astype(BF16), "w_out": w_out[0].astype(BF16),
        "norm_mix_post": norm_mix_post[0], "norm_ffn_pre": norm_ffn_pre[0], "w_ffn_up": w_ffn_up[0],
        "conv_ffn_w": conv_ffn_w[0], "conv_ffn_b": conv_ffn_b[0],
        "w_ffn_down": w_ffn_down[0].astype(BF16), "norm_ffn_post": norm_ffn_post[0],
    }
    pa = _prep_params(conv_a_w[0], conv_a_b[0], w_q_a[0], w_k_a[0], w_v_a[0], w_if_a[0], b_if_a[0],
                      norm_h_a[0], skip_a[0])

    mem_h = rms_cast(mem_prompt.reshape(nb * MEM_LEN, D_MODEL), norm_mem[0])
    kv = matmul(mem_h, w_mem_kv[0], 1024, 1024).reshape(nb, MEM_LEN, 2 * D_C)
    yp, conv_p, ffn_p, _, c_p, n_p, m_p = _trunk(
        x_prompt.reshape(nb * seq, D_MODEL), (kv, kv, 0, 1),
        jnp.zeros((nb, CONV_A - 1, D_MODEL), F32), None, None, w, pa, nseq=nb, seq=seq)

    m0 = jnp.pad(state_mlstm_m[0][:, None, :], ((0, 0), (0, SUBLANES - 1), (0, GATE_LANES - H_A)))
    ys, conv_s, ffn_s, vn_s, c_s, n_s, m_s = _trunk(
        x_sample.reshape(db * dseq, D_MODEL),
        (cache_mem_k[0].reshape(db, MEM_LEN, D_C), cache_mem_v[0].reshape(db, MEM_LEN, D_C), 0, 0),
        state_conv_a[0], (state_mlstm_c[0], state_mlstm_n[0], m0), state_ffn_conv[0],
        w, pa, nseq=db, seq=dseq)

    return (
        yp.reshape(nb, seq, D_MODEL), ys.reshape(db, dseq, D_MODEL),
        conv_p[None], c_p[None], n_p[None], m_p[None],
        kv[:, :, :D_C].reshape(1, nb, MEM_LEN, H_C, DH_C), kv[:, :, D_C:].reshape(1, nb, MEM_LEN, H_C, DH_C),
        ffn_p[None],
        conv_s[None], c_s[None], n_s[None], m_s[None],
        vn_s.reshape(1, db, dseq, D_B), ffn_s[None],
    )
```

```python
import functools

import jax
import jax.numpy as jnp
from jax import lax
from jax.experimental import pallas as pl
from jax.experimental.pallas import tpu as pltpu

F32 = jnp.float32
BF16 = jnp.bfloat16

D_MODEL = 2048
H_A = 4
DH_A = 512
QKV_BLOCK = 4
CONV_A = 4
CONV_F = 3
D_B = 1024
G_B = 4
DG_B = 256
H_C = 4
DH_C = 256
D_C = 1024
MEM_LEN = 256
D_FF = 6144
N_IN = 13312
GATE_COL0 = 2 * D_MODEL + 2 * D_B + D_C
EPS = 1e-6
NEG_INF = -1e30

CHUNK = 128
BD_BLOCK = 256
GATE_LANES = 128
SUBLANES = 8
V7X_VMEM_BYTES = 64 * 1024 * 1024
VMEM_LIMIT = V7X_VMEM_BYTES - 8 * 1024 * 1024


def _cparams(*sem):
    return pltpu.CompilerParams(dimension_semantics=sem, vmem_limit_bytes=VMEM_LIMIT)


def _resident(shape):
    return pl.BlockSpec(shape, lambda *_: (0,) * len(shape), pipeline_mode=pl.Buffered(1))


def _rms(x):
    return x * lax.rsqrt(jnp.mean(x * x, axis=-1, keepdims=True) + EPS)


def _unpack(tile, i, valid):
    return tile if i == 0 else pltpu.roll(tile, SUBLANES - i * valid, axis=0)


def _pack(parts, valid):
    row = lax.broadcasted_iota(jnp.int32, (SUBLANES, 1), 0)
    out = parts[0]
    for i in range(1, len(parts)):
        out = jnp.where(row < i * valid, out, pltpu.roll(parts[i], i * valid, axis=0))
    return out


def _rms_kernel(x_ref, g_ref, o_ref):
    o_ref[...] = (_rms(x_ref[...]) * g_ref[...]).astype(o_ref.dtype)


def rms_cast(x, g, tile=512):
    r, d = x.shape
    return pl.pallas_call(
        _rms_kernel,
        grid=(r // tile,),
        in_specs=[pl.BlockSpec((tile, d), lambda i: (i, 0)),
                  pl.BlockSpec((1, d), lambda i: (0, 0))],
        out_specs=pl.BlockSpec((tile, d), lambda i: (i, 0)),
        out_shape=jax.ShapeDtypeStruct((r, d), BF16),
        compiler_params=_cparams("parallel"),
        name="rms_cast",
    )(x, g.reshape(1, d))


def _mm_kernel(a_ref, w_ref, o_ref):
    o_ref[...] = jnp.dot(a_ref[...].astype(BF16), w_ref[...].astype(BF16),
                         preferred_element_type=F32)


def matmul(a, w, bm, bn):
    r, k = a.shape
    n = w.shape[1]
    bm = min(bm, r)
    return pl.pallas_call(
        _mm_kernel,
        grid=(n // bn, r // bm),
        in_specs=[pl.BlockSpec((bm, k), lambda j, i: (i, 0)),
                  pl.BlockSpec((k, bn), lambda j, i: (0, j))],
        out_specs=pl.BlockSpec((bm, bn), lambda j, i: (i, j)),
        out_shape=jax.ShapeDtypeStruct((r, n), F32),
        compiler_params=_cparams("parallel", "arbitrary"),
        name="matmul",
    )(a, w)


def _log_sigmoid(x):
    return jnp.minimum(x, 0.0) - jnp.log1p(jnp.exp(-jnp.abs(x)))


def _layer_norm_rows(x):
    xc = x - jnp.mean(x, axis=-1, keepdims=True)
    return xc * lax.rsqrt(jnp.mean(xc * xc, axis=-1, keepdims=True) + EPS)


def _branch_a_kernel(*refs, nsub, tq, valid, nchunk, zero_state):
    n_in = 3 if zero_state else 6
    xm_ref, z_ref, st_ref = refs[:3]
    (cw_ref, cb_ref, wq_ref, wk_ref, wv_ref, wif_ref, bif_ref, nh_ref, sk_ref,
     ya_ref, c_ref, n_ref, m_ref, ext_ref) = refs[n_in:]
    lc = CHUNK
    packed = valid < tq
    carried = nchunk > 1
    chunk = pl.program_id(1)

    if carried:
        @pl.when(chunk == 0)
        def _init():
            for i in range(nsub):
                ext_ref[i, 0:SUBLANES, :] = st_ref[i]
            if zero_state:
                c_ref[...] = jnp.zeros(c_ref.shape, F32)
                n_ref[...] = jnp.zeros(n_ref.shape, F32)
                m_ref[...] = jnp.zeros(m_ref.shape, F32)
            else:
                c_ref[...] = refs[3][...]
                n_ref[...] = refs[4][...]
                m_ref[...] = refs[5][...]
        cin_ref, nin_ref, min_ref = c_ref, n_ref, m_ref
    else:
        cin_ref, nin_ref, min_ref = refs[3:6]

    lane = lax.broadcasted_iota(jnp.int32, (1, GATE_LANES), 1)
    r = lax.broadcasted_iota(jnp.int32, (tq, lc), 0)
    s_idx = lax.broadcasted_iota(jnp.int32, (tq, lc), 1)
    tril = s_idx <= r
    eye = s_idx == r
    row_valid = lax.broadcasted_iota(jnp.int32, (tq, 1), 0) < valid
    lane_valid = lax.broadcasted_iota(jnp.int32, (1, lc), 1) < valid
    scale = DH_A ** -0.5
    cw = cw_ref[...]

    def blockdiag(xb, w_ref):
        nblk = D_MODEL // BD_BLOCK
        return jnp.concatenate(
            [jnp.dot(xb[:, j * BD_BLOCK:(j + 1) * BD_BLOCK], w_ref[j], preferred_element_type=F32)
             for j in range(nblk)], axis=1)

    def pad_rows(x):
        if tq == lc:
            return x
        return jnp.concatenate([x, jnp.zeros((lc - tq, x.shape[1]), x.dtype)], axis=0)

    packed_out = []
    for i in range(nsub):
        if packed:
            x_in = _unpack(xm_ref[0], i, valid)
            z_in = _unpack(z_ref[0], i, valid)
        else:
            x_in = xm_ref[i]
            z_in = z_ref[i]

        if not carried:
            ext_ref[i, 0:SUBLANES, :] = st_ref[i]
        ext_ref[i, SUBLANES:SUBLANES + tq, :] = x_in
        y = cb_ref[...] + x_in * cw[CONV_A - 1:CONV_A, :]
        for j in range(CONV_A - 1):
            y = y + ext_ref[i, pl.ds(SUBLANES - (CONV_A - 1) + j, tq), :] * cw[j:j + 1, :]
        xconv = y * jax.nn.sigmoid(y)
        if carried:
            ext_ref[i, 0:SUBLANES, :] = x_in[tq - SUBLANES:tq, :]

        xcb = xconv.astype(BF16)
        q32 = blockdiag(xcb, wq_ref)
        k32 = blockdiag(xcb, wk_ref)
        v32 = blockdiag(x_in.astype(BF16), wv_ref)
        qb, kb, vb = q32.astype(BF16), k32.astype(BF16), v32.astype(BF16)
        if_pre = (jnp.dot(qb, wif_ref[0:D_MODEL, :], preferred_element_type=F32)
                  + jnp.dot(kb, wif_ref[D_MODEL:2 * D_MODEL, :], preferred_element_type=F32)
                  + jnp.dot(vb, wif_ref[2 * D_MODEL:3 * D_MODEL, :], preferred_element_type=F32)
                  + bif_ref[...])
        log_f_all = _log_sigmoid(if_pre)
        kpad = pad_rows(k32).astype(BF16)
        vpad = pad_rows(v32).astype(BF16)

        m_all = min_ref[i, 0:1, :]
        m_out = m_all
        heads = []
        for h in range(H_A):
            hs = slice(h * DH_A, (h + 1) * DH_A)
            qh, kh, vh = qb[:, hs], kpad[:, hs], vpad[:, hs]
            li_col = jnp.sum(jnp.where(lane == h, if_pre, 0.0), axis=1, keepdims=True)
            lf_col = jnp.sum(jnp.where(lane == H_A + h, log_f_all, 0.0), axis=1, keepdims=True)
            if packed:
                li_col = jnp.where(row_valid, li_col, NEG_INF)
                lf_col = jnp.where(row_valid, lf_col, 0.0)
            m_prev = jnp.sum(jnp.where(lane == h, m_all, 0.0), axis=1, keepdims=True)

            b_row = jnp.sum(jnp.where(r <= s_idx, lf_col, 0.0), axis=0, keepdims=True)
            li_row = jnp.sum(jnp.where(eye, li_col, 0.0), axis=0, keepdims=True)
            if packed:
                li_row = jnp.where(lane_valid, li_row, NEG_INF)
            b_col = jnp.sum(jnp.where(eye, b_row, 0.0), axis=1, keepdims=True)

            d = jnp.where(tril, b_col - b_row + li_row, NEG_INF)
            inter = m_prev + b_col
            m_t = jnp.maximum(jnp.max(d, axis=1, keepdims=True), inter)
            w = jnp.exp(d - m_t)
            g = jnp.exp(inter - m_t)
            qk = lax.dot_general(qh, kh, (((1,), (1,)), ((), ())), preferred_element_type=F32)
            s = qk * (w * scale)
            c_h = cin_ref[i, h]
            n_h = nin_ref[i, h:h + 1, :]
            num = (g * jnp.dot(qh, c_h.astype(BF16), preferred_element_type=F32)
                   + jnp.dot(s.astype(BF16), vh, preferred_element_type=F32))
            den = (g * jnp.sum(q32[:, hs] * n_h, axis=1, keepdims=True)
                   + jnp.sum(s, axis=1, keepdims=True))
            ht = num / jnp.maximum(jnp.abs(den), jnp.exp(-m_t))

            b_last = jnp.sum(lf_col, axis=0, keepdims=True)
            a_col = b_last - b_col + li_col
            m_new = jnp.maximum(m_prev + b_last, jnp.max(a_col, axis=0, keepdims=True))
            decay = jnp.exp(m_prev + b_last - m_new)
            kw = k32[:, hs] * (jnp.exp(a_col - m_new) * scale)
            kwt = jnp.transpose(pad_rows(kw)).astype(BF16)
            c_ref[i, h] = decay * c_h + jnp.dot(kwt, vh, preferred_element_type=F32)
            n_ref[i, h:h + 1, :] = decay * n_h + jnp.sum(kw, axis=0, keepdims=True)
            m_out = jnp.where(lane == h, m_new, m_out)

            hn = _layer_norm_rows(ht) * nh_ref[:, hs]
            heads.append(jax.nn.sigmoid(z_in[:, hs]) * (hn + sk_ref[:, hs] * xconv[:, hs]))
        m_ref[i, 0:1, :] = m_out
        if packed:
            packed_out.append(heads)
        else:
            for h in range(H_A):
                ya_ref[i, :, h * DH_A:(h + 1) * DH_A] = heads[h].astype(ya_ref.dtype)
    if packed:
        for h in range(H_A):
            tile = _pack([heads[h] for heads in packed_out], valid)
            ya_ref[0, :, h * DH_A:(h + 1) * DH_A] = tile.astype(ya_ref.dtype)


def branch_a(proj, st8, states, p, *, nseq, seq, out_dtype):
    r = proj.shape[0]
    zero_state = states is None
    if seq >= CHUNK:
        nsub, tq, valid, nchunk = 2, CHUNK, CHUNK, seq // CHUNK
        proj3 = proj.reshape(nseq, seq, N_IN)
        row = lambda col: pl.BlockSpec((nsub, tq, D_MODEL), lambda g, c, col=col: (g, c, col))
        ya_shape = (nseq, seq, D_MODEL)
    else:
        nsub, tq, valid, nchunk = SUBLANES // seq, SUBLANES, seq, 1
        proj3 = proj.reshape(r // SUBLANES, SUBLANES, N_IN)
        row = lambda col: pl.BlockSpec((1, tq, D_MODEL), lambda g, c, col=col: (g, 0, col))
        ya_shape = (r // SUBLANES, SUBLANES, D_MODEL)
    assert nchunk > 1 or not zero_state
    full = lambda shape: pl.BlockSpec(shape, lambda g, c: (0,) * len(shape))
    c_spec = pl.BlockSpec((nsub, H_A, DH_A, DH_A), lambda g, c: (g, 0, 0, 0))
    n_spec = pl.BlockSpec((nsub, H_A, DH_A), lambda g, c: (g, 0, 0))
    m_spec = pl.BlockSpec((nsub, SUBLANES, GATE_LANES), lambda g, c: (g, 0, 0))
    in_specs = [row(0), row(1), pl.BlockSpec((nsub, SUBLANES, D_MODEL), lambda g, c: (g, 0, 0))]
    args = [proj3, proj3, st8]
    if not zero_state:
        in_specs += [c_spec, n_spec, m_spec]
        args += list(states)
    nblk = D_MODEL // BD_BLOCK
    in_specs += [full((CONV_A, D_MODEL)), full((1, D_MODEL)),
                 full((nblk, BD_BLOCK, BD_BLOCK)), full((nblk, BD_BLOCK, BD_BLOCK)),
                 full((nblk, BD_BLOCK, BD_BLOCK)),
                 full((3 * D_MODEL, GATE_LANES)), full((1, GATE_LANES)),
                 full((1, D_MODEL)), full((1, D_MODEL))]
    args += [p["conv_a_w"], p["conv_a_b"], p["wq_bd"], p["wk_bd"], p["wv_bd"], p["w_if"], p["b_if"],
             p["norm_h_a"], p["skip_a"]]
    kern = functools.partial(_branch_a_kernel, nsub=nsub, tq=tq, valid=valid, nchunk=nchunk,
                             zero_state=zero_state)
    ya, c_new, n_new, m_new = pl.pallas_call(
        kern,
        grid=(nseq // nsub, nchunk),
        in_specs=in_specs,
        out_specs=[row(0), c_spec, n_spec, m_spec],
        out_shape=[jax.ShapeDtypeStruct(ya_shape, out_dtype),
                   jax.ShapeDtypeStruct((nseq, H_A, DH_A, DH_A), F32),
                   jax.ShapeDtypeStruct((nseq, H_A, DH_A), F32),
                   jax.ShapeDtypeStruct((nseq, SUBLANES, GATE_LANES), F32)],
        scratch_shapes=[pltpu.VMEM((nsub, SUBLANES + tq, D_MODEL), F32)],
        compiler_params=_cparams("parallel", "arbitrary"),
        name="branch_a",
    )(*args)
    return ya.reshape(r, D_MODEL), c_new, n_new, m_new[:, 0, :H_A]


def _gmlp_kernel(u_ref, v_ref, nv_ref, wmix_ref, bmix_ref, yb_ref, vn_ref):
    vn = _layer_norm_rows(jax.nn.gelu(v_ref[...])) * nv_ref[...]
    vn_ref[...] = vn
    vnb = vn.astype(BF16)
    bias = bmix_ref[...]
    lane = lax.broadcasted_iota(jnp.int32, (1, GATE_LANES), 1)
    mixed = []
    for g in range(G_B):
        b_col = jnp.sum(jnp.where(lane == g, bias, 0.0), axis=1, keepdims=True)
        mixed.append(jnp.dot(wmix_ref[g], vnb[:, g * DG_B:(g + 1) * DG_B],
                             preferred_element_type=F32) + b_col)
    yb_ref[...] = (jax.nn.gelu(u_ref[...]) * jnp.concatenate(mixed, axis=1)).astype(yb_ref.dtype)


def gmlp(proj, norm_v, w_s, b_s, seq):
    r = proj.shape[0]
    lc = min(seq, CHUNK)
    per = CHUNK // lc
    tril = jnp.tril(jnp.ones((lc, lc), bool))
    w_small = jnp.where(tril[None], w_s[:, :lc, :lc], 0.0)
    wmix = jnp.einsum("ab,gts->gatbs", jnp.eye(per, dtype=F32), w_small)
    wmix = wmix.reshape(G_B, CHUNK, CHUNK).astype(BF16)
    bmix = jnp.pad(jnp.tile(b_s[:, :lc].T, (per, 1)), ((0, 0), (0, GATE_LANES - G_B)))
    u_col = 2 * D_MODEL // D_B
    return pl.pallas_call(
        _gmlp_kernel,
        grid=(r // CHUNK,),
        in_specs=[pl.BlockSpec((CHUNK, D_B), lambda i: (i, u_col)),
                  pl.BlockSpec((CHUNK, D_B), lambda i: (i, u_col + 1)),
                  pl.BlockSpec((1, D_B), lambda i: (0, 0)),
                  pl.BlockSpec((G_B, CHUNK, CHUNK), lambda i: (0, 0, 0)),
                  pl.BlockSpec((CHUNK, GATE_LANES), lambda i: (0, 0))],
        out_specs=[pl.BlockSpec((CHUNK, D_B), lambda i: (i, 0)),
                   pl.BlockSpec((CHUNK, D_B), lambda i: (i, 0))],
        out_shape=[jax.ShapeDtypeStruct((r, D_B), BF16), jax.ShapeDtypeStruct((r, D_B), F32)],
        compiler_params=_cparams("parallel"),
        name="gmlp",
    )(proj, proj, norm_v.reshape(1, D_B), wmix, bmix)


def _attend(q, k, v):
    outs = []
    for h in range(H_C):
        hs = slice(h * DH_C, (h + 1) * DH_C)
        s = lax.dot_general(q[:, hs], k[:, hs], (((1,), (1,)), ((), ())),
                            preferred_element_type=F32) * (DH_C ** -0.5)
        e = jnp.exp(s - jnp.max(s, axis=-1, keepdims=True))
        prob = e / jnp.sum(e, axis=-1, keepdims=True)
        outs.append(jnp.dot(prob.astype(BF16), v[:, hs], preferred_element_type=F32))
    return jnp.concatenate(outs, axis=1)


def _attend_stacked(q, k, v):
    col = lax.broadcasted_iota(jnp.int32, (SUBLANES, D_C), 1)
    head = [(col >= h * DH_C) & (col < (h + 1) * DH_C) for h in range(H_C)]
    q4 = jnp.concatenate([jnp.where(head[h], q, 0.0) for h in range(H_C)], axis=0).astype(BF16)
    s = lax.dot_general(q4, k, (((1,), (1,)), ((), ())), preferred_element_type=F32) * (DH_C ** -0.5)
    e = jnp.exp(s - jnp.max(s, axis=-1, keepdims=True))
    prob = e / jnp.sum(e, axis=-1, keepdims=True)
    o4 = jnp.dot(prob.astype(BF16), v, preferred_element_type=F32)
    out = o4[0:SUBLANES]
    for h in range(1, H_C):
        out = jnp.where(head[h], o4[h * SUBLANES:(h + 1) * SUBLANES], out)
    return out


def _xattn_kernel(q_ref, k_ref, v_ref, o_ref, *, nseq, rq, valid):
    per = rq // valid
    for t in range(nseq // per):
        tile = q_ref[t * rq:(t + 1) * rq, :]
        parts = []
        for j in range(per):
            i = t * per + j
            k = k_ref[i * MEM_LEN:(i + 1) * MEM_LEN, :].astype(BF16)
            v = v_ref[i * MEM_LEN:(i + 1) * MEM_LEN, :].astype(BF16)
            if per == 1:
                parts.append(_attend(tile.astype(BF16), k, v))
            else:
                parts.append(_attend_stacked(_unpack(tile, j, valid), k, v))
        o = parts[0] if per == 1 else _pack(parts, valid)
        o_ref[t * rq:(t + 1) * rq, :] = o.astype(o_ref.dtype)


def xattn(proj, k_arr, v_arr, k_col, v_col, *, nbatch, nseq, rq, out_dtype):
    r = proj.shape[0]
    seq = r // nbatch
    valid = min(seq, rq)
    rows = nseq * valid
    nq = seq // valid
    assert nseq == 1 or nq == 1
    q_col = (2 * D_MODEL + 2 * D_B) // D_C
    kern = functools.partial(_xattn_kernel, nseq=nseq, rq=rq, valid=valid)
    return pl.pallas_call(
        kern,
        grid=(nbatch // nseq, nq),
        in_specs=[pl.BlockSpec((rows, D_C), lambda g, t: (g * nq + t, q_col)),
                  pl.BlockSpec((nseq * MEM_LEN, D_C), lambda g, t: (g, k_col)),
                  pl.BlockSpec((nseq * MEM_LEN, D_C), lambda g, t: (g, v_col))],
        out_specs=pl.BlockSpec((rows, D_C), lambda g, t: (g * nq + t, 0)),
        out_shape=jax.ShapeDtypeStruct((r, D_C), out_dtype),
        compiler_params=_cparams("parallel", "arbitrary"),
        name="xattn",
    )(proj, k_arr, v_arr)


def _mix_out_kernel(ya_ref, yb_ref, yc_ref, ga0, ga1, gb0, gb1, gc0, gc1, x_ref,
                    wa_ref, wb_ref, wc_ref, wo_ref, g1_ref, g2_ref, x1_ref, h2_ref):
    def branch(y_ref, w_ref, lo, hi):
        gate = jnp.concatenate([lo[...], hi[...]], axis=1)
        return jax.nn.sigmoid(gate) * jnp.dot(y_ref[...].astype(BF16), w_ref[...],
                                              preferred_element_type=F32)

    merged = (branch(ya_ref, wa_ref, ga0, ga1) + branch(yb_ref, wb_ref, gb0, gb1)
              + branch(yc_ref, wc_ref, gc0, gc1))
    o = jnp.dot(merged.astype(BF16), wo_ref[...], preferred_element_type=F32)
    x1 = x_ref[...] + _rms(o) * g1_ref[...]
    x1_ref[...] = x1
    h2_ref[...] = (_rms(x1) * g2_ref[...]).astype(h2_ref.dtype)


def mix_out(ya, yb, yc, proj, x, w_a, w_b, w_c, w_o, g_post, g_pre, bm=256):
    r = x.shape[0]
    half = D_MODEL // 2
    g0 = GATE_COL0 // half
    gate = lambda k: pl.BlockSpec((bm, half), lambda i, k=k: (i, g0 + k))
    rowspec = lambda d: pl.BlockSpec((bm, d), lambda i: (i, 0))
    vec = pl.BlockSpec((1, D_MODEL), lambda i: (0, 0))
    return pl.pallas_call(
        _mix_out_kernel,
        grid=(r // bm,),
        in_specs=[rowspec(D_MODEL), rowspec(D_B), rowspec(D_C)] + [gate(k) for k in range(6)]
                 + [rowspec(D_MODEL), _resident((D_MODEL, D_MODEL)), _resident((D_B, D_MODEL)),
                    _resident((D_C, D_MODEL)), _resident((D_MODEL, D_MODEL)), vec, vec],
        out_specs=[rowspec(D_MODEL), rowspec(D_MODEL)],
        out_shape=[jax.ShapeDtypeStruct((r, D_MODEL), F32), jax.ShapeDtypeStruct((r, D_MODEL), BF16)],
        compiler_params=_cparams("parallel"),
        name="mix_out",
    )(ya, yb, yc, proj, proj, proj, proj, proj, proj, x, w_a, w_b, w_c, w_o,
      g_post.reshape(1, D_MODEL), g_pre.reshape(1, D_MODEL))


def _gated_act(gate, val):
    return jax.nn.gelu(gate) * val


def _ffn_up_act_kernel(a_ref, wg_ref, wv_ref, cwg_ref, cwv_ref, cbg_ref, cbv_ref,
                       o_ref, sg_ref, sv_ref, ext_ref, *, bm):
    @pl.when(pl.program_id(2) == 0)
    def _():
        ext_ref[:, 0:SUBLANES, :] = jnp.zeros((2, SUBLANES, ext_ref.shape[2]), F32)

    a = a_ref[...]

    def conv(which, w_ref, cw_ref, cb_ref, st_ref):
        u = jnp.dot(a, w_ref[...].astype(BF16), preferred_element_type=F32)
        ext_ref[which, SUBLANES:SUBLANES + bm, :] = u
        cw = cw_ref[...]
        y = cb_ref[...] + u * cw[CONV_F - 1:CONV_F, :]
        for j in range(CONV_F - 1):
            y = y + ext_ref[which, pl.ds(SUBLANES - (CONV_F - 1) + j, bm), :] * cw[j:j + 1, :]
        tail = u[bm - SUBLANES:bm, :]
        ext_ref[which, 0:SUBLANES, :] = tail
        st_ref[0] = tail
        return y

    gate = conv(0, wg_ref, cwg_ref, cbg_ref, sg_ref)
    val = conv(1, wv_ref, cwv_ref, cbv_ref, sv_ref)
    o_ref[...] = _gated_act(gate, val).astype(o_ref.dtype)


def ffn_up_act(h2, w_up, conv_w, conv_b, *, nseq, bm=1024, bn=512):
    r = h2.shape[0]
    tiles = r // nseq // bm
    ncol = D_FF // bn
    wspec = lambda off: pl.BlockSpec((D_MODEL, bn), lambda j, b, t, off=off: (0, off + j))
    cwspec = lambda off: pl.BlockSpec((CONV_F, bn), lambda j, b, t, off=off: (0, off + j))
    cbspec = lambda off: pl.BlockSpec((1, bn), lambda j, b, t, off=off: (0, off + j))
    st_spec = pl.BlockSpec((1, SUBLANES, bn), lambda j, b, t: (b, 0, j))
    kern = functools.partial(_ffn_up_act_kernel, bm=bm)
    return pl.pallas_call(
        kern,
        grid=(ncol, nseq, tiles),
        in_specs=[pl.BlockSpec((bm, D_MODEL), lambda j, b, t: (b * tiles + t, 0)),
                  wspec(0), wspec(ncol), cwspec(0), cwspec(ncol), cbspec(0), cbspec(ncol)],
        out_specs=[pl.BlockSpec((bm, bn), lambda j, b, t: (b * tiles + t, j)), st_spec, st_spec],
        out_shape=[jax.ShapeDtypeStruct((r, D_FF), BF16),
                   jax.ShapeDtypeStruct((nseq, SUBLANES, D_FF), F32),
                   jax.ShapeDtypeStruct((nseq, SUBLANES, D_FF), F32)],
        scratch_shapes=[pltpu.VMEM((2, SUBLANES + bm, bn), F32)],
        compiler_params=_cparams("parallel", "arbitrary", "arbitrary"),
        name="ffn_up_act",
    )(h2, w_up, w_up, conv_w, conv_w, conv_b, conv_b)


def _ffn_act_kernel(g_ref, v_ref, sg_ref, sv_ref, wg_ref, wv_ref, bg_ref, bv_ref, o_ref, ext_ref,
                    *, nseq, valid):
    per = SUBLANES // valid

    def conv(which, i, x, st_ref, w_ref, b_ref):
        ext_ref[which, 0:SUBLANES, :] = st_ref[i]
        ext_ref[which, SUBLANES:2 * SUBLANES, :] = x
        w = w_ref[...]
        y = b_ref[...] + x * w[CONV_F - 1:CONV_F, :]
        for j in range(CONV_F - 1):
            y = y + ext_ref[which, pl.ds(SUBLANES - (CONV_F - 1) + j, SUBLANES), :] * w[j:j + 1, :]
        return y

    for t in range(nseq // per):
        rows = slice(t * SUBLANES, (t + 1) * SUBLANES)
        g_tile, v_tile = g_ref[rows, :], v_ref[rows, :]
        parts = []
        for j in range(per):
            i = t * per + j
            gate = conv(0, i, _unpack(g_tile, j, valid), sg_ref, wg_ref, bg_ref)
            val = conv(1, i, _unpack(v_tile, j, valid), sv_ref, wv_ref, bv_ref)
            parts.append(_gated_act(gate, val))
        o_ref[rows, :] = _pack(parts, valid).astype(o_ref.dtype)


def ffn_act(up, st8, conv_w, conv_b, *, nbatch, nseq, bc=2048):
    r = up.shape[0]
    valid = r // nbatch
    ncol = D_FF // bc
    x_spec = lambda off: pl.BlockSpec((nseq * valid, bc), lambda g, j, off=off: (g, off + j))
    s_spec = lambda off: pl.BlockSpec((nseq, SUBLANES, bc), lambda g, j, off=off: (g, 0, off + j))
    w_spec = lambda off: pl.BlockSpec((CONV_F, bc), lambda g, j, off=off: (0, off + j))
    b_spec = lambda off: pl.BlockSpec((1, bc), lambda g, j, off=off: (0, off + j))
    kern = functools.partial(_ffn_act_kernel, nseq=nseq, valid=valid)
    return pl.pallas_call(
        kern,
        grid=(nbatch // nseq, ncol),
        in_specs=[x_spec(0), x_spec(ncol), s_spec(0), s_spec(ncol), w_spec(0), w_spec(ncol),
                  b_spec(0), b_spec(ncol)],
        out_specs=pl.BlockSpec((nseq * valid, bc), lambda g, j: (g, j)),
        out_shape=jax.ShapeDtypeStruct((r, D_FF), F32),
        scratch_shapes=[pltpu.VMEM((2, 2 * SUBLANES, bc), F32)],
        compiler_params=_cparams("parallel", "parallel"),
        name="ffn_act",
    )(up, up, st8, st8, conv_w, conv_w, conv_b, conv_b)


def _ffn_down_kernel(a_ref, w_ref, x_ref, g_ref, y_ref):
    f = jnp.dot(a_ref[...].astype(BF16), w_ref[...], preferred_element_type=F32)
    y_ref[...] = x_ref[...] + _rms(f) * g_ref[...]


def ffn_down_post(act, w_down, x1, g, bm=256):
    r = x1.shape[0]
    return pl.pallas_call(
        _ffn_down_kernel,
        grid=(r // bm,),
        in_specs=[pl.BlockSpec((bm, D_FF), lambda i: (i, 0)), _resident((D_FF, D_MODEL)),
                  pl.BlockSpec((bm, D_MODEL), lambda i: (i, 0)),
                  pl.BlockSpec((1, D_MODEL), lambda i: (0, 0))],
        out_specs=pl.BlockSpec((bm, D_MODEL), lambda i: (i, 0)),
        out_shape=jax.ShapeDtypeStruct((r, D_MODEL), F32),
        compiler_params=_cparams("parallel"),
        name="ffn_down_post",
    )(act, w_down, x1, g.reshape(1, D_MODEL))


def _blockdiag_tiles(w):
    per = BD_BLOCK // QKV_BLOCK
    w = w.reshape(D_MODEL // BD_BLOCK, per, QKV_BLOCK, QKV_BLOCK)
    eye = jnp.eye(per, dtype=w.dtype)
    t = jnp.einsum("jnik,nm->jnimk", w, eye)
    return t.reshape(D_MODEL // BD_BLOCK, BD_BLOCK, BD_BLOCK).astype(BF16)


def _prep_params(conv_a_w, conv_a_b, w_q_a, w_k_a, w_v_a, w_if_a, b_if_a, norm_h_a, skip_a):
    pad = GATE_LANES - 2 * H_A
    return {
        "conv_a_w": conv_a_w,
        "conv_a_b": conv_a_b.reshape(1, D_MODEL),
        "wq_bd": _blockdiag_tiles(w_q_a),
        "wk_bd": _blockdiag_tiles(w_k_a),
        "wv_bd": _blockdiag_tiles(w_v_a),
        "w_if": jnp.pad(w_if_a, ((0, 0), (0, pad))).astype(BF16),
        "b_if": jnp.pad(b_if_a, (0, pad)).reshape(1, GATE_LANES),
        "norm_h_a": norm_h_a.reshape(1, D_MODEL),
        "skip_a": skip_a.reshape(1, D_MODEL),
    }


def _tail_rows(state, k):
    return jnp.pad(state, ((0, 0), (SUBLANES - k, 0), (0, 0)))


def _trunk(x, mem, conv_prev, mlstm_state, ffn_prev, w, pa, *, nseq, seq):
    prompt = seq >= CHUNK
    k_arr, v_arr, k_col, v_col = mem

    h = rms_cast(x, w["norm_mix_pre"])
    proj = matmul(h, w["w_in"], 1024, 1024)

    st8 = _tail_rows(conv_prev, CONV_A - 1)
    yb, vn = gmlp(proj, w["norm_v_b"], w["w_s_b"], w["b_s_b"], seq)
    ya, c_new, n_new, m_new = branch_a(proj, st8, mlstm_state, pa, nseq=nseq, seq=seq,
                                       out_dtype=BF16 if prompt else F32)
    if prompt:
        yc = xattn(proj, k_arr, v_arr, k_col, v_col, nbatch=nseq, nseq=1, rq=256, out_dtype=BF16)
    else:
        yc = xattn(proj, k_arr, v_arr, k_col, v_col, nbatch=nseq, nseq=4, rq=SUBLANES, out_dtype=F32)

    x1, h2 = mix_out(ya, yb, yc, proj, x, w["w_br_a"], w["w_br_b"], w["w_br_c"], w["w_out"],
                     w["norm_mix_post"], w["norm_ffn_pre"])

    conv_b = w["conv_ffn_b"].reshape(1, 2 * D_FF)
    if prompt:
        act, tail_g, tail_v = ffn_up_act(h2, w["w_ffn_up"], w["conv_ffn_w"], conv_b, nseq=nseq)
        ffn_new = jnp.concatenate([tail_g, tail_v], axis=-1)[:, SUBLANES - (CONV_F - 1):]
    else:
        up = matmul(h2, w["w_ffn_up"], 1024, 1024)
        act = ffn_act(up, _tail_rows(ffn_prev, CONV_F - 1), w["conv_ffn_w"], conv_b,
                      nbatch=nseq, nseq=32)
        ffn_new = up.reshape(nseq, seq, 2 * D_FF)[:, seq - (CONV_F - 1):]
    y = ffn_down_post(act, w["w_ffn_down"], x1, w["norm_ffn_post"])
    conv_new = proj.reshape(nseq, seq, N_IN)[:, seq - (CONV_A - 1):, :D_MODEL]
    return y, conv_new, ffn_new, vn, c_new, n_new, m_new


def kernel(x_prompt, x_sample, state_conv_a, state_mlstm_c, state_mlstm_n, state_mlstm_m, cache_mem_k, cache_mem_v, state_ffn_conv, mem_prompt, norm_mix_pre, w_in, conv_a_w, conv_a_b, w_q_a, w_k_a, w_v_a, w_if_a, b_if_a, norm_h_a, skip_a, w_br_a, norm_v_b, w_s_b, b_s_b, w_br_b, norm_mem, w_mem_kv, w_br_c, w_out, norm_mix_post, norm_ffn_pre, w_ffn_up, conv_ffn_w, conv_ffn_b, w_ffn_down, norm_ffn_post):
    nb, seq, _ = x_prompt.shape
    db, dseq, _ = x_sample.shape
    w = {
        "norm_mix_pre": norm_mix_pre[0], "w_in": w_in[0], "norm_v_b": norm_v_b[0],
        "w_s_b": w_s_b[0], "b_s_b": b_s_b[0],
        "w_br_a": w_br_a[0].astype(BF16), "w_br_b": w_br_b[0].astype(BF16),
        "w_br_c": w_br_c[0].astype(BF16), "w_out": w_out[0].astype(BF16),
        "norm_mix_post": norm_mix_post[0], "norm_ffn_pre": norm_ffn_pre[0], "w_ffn_up": w_ffn_up[0],
        "conv_ffn_w": conv_ffn_w[0], "conv_ffn_b": conv_ffn_b[0],
        "w_ffn_down": w_ffn_down[0].astype(BF16), "norm_ffn_post": norm_ffn_post[0],
    }
    pa = _prep_params(conv_a_w[0], conv_a_b[0], w_q_a[0], w_k_a[0], w_v_a[0], w_if_a[0], b_if_a[0],
                      norm_h_a[0], skip_a[0])

    mem_h = rms_cast(mem_prompt.reshape(nb * MEM_LEN, D_MODEL), norm_mem[0])
    kv = matmul(mem_h, w_mem_kv[0], 1024, 1024)
    yp, conv_p, ffn_p, _, c_p, n_p, m_p = _trunk(
        x_prompt.reshape(nb * seq, D_MODEL), (kv, kv, 0, 1),
        jnp.zeros((nb, CONV_A - 1, D_MODEL), F32), None, None, w, pa, nseq=nb, seq=seq)

    m0 = jnp.pad(state_mlstm_m[0][:, None, :], ((0, 0), (0, SUBLANES - 1), (0, GATE_LANES - H_A)))
    ys, conv_s, ffn_s, vn_s, c_s, n_s, m_s = _trunk(
        x_sample.reshape(db * dseq, D_MODEL),
        (cache_mem_k[0].reshape(db * MEM_LEN, D_C), cache_mem_v[0].reshape(db * MEM_LEN, D_C), 0, 0),
        state_conv_a[0], (state_mlstm_c[0], state_mlstm_n[0], m0), state_ffn_conv[0],
        w, pa, nseq=db, seq=dseq)

    kv = kv.reshape(nb, MEM_LEN, 2 * D_C)
    return (
        yp.reshape(nb, seq, D_MODEL), ys.reshape(db, dseq, D_MODEL),
        conv_p[None], c_p[None], n_p[None], m_p[None],
        kv[:, :, :D_C].reshape(1, nb, MEM_LEN, H_C, DH_C), kv[:, :, D_C:].reshape(1, nb, MEM_LEN, H_C, DH_C),
        ffn_p[None],
        conv_s[None], c_s[None], n_s[None], m_s[None],
        vn_s.reshape(1, db, dseq, D_B), ffn_s[None],
    )
```

```python
import functools

import jax
import jax.numpy as jnp
from jax import lax
from jax.experimental import pallas as pl
from jax.experimental.pallas import tpu as pltpu

F32 = jnp.float32
BF16 = jnp.bfloat16

D_MODEL = 2048
H_A = 4
DH_A = 512
QKV_BLOCK = 4
CONV_A = 4
CONV_F = 3
D_B = 1024
G_B = 4
DG_B = 256
H_C = 4
DH_C = 256
D_C = 1024
MEM_LEN = 256
D_FF = 6144
N_IN = 13312
GATE_COL0 = 2 * D_MODEL + 2 * D_B + D_C
EPS = 1e-6
NEG_INF = -1e30

CHUNK = 128
BD_BLOCK = 256
GATE_LANES = 128
SUBLANES = 8
V7X_VMEM_BYTES = 64 * 1024 * 1024
VMEM_LIMIT = V7X_VMEM_BYTES - 8 * 1024 * 1024


def _cparams(*sem):
    return pltpu.CompilerParams(dimension_semantics=sem, vmem_limit_bytes=VMEM_LIMIT)


def _resident(shape):
    return pl.BlockSpec(shape, lambda *_: (0,) * len(shape), pipeline_mode=pl.Buffered(1))


def _rms(x):
    return x * lax.rsqrt(jnp.mean(x * x, axis=-1, keepdims=True) + EPS)


def _unpack(tile, i, valid):
    return tile if i == 0 else pltpu.roll(tile, SUBLANES - i * valid, axis=0)


def _packed_conv(x, hist, w, b):
    nh = hist.shape[0]
    row = lax.broadcasted_iota(jnp.int32, (SUBLANES, 1), 0)
    xs = x
    for k in range(nh):
        xs = jnp.where(row == SUBLANES - nh + k, hist[k:k + 1, :], xs)
    y = b + x * w[nh:nh + 1, :]
    for j in range(nh):
        y = y + pltpu.roll(xs, nh - j, axis=0) * w[j:j + 1, :]
    return y


def _last_rows(x, valid, n):
    return pltpu.roll(x, SUBLANES - (valid - n), axis=0)[0:n]


def _pack(parts, valid):
    row = lax.broadcasted_iota(jnp.int32, (SUBLANES, 1), 0)
    out = parts[0]
    for i in range(1, len(parts)):
        out = jnp.where(row < i * valid, out, pltpu.roll(parts[i], i * valid, axis=0))
    return out


def _rms_kernel(x_ref, g_ref, o_ref):
    o_ref[...] = (_rms(x_ref[...]) * g_ref[...]).astype(o_ref.dtype)


def rms_cast(x, g, tile=512):
    r, d = x.shape
    return pl.pallas_call(
        _rms_kernel,
        grid=(r // tile,),
        in_specs=[pl.BlockSpec((tile, d), lambda i: (i, 0)),
                  pl.BlockSpec((1, d), lambda i: (0, 0))],
        out_specs=pl.BlockSpec((tile, d), lambda i: (i, 0)),
        out_shape=jax.ShapeDtypeStruct((r, d), BF16),
        compiler_params=_cparams("parallel"),
        name="rms_cast",
    )(x, g.reshape(1, d))


def _mm_kernel(a_ref, w_ref, o_ref):
    o_ref[...] = jnp.dot(a_ref[...].astype(BF16), w_ref[...].astype(BF16),
                         preferred_element_type=F32)


def matmul(a, w, bm, bn, col0=0, n=None):
    r, k = a.shape
    n = w.shape[1] if n is None else n
    bm = min(bm, r)
    j0 = col0 // bn
    return pl.pallas_call(
        _mm_kernel,
        grid=(n // bn, r // bm),
        in_specs=[pl.BlockSpec((bm, k), lambda j, i: (i, 0)),
                  pl.BlockSpec((k, bn), lambda j, i: (0, j0 + j))],
        out_specs=pl.BlockSpec((bm, bn), lambda j, i: (i, j)),
        out_shape=jax.ShapeDtypeStruct((r, n), F32),
        compiler_params=_cparams("parallel", "arbitrary"),
        name="matmul",
    )(a, w)


def _log_sigmoid(x):
    return jnp.minimum(x, 0.0) - jnp.log1p(jnp.exp(-jnp.abs(x)))


def _layer_norm_rows(x):
    xc = x - jnp.mean(x, axis=-1, keepdims=True)
    return xc * lax.rsqrt(jnp.mean(xc * xc, axis=-1, keepdims=True) + EPS)


def _branch_a_kernel(*refs, nsub, tq, valid, nchunk, zero_state):
    n_in = 3 if zero_state else 6
    xm_ref, z_ref, st_ref = refs[:3]
    (cw_ref, cb_ref, wq_ref, wk_ref, wv_ref, wif_ref, bif_ref, nh_ref, sk_ref,
     ya_ref, c_ref, n_ref, m_ref, last_ref) = refs[n_in:]
    lc = CHUNK
    packed = valid < tq
    carried = nchunk > 1
    chunk = pl.program_id(1)
    cs_ref, ext_ref = (last_ref, None) if packed else (None, last_ref)

    if carried:
        @pl.when(chunk == 0)
        def _init():
            for i in range(nsub):
                ext_ref[i, 0:SUBLANES, :] = st_ref[i]
            if zero_state:
                c_ref[...] = jnp.zeros(c_ref.shape, F32)
                n_ref[...] = jnp.zeros(n_ref.shape, F32)
                m_ref[...] = jnp.zeros(m_ref.shape, F32)
            else:
                c_ref[...] = refs[3][...]
                n_ref[...] = refs[4][...]
                m_ref[...] = refs[5][...]
        cin_ref, nin_ref, min_ref = c_ref, n_ref, m_ref
    else:
        cin_ref, nin_ref, min_ref = refs[3:6]

    lane = lax.broadcasted_iota(jnp.int32, (1, GATE_LANES), 1)
    r = lax.broadcasted_iota(jnp.int32, (tq, lc), 0)
    s_idx = lax.broadcasted_iota(jnp.int32, (tq, lc), 1)
    tril = s_idx <= r
    eye = s_idx == r
    row_valid = lax.broadcasted_iota(jnp.int32, (tq, 1), 0) < valid
    lane_valid = lax.broadcasted_iota(jnp.int32, (1, lc), 1) < valid
    scale = DH_A ** -0.5
    cw = cw_ref[...]

    def blockdiag(xb, w_ref):
        nblk = D_MODEL // BD_BLOCK
        return jnp.concatenate(
            [jnp.dot(xb[:, j * BD_BLOCK:(j + 1) * BD_BLOCK], w_ref[j], preferred_element_type=F32)
             for j in range(nblk)], axis=1)

    def pad_rows(x):
        if tq == lc:
            return x
        return jnp.concatenate([x, jnp.zeros((lc - tq, x.shape[1]), x.dtype)], axis=0)

    packed_out = []
    for i in range(nsub):
        if packed:
            x_in = _unpack(xm_ref[0], i, valid)
            z_in = _unpack(z_ref[0], i, valid)
        else:
            x_in = xm_ref[i]
            z_in = z_ref[i]

        if packed:
            y = _packed_conv(x_in, st_ref[i], cw, cb_ref[...])
            cs_ref[i] = _last_rows(x_in, valid, CONV_A - 1)
        else:
            if not carried:
                ext_ref[i, 0:SUBLANES, :] = st_ref[i]
            ext_ref[i, SUBLANES:SUBLANES + tq, :] = x_in
            y = cb_ref[...] + x_in * cw[CONV_A - 1:CONV_A, :]
            for j in range(CONV_A - 1):
                y = y + ext_ref[i, pl.ds(SUBLANES - (CONV_A - 1) + j, tq), :] * cw[j:j + 1, :]
            if carried:
                ext_ref[i, 0:SUBLANES, :] = x_in[tq - SUBLANES:tq, :]
        xconv = y * jax.nn.sigmoid(y)

        xcb = xconv.astype(BF16)
        q32 = blockdiag(xcb, wq_ref)
        k32 = blockdiag(xcb, wk_ref)
        v32 = blockdiag(x_in.astype(BF16), wv_ref)
        qb, kb, vb = q32.astype(BF16), k32.astype(BF16), v32.astype(BF16)
        if_pre = (jnp.dot(qb, wif_ref[0:D_MODEL, :], preferred_element_type=F32)
                  + jnp.dot(kb, wif_ref[D_MODEL:2 * D_MODEL, :], preferred_element_type=F32)
                  + jnp.dot(vb, wif_ref[2 * D_MODEL:3 * D_MODEL, :], preferred_element_type=F32)
                  + bif_ref[...])
        log_f_all = _log_sigmoid(if_pre)
        kpad = pad_rows(k32).astype(BF16)
        vpad = pad_rows(v32).astype(BF16)

        m_all = min_ref[i, 0:1, :]
        m_out = m_all
        heads = []
        for h in range(H_A):
            hs = slice(h * DH_A, (h + 1) * DH_A)
            qh, kh, vh = qb[:, hs], kpad[:, hs], vpad[:, hs]
            li_col = jnp.sum(jnp.where(lane == h, if_pre, 0.0), axis=1, keepdims=True)
            lf_col = jnp.sum(jnp.where(lane == H_A + h, log_f_all, 0.0), axis=1, keepdims=True)
            if packed:
                li_col = jnp.where(row_valid, li_col, NEG_INF)
                lf_col = jnp.where(row_valid, lf_col, 0.0)
            m_prev = jnp.sum(jnp.where(lane == h, m_all, 0.0), axis=1, keepdims=True)

            b_row = jnp.sum(jnp.where(r <= s_idx, lf_col, 0.0), axis=0, keepdims=True)
            li_row = jnp.sum(jnp.where(eye, li_col, 0.0), axis=0, keepdims=True)
            if packed:
                li_row = jnp.where(lane_valid, li_row, NEG_INF)
            b_col = jnp.sum(jnp.where(eye, b_row, 0.0), axis=1, keepdims=True)

            d = jnp.where(tril, b_col - b_row + li_row, NEG_INF)
            inter = m_prev + b_col
            m_t = jnp.maximum(jnp.max(d, axis=1, keepdims=True), inter)
            w = jnp.exp(d - m_t)
            g = jnp.exp(inter - m_t)
            qk = lax.dot_general(qh, kh, (((1,), (1,)), ((), ())), preferred_element_type=F32)
            s = qk * (w * scale)
            c_h = cin_ref[i, h]
            n_h = nin_ref[i, h:h + 1, :]
            num = (g * jnp.dot(qh, c_h.astype(BF16), preferred_element_type=F32)
                   + jnp.dot(s.astype(BF16), vh, preferred_element_type=F32))
            den = (g * jnp.sum(q32[:, hs] * n_h, axis=1, keepdims=True)
                   + jnp.sum(s, axis=1, keepdims=True))
            ht = num / jnp.maximum(jnp.abs(den), jnp.exp(-m_t))

            b_last = jnp.sum(lf_col, axis=0, keepdims=True)
            a_col = b_last - b_col + li_col
            m_new = jnp.maximum(m_prev + b_last, jnp.max(a_col, axis=0, keepdims=True))
            decay = jnp.exp(m_prev + b_last - m_new)
            kw = k32[:, hs] * (jnp.exp(a_col - m_new) * scale)
            kwt = jnp.transpose(pad_rows(kw)).astype(BF16)
            c_ref[i, h] = decay * c_h + jnp.dot(kwt, vh, preferred_element_type=F32)
            n_ref[i, h:h + 1, :] = decay * n_h + jnp.sum(kw, axis=0, keepdims=True)
            m_out = jnp.where(lane == h, m_new, m_out)

            hn = _layer_norm_rows(ht) * nh_ref[:, hs]
            heads.append(jax.nn.sigmoid(z_in[:, hs]) * (hn + sk_ref[:, hs] * xconv[:, hs]))
        m_ref[i, 0:1, :] = m_out
        if packed:
            packed_out.append(heads)
        else:
            for h in range(H_A):
                ya_ref[i, :, h * DH_A:(h + 1) * DH_A] = heads[h].astype(ya_ref.dtype)
    if packed:
        for h in range(H_A):
            tile = _pack([heads[h] for heads in packed_out], valid)
            ya_ref[0, :, h * DH_A:(h + 1) * DH_A] = tile.astype(ya_ref.dtype)


def branch_a(proj, states, p, *, nseq, seq, out_dtype):
    r = proj.shape[0]
    zero_state = states is None
    if seq >= CHUNK:
        nsub, tq, valid, nchunk = 2, CHUNK, CHUNK, seq // CHUNK
        proj3 = proj.reshape(nseq, seq, N_IN)
        row = lambda col: pl.BlockSpec((nsub, tq, D_MODEL), lambda g, c, col=col: (g, c, col))
        ya_shape = (nseq, seq, D_MODEL)
        assert zero_state
        hist = jnp.zeros((nseq, SUBLANES, D_MODEL), F32)
    else:
        nsub, tq, valid, nchunk = SUBLANES // seq, SUBLANES, seq, 1
        proj3 = proj.reshape(r // SUBLANES, SUBLANES, N_IN)
        row = lambda col: pl.BlockSpec((1, tq, D_MODEL), lambda g, c, col=col: (g, 0, col))
        ya_shape = (r // SUBLANES, SUBLANES, D_MODEL)
        assert not zero_state
        hist = states[0]
    packed = valid < tq
    full = lambda shape: pl.BlockSpec(shape, lambda g, c: (0,) * len(shape))
    c_spec = pl.BlockSpec((nsub, H_A, DH_A, DH_A), lambda g, c: (g, 0, 0, 0))
    n_spec = pl.BlockSpec((nsub, H_A, DH_A), lambda g, c: (g, 0, 0))
    m_spec = pl.BlockSpec((nsub, SUBLANES, GATE_LANES), lambda g, c: (g, 0, 0))
    hist_spec = pl.BlockSpec((nsub,) + hist.shape[1:], lambda g, c: (g, 0, 0))
    in_specs = [row(0), row(1), hist_spec]
    args = [proj3, proj3, hist]
    if not zero_state:
        in_specs += [c_spec, n_spec, m_spec]
        args += list(states[1:])
    nblk = D_MODEL // BD_BLOCK
    in_specs += [full((CONV_A, D_MODEL)), full((1, D_MODEL)),
                 full((nblk, BD_BLOCK, BD_BLOCK)), full((nblk, BD_BLOCK, BD_BLOCK)),
                 full((nblk, BD_BLOCK, BD_BLOCK)),
                 full((3 * D_MODEL, GATE_LANES)), full((1, GATE_LANES)),
                 full((1, D_MODEL)), full((1, D_MODEL))]
    args += [p["conv_a_w"], p["conv_a_b"], p["wq_bd"], p["wk_bd"], p["wv_bd"], p["w_if"], p["b_if"],
             p["norm_h_a"], p["skip_a"]]
    kern = functools.partial(_branch_a_kernel, nsub=nsub, tq=tq, valid=valid, nchunk=nchunk,
                             zero_state=zero_state)
    out_specs = [row(0), c_spec, n_spec, m_spec]
    out_shape = [jax.ShapeDtypeStruct(ya_shape, out_dtype),
                 jax.ShapeDtypeStruct((nseq, H_A, DH_A, DH_A), F32),
                 jax.ShapeDtypeStruct((nseq, H_A, DH_A), F32),
                 jax.ShapeDtypeStruct((nseq, SUBLANES, GATE_LANES), F32)]
    scratch = []
    if packed:
        out_specs.append(hist_spec)
        out_shape.append(jax.ShapeDtypeStruct(hist.shape, F32))
    else:
        scratch.append(pltpu.VMEM((nsub, SUBLANES + tq, D_MODEL), F32))
    outs = pl.pallas_call(
        kern,
        grid=(nseq // nsub, nchunk),
        in_specs=in_specs,
        out_specs=out_specs,
        out_shape=out_shape,
        scratch_shapes=scratch,
        compiler_params=_cparams("parallel", "arbitrary"),
        name="branch_a",
    )(*args)
    ya, c_new, n_new, m_new = outs[:4]
    hist_new = outs[4] if packed else None
    return ya.reshape(r, D_MODEL), c_new, n_new, m_new[:, 0, :H_A], hist_new


def _gmlp_kernel(u_ref, v_ref, nv_ref, wmix_ref, bmix_ref, yb_ref, vn_ref):
    vn = _layer_norm_rows(jax.nn.gelu(v_ref[...])) * nv_ref[...]
    vn_ref[...] = vn
    vnb = vn.astype(BF16)
    bias = bmix_ref[...]
    lane = lax.broadcasted_iota(jnp.int32, (1, GATE_LANES), 1)
    mixed = []
    for g in range(G_B):
        b_col = jnp.sum(jnp.where(lane == g, bias, 0.0), axis=1, keepdims=True)
        mixed.append(jnp.dot(wmix_ref[g], vnb[:, g * DG_B:(g + 1) * DG_B],
                             preferred_element_type=F32) + b_col)
    yb_ref[...] = (jax.nn.gelu(u_ref[...]) * jnp.concatenate(mixed, axis=1)).astype(yb_ref.dtype)


def gmlp(proj, norm_v, w_s, b_s, seq):
    r = proj.shape[0]
    lc = min(seq, CHUNK)
    per = CHUNK // lc
    tril = jnp.tril(jnp.ones((lc, lc), bool))
    w_small = jnp.where(tril[None], w_s[:, :lc, :lc], 0.0)
    wmix = jnp.einsum("ab,gts->gatbs", jnp.eye(per, dtype=F32), w_small)
    wmix = wmix.reshape(G_B, CHUNK, CHUNK).astype(BF16)
    bmix = jnp.pad(jnp.tile(b_s[:, :lc].T, (per, 1)), ((0, 0), (0, GATE_LANES - G_B)))
    u_col = 2 * D_MODEL // D_B
    return pl.pallas_call(
        _gmlp_kernel,
        grid=(r // CHUNK,),
        in_specs=[pl.BlockSpec((CHUNK, D_B), lambda i: (i, u_col)),
                  pl.BlockSpec((CHUNK, D_B), lambda i: (i, u_col + 1)),
                  pl.BlockSpec((1, D_B), lambda i: (0, 0)),
                  pl.BlockSpec((G_B, CHUNK, CHUNK), lambda i: (0, 0, 0)),
                  pl.BlockSpec((CHUNK, GATE_LANES), lambda i: (0, 0))],
        out_specs=[pl.BlockSpec((CHUNK, D_B), lambda i: (i, 0)),
                   pl.BlockSpec((CHUNK, D_B), lambda i: (i, 0))],
        out_shape=[jax.ShapeDtypeStruct((r, D_B), BF16), jax.ShapeDtypeStruct((r, D_B), F32)],
        compiler_params=_cparams("parallel"),
        name="gmlp",
    )(proj, proj, norm_v.reshape(1, D_B), wmix, bmix)


def _attend(q, k, v):
    outs = []
    for h in range(H_C):
        hs = slice(h * DH_C, (h + 1) * DH_C)
        s = lax.dot_general(q[:, hs], k[:, hs], (((1,), (1,)), ((), ())),
                            preferred_element_type=F32) * (DH_C ** -0.5)
        e = jnp.exp(s - jnp.max(s, axis=-1, keepdims=True))
        prob = e / jnp.sum(e, axis=-1, keepdims=True)
        outs.append(jnp.dot(prob.astype(BF16), v[:, hs], preferred_element_type=F32))
    return jnp.concatenate(outs, axis=1)


def _attend_stacked(q, k, v):
    col = lax.broadcasted_iota(jnp.int32, (SUBLANES, D_C), 1)
    head = [(col >= h * DH_C) & (col < (h + 1) * DH_C) for h in range(H_C)]
    q4 = jnp.concatenate([jnp.where(head[h], q, 0.0) for h in range(H_C)], axis=0).astype(BF16)
    s = lax.dot_general(q4, k, (((1,), (1,)), ((), ())), preferred_element_type=F32) * (DH_C ** -0.5)
    e = jnp.exp(s - jnp.max(s, axis=-1, keepdims=True))
    prob = e / jnp.sum(e, axis=-1, keepdims=True)
    o4 = jnp.dot(prob.astype(BF16), v, preferred_element_type=F32)
    out = o4[0:SUBLANES]
    for h in range(1, H_C):
        out = jnp.where(head[h], o4[h * SUBLANES:(h + 1) * SUBLANES], out)
    return out


def _mem_rows(ref, i):
    if len(ref.shape) == 4:
        return jnp.concatenate([ref[i, :, h, :] for h in range(H_C)], axis=1).astype(BF16)
    return ref[i * MEM_LEN:(i + 1) * MEM_LEN, :].astype(BF16)


def _xattn_kernel(q_ref, k_ref, v_ref, o_ref, *, nseq, rq, valid):
    per = rq // valid
    for t in range(nseq // per):
        tile = q_ref[t * rq:(t + 1) * rq, :]
        parts = []
        for j in range(per):
            i = t * per + j
            k = _mem_rows(k_ref, i)
            v = _mem_rows(v_ref, i)
            if per == 1:
                parts.append(_attend(tile.astype(BF16), k, v))
            else:
                parts.append(_attend_stacked(_unpack(tile, j, valid), k, v))
        o = parts[0] if per == 1 else _pack(parts, valid)
        o_ref[t * rq:(t + 1) * rq, :] = o.astype(o_ref.dtype)


def xattn(proj, k_arr, v_arr, k_col, v_col, *, nbatch, nseq, rq, out_dtype):
    r = proj.shape[0]
    seq = r // nbatch
    valid = min(seq, rq)
    rows = nseq * valid
    nq = seq // valid
    assert nseq == 1 or nq == 1
    q_col = (2 * D_MODEL + 2 * D_B) // D_C

    def mem_spec(arr, col):
        if arr.ndim == 4:
            return pl.BlockSpec((nseq, MEM_LEN, H_C, DH_C), lambda g, t: (g, 0, 0, 0))
        return pl.BlockSpec((nseq * MEM_LEN, D_C), lambda g, t: (g, col))

    kern = functools.partial(_xattn_kernel, nseq=nseq, rq=rq, valid=valid)
    return pl.pallas_call(
        kern,
        grid=(nbatch // nseq, nq),
        in_specs=[pl.BlockSpec((rows, D_C), lambda g, t: (g * nq + t, q_col)),
                  mem_spec(k_arr, k_col), mem_spec(v_arr, v_col)],
        out_specs=pl.BlockSpec((rows, D_C), lambda g, t: (g * nq + t, 0)),
        out_shape=jax.ShapeDtypeStruct((r, D_C), out_dtype),
        compiler_params=_cparams("parallel", "arbitrary"),
        name="xattn",
    )(proj, k_arr, v_arr)


def _mix_out_kernel(ya_ref, yb_ref, yc_ref, ga0, ga1, gb0, gb1, gc0, gc1, x_ref,
                    wa_ref, wb_ref, wc_ref, wo_ref, g1_ref, g2_ref, x1_ref, h2_ref):
    def branch(y_ref, w_ref, lo, hi):
        gate = jnp.concatenate([lo[...], hi[...]], axis=1)
        return jax.nn.sigmoid(gate) * jnp.dot(y_ref[...].astype(BF16), w_ref[...],
                                              preferred_element_type=F32)

    merged = (branch(ya_ref, wa_ref, ga0, ga1) + branch(yb_ref, wb_ref, gb0, gb1)
              + branch(yc_ref, wc_ref, gc0, gc1))
    o = jnp.dot(merged.astype(BF16), wo_ref[...], preferred_element_type=F32)
    x1 = x_ref[...] + _rms(o) * g1_ref[...]
    x1_ref[...] = x1
    h2_ref[...] = (_rms(x1) * g2_ref[...]).astype(h2_ref.dtype)


def mix_out(ya, yb, yc, proj, x, w_a, w_b, w_c, w_o, g_post, g_pre, bm=256):
    r = x.shape[0]
    half = D_MODEL // 2
    g0 = GATE_COL0 // half
    gate = lambda k: pl.BlockSpec((bm, half), lambda i, k=k: (i, g0 + k))
    rowspec = lambda d: pl.BlockSpec((bm, d), lambda i: (i, 0))
    vec = pl.BlockSpec((1, D_MODEL), lambda i: (0, 0))
    return pl.pallas_call(
        _mix_out_kernel,
        grid=(r // bm,),
        in_specs=[rowspec(D_MODEL), rowspec(D_B), rowspec(D_C)] + [gate(k) for k in range(6)]
                 + [rowspec(D_MODEL), _resident((D_MODEL, D_MODEL)), _resident((D_B, D_MODEL)),
                    _resident((D_C, D_MODEL)), _resident((D_MODEL, D_MODEL)), vec, vec],
        out_specs=[rowspec(D_MODEL), rowspec(D_MODEL)],
        out_shape=[jax.ShapeDtypeStruct((r, D_MODEL), F32), jax.ShapeDtypeStruct((r, D_MODEL), BF16)],
        compiler_params=_cparams("parallel"),
        name="mix_out",
    )(ya, yb, yc, proj, proj, proj, proj, proj, proj, x, w_a, w_b, w_c, w_o,
      g_post.reshape(1, D_MODEL), g_pre.reshape(1, D_MODEL))


def _gated_act(gate, val):
    return jax.nn.gelu(gate) * val


def _ffn_up_act_kernel(a_ref, wg_ref, wv_ref, cwg_ref, cwv_ref, cbg_ref, cbv_ref,
                       o_ref, sg_ref, sv_ref, ext_ref, *, bm):
    @pl.when(pl.program_id(2) == 0)
    def _():
        ext_ref[:, 0:SUBLANES, :] = jnp.zeros((2, SUBLANES, ext_ref.shape[2]), F32)

    a = a_ref[...]

    def conv(which, w_ref, cw_ref, cb_ref, st_ref):
        u = jnp.dot(a, w_ref[...].astype(BF16), preferred_element_type=F32)
        ext_ref[which, SUBLANES:SUBLANES + bm, :] = u
        cw = cw_ref[...]
        y = cb_ref[...] + u * cw[CONV_F - 1:CONV_F, :]
        for j in range(CONV_F - 1):
            y = y + ext_ref[which, pl.ds(SUBLANES - (CONV_F - 1) + j, bm), :] * cw[j:j + 1, :]
        tail = u[bm - SUBLANES:bm, :]
        ext_ref[which, 0:SUBLANES, :] = tail
        st_ref[0] = tail
        return y

    gate = conv(0, wg_ref, cwg_ref, cbg_ref, sg_ref)
    val = conv(1, wv_ref, cwv_ref, cbv_ref, sv_ref)
    o_ref[...] = _gated_act(gate, val).astype(o_ref.dtype)


def ffn_up_act(h2, w_up, conv_w, conv_b, *, nseq, bm=1024, bn=512):
    r = h2.shape[0]
    tiles = r // nseq // bm
    ncol = D_FF // bn
    wspec = lambda off: pl.BlockSpec((D_MODEL, bn), lambda j, b, t, off=off: (0, off + j))
    cwspec = lambda off: pl.BlockSpec((CONV_F, bn), lambda j, b, t, off=off: (0, off + j))
    cbspec = lambda off: pl.BlockSpec((1, bn), lambda j, b, t, off=off: (0, off + j))
    st_spec = pl.BlockSpec((1, SUBLANES, bn), lambda j, b, t: (b, 0, j))
    kern = functools.partial(_ffn_up_act_kernel, bm=bm)
    return pl.pallas_call(
        kern,
        grid=(ncol, nseq, tiles),
        in_specs=[pl.BlockSpec((bm, D_MODEL), lambda j, b, t: (b * tiles + t, 0)),
                  wspec(0), wspec(ncol), cwspec(0), cwspec(ncol), cbspec(0), cbspec(ncol)],
        out_specs=[pl.BlockSpec((bm, bn), lambda j, b, t: (b * tiles + t, j)), st_spec, st_spec],
        out_shape=[jax.ShapeDtypeStruct((r, D_FF), BF16),
                   jax.ShapeDtypeStruct((nseq, SUBLANES, D_FF), F32),
                   jax.ShapeDtypeStruct((nseq, SUBLANES, D_FF), F32)],
        scratch_shapes=[pltpu.VMEM((2, SUBLANES + bm, bn), F32)],
        compiler_params=_cparams("parallel", "arbitrary", "arbitrary"),
        name="ffn_up_act",
    )(h2, w_up, w_up, conv_w, conv_w, conv_b, conv_b)


def _ffn_act_kernel(g_ref, v_ref, sg_ref, sv_ref, wg_ref, wv_ref, bg_ref, bv_ref,
                    o_ref, ng_ref, nv_ref, *, nseq, valid):
    per = SUBLANES // valid
    wg, wv, bg, bv = wg_ref[...], wv_ref[...], bg_ref[...], bv_ref[...]
    for t in range(nseq // per):
        rows = slice(t * SUBLANES, (t + 1) * SUBLANES)
        g_tile, v_tile = g_ref[rows, :], v_ref[rows, :]
        parts = []
        for j in range(per):
            i = t * per + j
            xg, xv = _unpack(g_tile, j, valid), _unpack(v_tile, j, valid)
            parts.append(_gated_act(_packed_conv(xg, sg_ref[i], wg, bg),
                                    _packed_conv(xv, sv_ref[i], wv, bv)))
            ng_ref[i] = _last_rows(xg, valid, CONV_F - 1)
            nv_ref[i] = _last_rows(xv, valid, CONV_F - 1)
        o_ref[rows, :] = _pack(parts, valid).astype(o_ref.dtype)


def ffn_act(up, hist, conv_w, conv_b, *, nbatch, nseq, bc=2048):
    r = up.shape[0]
    valid = r // nbatch
    ncol = D_FF // bc
    nh = CONV_F - 1
    x_spec = lambda off: pl.BlockSpec((nseq * valid, bc), lambda g, j, off=off: (g, off + j))
    s_spec = lambda off: pl.BlockSpec((nseq, nh, bc), lambda g, j, off=off: (g, 0, off + j))
    w_spec = lambda off: pl.BlockSpec((CONV_F, bc), lambda g, j, off=off: (0, off + j))
    b_spec = lambda off: pl.BlockSpec((1, bc), lambda g, j, off=off: (0, off + j))
    kern = functools.partial(_ffn_act_kernel, nseq=nseq, valid=valid)
    return pl.pallas_call(
        kern,
        grid=(nbatch // nseq, ncol),
        in_specs=[x_spec(0), x_spec(ncol), s_spec(0), s_spec(ncol), w_spec(0), w_spec(ncol),
                  b_spec(0), b_spec(ncol)],
        out_specs=[pl.BlockSpec((nseq * valid, bc), lambda g, j: (g, j)), s_spec(0), s_spec(0)],
        out_shape=[jax.ShapeDtypeStruct((r, D_FF), F32),
                   jax.ShapeDtypeStruct((nbatch, nh, D_FF), F32),
                   jax.ShapeDtypeStruct((nbatch, nh, D_FF), F32)],
        compiler_params=_cparams("parallel", "parallel"),
        name="ffn_act",
    )(up, up, hist, hist, conv_w, conv_w, conv_b, conv_b)


def _ffn_down_kernel(a_ref, w_ref, x_ref, g_ref, y_ref):
    f = jnp.dot(a_ref[...].astype(BF16), w_ref[...], preferred_element_type=F32)
    y_ref[...] = x_ref[...] + _rms(f) * g_ref[...]


def ffn_down_post(act, w_down, x1, g, bm=256):
    r = x1.shape[0]
    return pl.pallas_call(
        _ffn_down_kernel,
        grid=(r // bm,),
        in_specs=[pl.BlockSpec((bm, D_FF), lambda i: (i, 0)), _resident((D_FF, D_MODEL)),
                  pl.BlockSpec((bm, D_MODEL), lambda i: (i, 0)),
                  pl.BlockSpec((1, D_MODEL), lambda i: (0, 0))],
        out_specs=pl.BlockSpec((bm, D_MODEL), lambda i: (i, 0)),
        out_shape=jax.ShapeDtypeStruct((r, D_MODEL), F32),
        compiler_params=_cparams("parallel"),
        name="ffn_down_post",
    )(act, w_down, x1, g.reshape(1, D_MODEL))


def _blockdiag_tiles(w):
    nblk = D_MODEL // BD_BLOCK
    rows = w.reshape(nblk, BD_BLOCK, QKV_BLOCK)
    tiled = jnp.tile(rows, (1, 1, BD_BLOCK // QKV_BLOCK))
    idx = jnp.arange(BD_BLOCK) // QKV_BLOCK
    same_block = idx[:, None] == idx[None, :]
    return jnp.where(same_block[None], tiled, 0.0).astype(BF16)


def _prep_params(conv_a_w, conv_a_b, w_q_a, w_k_a, w_v_a, w_if_a, b_if_a, norm_h_a, skip_a):
    pad = GATE_LANES - 2 * H_A
    return {
        "conv_a_w": conv_a_w,
        "conv_a_b": conv_a_b.reshape(1, D_MODEL),
        "wq_bd": _blockdiag_tiles(w_q_a),
        "wk_bd": _blockdiag_tiles(w_k_a),
        "wv_bd": _blockdiag_tiles(w_v_a),
        "w_if": jnp.pad(w_if_a, ((0, 0), (0, pad))).astype(BF16),
        "b_if": jnp.pad(b_if_a, (0, pad)).reshape(1, GATE_LANES),
        "norm_h_a": norm_h_a.reshape(1, D_MODEL),
        "skip_a": skip_a.reshape(1, D_MODEL),
    }


def _trunk(x, mem, states, w, pa, *, nseq, seq):
    prompt = states is None
    k_arr, v_arr, k_col, v_col = mem

    h = rms_cast(x, w["norm_mix_pre"])
    proj = matmul(h, w["w_in"], 1024, 1024)

    yb, vn = gmlp(proj, w["norm_v_b"], w["w_s_b"], w["b_s_b"], seq)
    ya, c_new, n_new, m_new, conv_new = branch_a(proj, None if prompt else states[:4], pa,
                                                 nseq=nseq, seq=seq,
                                                 out_dtype=BF16 if prompt else F32)
    if prompt:
        yc = xattn(proj, k_arr, v_arr, k_col, v_col, nbatch=nseq, nseq=1, rq=256, out_dtype=BF16)
        conv_new = proj.reshape(nseq, seq, N_IN)[:, seq - (CONV_A - 1):, :D_MODEL]
    else:
        yc = xattn(proj, k_arr, v_arr, k_col, v_col, nbatch=nseq, nseq=4, rq=SUBLANES, out_dtype=F32)

    x1, h2 = mix_out(ya, yb, yc, proj, x, w["w_br_a"], w["w_br_b"], w["w_br_c"], w["w_out"],
                     w["norm_mix_post"], w["norm_ffn_pre"])

    conv_b = w["conv_ffn_b"].reshape(1, 2 * D_FF)
    if prompt:
        act, tail_g, tail_v = ffn_up_act(h2, w["w_ffn_up"], w["conv_ffn_w"], conv_b, nseq=nseq)
        ffn_new = jnp.concatenate([tail_g, tail_v], axis=-1)[:, SUBLANES - (CONV_F - 1):]
    else:
        up = matmul(h2, w["w_ffn_up"], 1024, 1024)
        act, new_g, new_v = ffn_act(up, states[4], w["conv_ffn_w"], conv_b, nbatch=nseq, nseq=32)
        ffn_new = jnp.concatenate([new_g, new_v], axis=-1)
    y = ffn_down_post(act, w["w_ffn_down"], x1, w["norm_ffn_post"])
    return y, conv_new, ffn_new, vn, c_new, n_new, m_new


def kernel(x_prompt, x_sample, state_conv_a, state_mlstm_c, state_mlstm_n, state_mlstm_m, cache_mem_k, cache_mem_v, state_ffn_conv, mem_prompt, norm_mix_pre, w_in, conv_a_w, conv_a_b, w_q_a, w_k_a, w_v_a, w_if_a, b_if_a, norm_h_a, skip_a, w_br_a, norm_v_b, w_s_b, b_s_b, w_br_b, norm_mem, w_mem_kv, w_br_c, w_out, norm_mix_post, norm_ffn_pre, w_ffn_up, conv_ffn_w, conv_ffn_b, w_ffn_down, norm_ffn_post):
    nb, seq, _ = x_prompt.shape
    db, dseq, _ = x_sample.shape
    w = {
        "norm_mix_pre": norm_mix_pre[0], "w_in": w_in[0], "norm_v_b": norm_v_b[0],
        "w_s_b": w_s_b[0], "b_s_b": b_s_b[0],
        "w_br_a": w_br_a[0].astype(BF16), "w_br_b": w_br_b[0].astype(BF16),
        "w_br_c": w_br_c[0].astype(BF16), "w_out": w_out[0].astype(BF16),
        "norm_mix_post": norm_mix_post[0], "norm_ffn_pre": norm_ffn_pre[0], "w_ffn_up": w_ffn_up[0],
        "conv_ffn_w": conv_ffn_w[0], "conv_ffn_b": conv_ffn_b[0],
        "w_ffn_down": w_ffn_down[0].astype(BF16), "norm_ffn_post": norm_ffn_post[0],
    }
    pa = _prep_params(conv_a_w[0], conv_a_b[0], w_q_a[0], w_k_a[0], w_v_a[0], w_if_a[0], b_if_a[0],
                      norm_h_a[0], skip_a[0])

    mem_h = rms_cast(mem_prompt.reshape(nb * MEM_LEN, D_MODEL), norm_mem[0])
    mem_k = matmul(mem_h, w_mem_kv[0], 1024, 1024, 0, D_C)
    mem_v = matmul(mem_h, w_mem_kv[0], 1024, 1024, D_C, D_C)
    yp, conv_p, ffn_p, _, c_p, n_p, m_p = _trunk(
        x_prompt.reshape(nb * seq, D_MODEL), (mem_k, mem_v, 0, 0), None, w, pa, nseq=nb, seq=seq)

    m0 = jnp.pad(state_mlstm_m[0][:, None, :], ((0, 0), (0, SUBLANES - 1), (0, GATE_LANES - H_A)))
    ys, conv_s, ffn_s, vn_s, c_s, n_s, m_s = _trunk(
        x_sample.reshape(db * dseq, D_MODEL), (cache_mem_k[0], cache_mem_v[0], 0, 0),
        (state_conv_a[0], state_mlstm_c[0], state_mlstm_n[0], m0, state_ffn_conv[0]),
        w, pa, nseq=db, seq=dseq)

    return (
        yp.reshape(nb, seq, D_MODEL), ys.reshape(db, dseq, D_MODEL),
        conv_p[None], c_p[None], n_p[None], m_p[None],
        mem_k.reshape(1, nb, MEM_LEN, H_C, DH_C), mem_v.reshape(1, nb, MEM_LEN, H_C, DH_C),
        ffn_p[None],
        conv_s[None], c_s[None], n_s[None], m_s[None],
        vn_s.reshape(1, db, dseq, D_B), ffn_s[None],
    )
```

```python
import functools

import jax
import jax.numpy as jnp
from jax import lax
from jax.experimental import pallas as pl
from jax.experimental.pallas import tpu as pltpu

F32 = jnp.float32
BF16 = jnp.bfloat16

D_MODEL = 2048
H_A = 4
DH_A = 512
QKV_BLOCK = 4
CONV_A = 4
CONV_F = 3
D_B = 1024
G_B = 4
DG_B = 256
H_C = 4
DH_C = 256
D_C = 1024
MEM_LEN = 256
D_FF = 6144
N_IN = 13312
GATE_COL0 = 2 * D_MODEL + 2 * D_B + D_C
EPS = 1e-6
NEG_INF = -1e30

CHUNK = 128
BD_BLOCK = 256
GATE_LANES = 128
SUBLANES = 8
V7X_VMEM_BYTES = 64 * 1024 * 1024
VMEM_LIMIT = V7X_VMEM_BYTES - 8 * 1024 * 1024


def _cparams(*sem):
    return pltpu.CompilerParams(dimension_semantics=sem, vmem_limit_bytes=VMEM_LIMIT)


def _resident(shape):
    return pl.BlockSpec(shape, lambda *_: (0,) * len(shape), pipeline_mode=pl.Buffered(1))


def _rms(x):
    return x * lax.rsqrt(jnp.mean(x * x, axis=-1, keepdims=True) + EPS)


def _unpack(tile, i, valid):
    return tile if i == 0 else pltpu.roll(tile, SUBLANES - i * valid, axis=0)


def _packed_conv(x, hist, w, b):
    nh = hist.shape[0]
    row = lax.broadcasted_iota(jnp.int32, (SUBLANES, 1), 0)
    xs = x
    for k in range(nh):
        xs = jnp.where(row == SUBLANES - nh + k, hist[k:k + 1, :], xs)
    y = b + x * w[nh:nh + 1, :]
    for j in range(nh):
        y = y + pltpu.roll(xs, nh - j, axis=0) * w[j:j + 1, :]
    return y


def _last_rows(x, valid, n):
    return pltpu.roll(x, SUBLANES - (valid - n), axis=0)[0:n]


def _pack(parts, valid):
    row = lax.broadcasted_iota(jnp.int32, (SUBLANES, 1), 0)
    out = parts[0]
    for i in range(1, len(parts)):
        out = jnp.where(row < i * valid, out, pltpu.roll(parts[i], i * valid, axis=0))
    return out


def _rms_kernel(x_ref, g_ref, o_ref):
    o_ref[...] = (_rms(x_ref[...]) * g_ref[...]).astype(o_ref.dtype)


def rms_cast(x, g, tile=512):
    r, d = x.shape
    return pl.pallas_call(
        _rms_kernel,
        grid=(r // tile,),
        in_specs=[pl.BlockSpec((tile, d), lambda i: (i, 0)),
                  pl.BlockSpec((1, d), lambda i: (0, 0))],
        out_specs=pl.BlockSpec((tile, d), lambda i: (i, 0)),
        out_shape=jax.ShapeDtypeStruct((r, d), BF16),
        compiler_params=_cparams("parallel"),
        name="rms_cast",
    )(x, g.reshape(1, d))


def _mm_kernel(a_ref, w_ref, o_ref):
    o_ref[...] = jnp.dot(a_ref[...].astype(BF16), w_ref[...].astype(BF16),
                         preferred_element_type=F32)


def matmul(a, w, bm, bn, col0=0, n=None):
    r, k = a.shape
    n = w.shape[1] if n is None else n
    bm = min(bm, r)
    j0 = col0 // bn
    return pl.pallas_call(
        _mm_kernel,
        grid=(n // bn, r // bm),
        in_specs=[pl.BlockSpec((bm, k), lambda j, i: (i, 0)),
                  pl.BlockSpec((k, bn), lambda j, i: (0, j0 + j))],
        out_specs=pl.BlockSpec((bm, bn), lambda j, i: (i, j)),
        out_shape=jax.ShapeDtypeStruct((r, n), F32),
        compiler_params=_cparams("parallel", "arbitrary"),
        name="matmul",
    )(a, w)


def _log_sigmoid(x):
    return jnp.minimum(x, 0.0) - jnp.log1p(jnp.exp(-jnp.abs(x)))


def _layer_norm_rows(x):
    xc = x - jnp.mean(x, axis=-1, keepdims=True)
    return xc * lax.rsqrt(jnp.mean(xc * xc, axis=-1, keepdims=True) + EPS)


def _branch_a_kernel(*refs, nsub, tq, valid, nchunk, zero_state):
    n_in = 3 if zero_state else 6
    xm_ref, z_ref, st_ref = refs[:3]
    (cw_ref, cb_ref, wq_ref, wk_ref, wv_ref, wif_ref, bif_ref, nh_ref, sk_ref,
     ya_ref, c_ref, n_ref, m_ref) = refs[n_in:n_in + 13]
    lc = CHUNK
    packed = valid < tq
    carried = nchunk > 1
    chunk = pl.program_id(1)
    cs_ref, ext_ref = (refs[n_in + 13], None) if packed else (None, refs[n_in + 13])

    if carried:
        @pl.when(chunk == 0)
        def _init():
            ext_ref[...] = st_ref[...]
            if zero_state:
                c_ref[...] = jnp.zeros(c_ref.shape, F32)
                n_ref[...] = jnp.zeros(n_ref.shape, F32)
                m_ref[...] = jnp.zeros(m_ref.shape, F32)
            else:
                c_ref[...] = refs[3][...]
                n_ref[...] = refs[4][...]
                m_ref[...] = refs[5][...]
        cin_ref, nin_ref, min_ref = c_ref, n_ref, m_ref
    else:
        cin_ref, nin_ref, min_ref = refs[3:6]

    lane = lax.broadcasted_iota(jnp.int32, (1, GATE_LANES), 1)
    r = lax.broadcasted_iota(jnp.int32, (tq, lc), 0)
    s_idx = lax.broadcasted_iota(jnp.int32, (tq, lc), 1)
    tril = s_idx <= r
    eye = s_idx == r
    row_valid = lax.broadcasted_iota(jnp.int32, (tq, 1), 0) < valid
    lane_valid = lax.broadcasted_iota(jnp.int32, (1, lc), 1) < valid
    scale = DH_A ** -0.5
    cw = cw_ref[...]

    def blockdiag(xb, w_ref):
        nblk = D_MODEL // BD_BLOCK
        return jnp.concatenate(
            [jnp.dot(xb[:, j * BD_BLOCK:(j + 1) * BD_BLOCK], w_ref[j], preferred_element_type=F32)
             for j in range(nblk)], axis=1)

    def pad_rows(x):
        if tq == lc:
            return x
        return jnp.concatenate([x, jnp.zeros((lc - tq, x.shape[1]), x.dtype)], axis=0)

    packed_out = []
    for i in range(nsub):
        if packed:
            x_in = _unpack(xm_ref[0], i, valid)
            z_in = _unpack(z_ref[0], i, valid)
        else:
            x_in = xm_ref[i]

        if packed:
            y = _packed_conv(x_in, st_ref[i], cw, cb_ref[...])
            cs_ref[i] = _last_rows(x_in, valid, CONV_A - 1)
        else:
            hist = ext_ref[i] if carried else st_ref[i]
            xe = jnp.concatenate([hist, x_in], axis=0)
            y = cb_ref[...] + x_in * cw[CONV_A - 1:CONV_A, :]
            for j in range(CONV_A - 1):
                shifted = pltpu.roll(xe, CONV_A - 1 - j, axis=0)[SUBLANES:, :]
                y = y + shifted * cw[j:j + 1, :]
            if carried:
                ext_ref[i] = x_in[tq - SUBLANES:tq, :]
        xconv = y * jax.nn.sigmoid(y)

        xcb = xconv.astype(BF16)
        q32 = blockdiag(xcb, wq_ref)
        k32 = blockdiag(xcb, wk_ref)
        v32 = blockdiag(x_in.astype(BF16), wv_ref)
        qb, kb, vb = q32.astype(BF16), k32.astype(BF16), v32.astype(BF16)
        if_pre = (jnp.dot(qb, wif_ref[0:D_MODEL, :], preferred_element_type=F32)
                  + jnp.dot(kb, wif_ref[D_MODEL:2 * D_MODEL, :], preferred_element_type=F32)
                  + jnp.dot(vb, wif_ref[2 * D_MODEL:3 * D_MODEL, :], preferred_element_type=F32)
                  + bif_ref[...])
        log_f_all = _log_sigmoid(if_pre)
        kpad = pad_rows(k32).astype(BF16)
        vpad = pad_rows(v32).astype(BF16)

        m_all = min_ref[i, 0:1, :]
        m_out = m_all
        heads = []
        for h in range(H_A):
            hs = slice(h * DH_A, (h + 1) * DH_A)
            qh, kh, vh = qb[:, hs], kpad[:, hs], vpad[:, hs]
            li_col = jnp.sum(jnp.where(lane == h, if_pre, 0.0), axis=1, keepdims=True)
            lf_col = jnp.sum(jnp.where(lane == H_A + h, log_f_all, 0.0), axis=1, keepdims=True)
            if packed:
                li_col = jnp.where(row_valid, li_col, NEG_INF)
                lf_col = jnp.where(row_valid, lf_col, 0.0)
            m_prev = jnp.sum(jnp.where(lane == h, m_all, 0.0), axis=1, keepdims=True)

            b_row = jnp.sum(jnp.where(r <= s_idx, lf_col, 0.0), axis=0, keepdims=True)
            li_row = jnp.sum(jnp.where(eye, li_col, 0.0), axis=0, keepdims=True)
            if packed:
                li_row = jnp.where(lane_valid, li_row, NEG_INF)
            b_col = jnp.sum(jnp.where(eye, b_row, 0.0), axis=1, keepdims=True)

            d = jnp.where(tril, b_col - b_row + li_row, NEG_INF)
            inter = m_prev + b_col
            m_t = jnp.maximum(jnp.max(d, axis=1, keepdims=True), inter)
            w = jnp.exp(d - m_t)
            g = jnp.exp(inter - m_t)
            qk = lax.dot_general(qh, kh, (((1,), (1,)), ((), ())), preferred_element_type=F32)
            s = qk * (w * scale)
            c_h = cin_ref[i, h]
            n_h = nin_ref[i, h:h + 1, :]
            num = (g * jnp.dot(qh, c_h.astype(BF16), preferred_element_type=F32)
                   + jnp.dot(s.astype(BF16), vh, preferred_element_type=F32))
            den = (g * jnp.sum(q32[:, hs] * n_h, axis=1, keepdims=True)
                   + jnp.sum(s, axis=1, keepdims=True))
            ht = num / jnp.maximum(jnp.abs(den), jnp.exp(-m_t))

            b_last = jnp.sum(lf_col, axis=0, keepdims=True)
            a_col = b_last - b_col + li_col
            m_new = jnp.maximum(m_prev + b_last, jnp.max(a_col, axis=0, keepdims=True))
            decay = jnp.exp(m_prev + b_last - m_new)
            kw = k32[:, hs] * (jnp.exp(a_col - m_new) * scale)
            kwt = jnp.transpose(pad_rows(kw)).astype(BF16)
            c_ref[i, h] = decay * c_h + jnp.dot(kwt, vh, preferred_element_type=F32)
            n_ref[i, h:h + 1, :] = decay * n_h + jnp.sum(kw, axis=0, keepdims=True)
            m_out = jnp.where(lane == h, m_new, m_out)

            hn = _layer_norm_rows(ht) * nh_ref[:, hs]
            z_h = z_in[:, hs] if packed else z_ref[i, :, hs]
            ya_h = jax.nn.sigmoid(z_h) * (hn + sk_ref[:, hs] * xconv[:, hs])
            if packed:
                heads.append(ya_h)
            else:
                ya_ref[i, :, hs] = ya_h.astype(ya_ref.dtype)
        m_ref[i, 0:1, :] = m_out
        if packed:
            packed_out.append(heads)
    if packed:
        for h in range(H_A):
            tile = _pack([heads[h] for heads in packed_out], valid)
            ya_ref[0, :, h * DH_A:(h + 1) * DH_A] = tile.astype(ya_ref.dtype)


def branch_a(proj, states, p, *, nseq, seq, out_dtype):
    r = proj.shape[0]
    zero_state = states is None
    if seq >= CHUNK:
        nsub, tq, valid, nchunk = 2, CHUNK, CHUNK, seq // CHUNK
        proj3 = proj.reshape(nseq, seq, N_IN)
        row = lambda col: pl.BlockSpec((nsub, tq, D_MODEL), lambda g, c, col=col: (g, c, col))
        ya_shape = (nseq, seq, D_MODEL)
        assert zero_state
        hist = jnp.zeros((nseq, SUBLANES, D_MODEL), F32)
    else:
        nsub, tq, valid, nchunk = SUBLANES // seq, SUBLANES, seq, 1
        proj3 = proj.reshape(r // SUBLANES, SUBLANES, N_IN)
        row = lambda col: pl.BlockSpec((1, tq, D_MODEL), lambda g, c, col=col: (g, 0, col))
        ya_shape = (r // SUBLANES, SUBLANES, D_MODEL)
        assert not zero_state
        hist = states[0]
    packed = valid < tq
    full = lambda shape: pl.BlockSpec(shape, lambda g, c: (0,) * len(shape))
    c_spec = pl.BlockSpec((nsub, H_A, DH_A, DH_A), lambda g, c: (g, 0, 0, 0))
    n_spec = pl.BlockSpec((nsub, H_A, DH_A), lambda g, c: (g, 0, 0))
    m_spec = pl.BlockSpec((nsub, SUBLANES, GATE_LANES), lambda g, c: (g, 0, 0))
    hist_spec = pl.BlockSpec((nsub,) + hist.shape[1:], lambda g, c: (g, 0, 0))
    in_specs = [row(0), row(1), hist_spec]
    args = [proj3, proj3, hist]
    if not zero_state:
        in_specs += [c_spec, n_spec, m_spec]
        args += list(states[1:])
    nblk = D_MODEL // BD_BLOCK
    in_specs += [full((CONV_A, D_MODEL)), full((1, D_MODEL)),
                 full((nblk, BD_BLOCK, BD_BLOCK)), full((nblk, BD_BLOCK, BD_BLOCK)),
                 full((nblk, BD_BLOCK, BD_BLOCK)),
                 full((3 * D_MODEL, GATE_LANES)), full((1, GATE_LANES)),
                 full((1, D_MODEL)), full((1, D_MODEL))]
    args += [p["conv_a_w"], p["conv_a_b"], p["wq_bd"], p["wk_bd"], p["wv_bd"], p["w_if"], p["b_if"],
             p["norm_h_a"], p["skip_a"]]
    kern = functools.partial(_branch_a_kernel, nsub=nsub, tq=tq, valid=valid, nchunk=nchunk,
                             zero_state=zero_state)
    out_specs = [row(0), c_spec, n_spec, m_spec]
    out_shape = [jax.ShapeDtypeStruct(ya_shape, out_dtype),
                 jax.ShapeDtypeStruct((nseq, H_A, DH_A, DH_A), F32),
                 jax.ShapeDtypeStruct((nseq, H_A, DH_A), F32),
                 jax.ShapeDtypeStruct((nseq, SUBLANES, GATE_LANES), F32)]
    scratch = []
    if packed:
        out_specs.append(hist_spec)
        out_shape.append(jax.ShapeDtypeStruct(hist.shape, F32))
    else:
        scratch.append(pltpu.VMEM((nsub, SUBLANES, D_MODEL), F32))
    outs = pl.pallas_call(
        kern,
        grid=(nseq // nsub, nchunk),
        in_specs=in_specs,
        out_specs=out_specs,
        out_shape=out_shape,
        scratch_shapes=scratch,
        compiler_params=_cparams("parallel", "arbitrary"),
        name="branch_a",
    )(*args)
    ya, c_new, n_new, m_new = outs[:4]
    hist_new = outs[4] if packed else None
    return ya.reshape(r, D_MODEL), c_new, n_new, m_new[:, 0, :H_A], hist_new


def _gmlp_kernel(u_ref, v_ref, nv_ref, wmix_ref, bmix_ref, yb_ref, *vn_ref):
    vn = _layer_norm_rows(jax.nn.gelu(v_ref[...])) * nv_ref[...]
    if vn_ref:
        vn_ref[0][...] = vn
    vnb = vn.astype(BF16)
    bias = bmix_ref[...]
    lane = lax.broadcasted_iota(jnp.int32, (1, GATE_LANES), 1)
    mixed = []
    for g in range(G_B):
        b_col = jnp.sum(jnp.where(lane == g, bias, 0.0), axis=1, keepdims=True)
        mixed.append(jnp.dot(wmix_ref[g], vnb[:, g * DG_B:(g + 1) * DG_B],
                             preferred_element_type=F32) + b_col)
    yb_ref[...] = (jax.nn.gelu(u_ref[...]) * jnp.concatenate(mixed, axis=1)).astype(yb_ref.dtype)


def gmlp(proj, norm_v, w_s, b_s, seq, *, emit_vn):
    r = proj.shape[0]
    lc = min(seq, CHUNK)
    per = CHUNK // lc
    tril = jnp.tril(jnp.ones((lc, lc), bool))
    w_small = jnp.where(tril[None], w_s[:, :lc, :lc], 0.0)
    wmix = jnp.einsum("ab,gts->gatbs", jnp.eye(per, dtype=F32), w_small)
    wmix = wmix.reshape(G_B, CHUNK, CHUNK).astype(BF16)
    bmix = jnp.pad(jnp.tile(b_s[:, :lc].T, (per, 1)), ((0, 0), (0, GATE_LANES - G_B)))
    u_col = 2 * D_MODEL // D_B
    row = pl.BlockSpec((CHUNK, D_B), lambda i: (i, 0))
    outs = pl.pallas_call(
        _gmlp_kernel,
        grid=(r // CHUNK,),
        in_specs=[pl.BlockSpec((CHUNK, D_B), lambda i: (i, u_col)),
                  pl.BlockSpec((CHUNK, D_B), lambda i: (i, u_col + 1)),
                  pl.BlockSpec((1, D_B), lambda i: (0, 0)),
                  pl.BlockSpec((G_B, CHUNK, CHUNK), lambda i: (0, 0, 0)),
                  pl.BlockSpec((CHUNK, GATE_LANES), lambda i: (0, 0))],
        out_specs=[row, row] if emit_vn else [row],
        out_shape=[jax.ShapeDtypeStruct((r, D_B), BF16)]
                  + ([jax.ShapeDtypeStruct((r, D_B), F32)] if emit_vn else []),
        compiler_params=_cparams("parallel"),
        name="gmlp",
    )(proj, proj, norm_v.reshape(1, D_B), wmix, bmix)
    return (outs[0], outs[1]) if emit_vn else (outs[0], None)


def _attend(q, k, v):
    outs = []
    for h in range(H_C):
        hs = slice(h * DH_C, (h + 1) * DH_C)
        s = lax.dot_general(q[:, hs], k[:, hs], (((1,), (1,)), ((), ())),
                            preferred_element_type=F32) * (DH_C ** -0.5)
        e = jnp.exp(s - jnp.max(s, axis=-1, keepdims=True))
        prob = e / jnp.sum(e, axis=-1, keepdims=True)
        outs.append(jnp.dot(prob.astype(BF16), v[:, hs], preferred_element_type=F32))
    return jnp.concatenate(outs, axis=1)


def _attend_stacked(q, k, v):
    col = lax.broadcasted_iota(jnp.int32, (SUBLANES, D_C), 1)
    head = [(col >= h * DH_C) & (col < (h + 1) * DH_C) for h in range(H_C)]
    q4 = jnp.concatenate([jnp.where(head[h], q, 0.0) for h in range(H_C)], axis=0).astype(BF16)
    s = lax.dot_general(q4, k, (((1,), (1,)), ((), ())), preferred_element_type=F32) * (DH_C ** -0.5)
    e = jnp.exp(s - jnp.max(s, axis=-1, keepdims=True))
    prob = e / jnp.sum(e, axis=-1, keepdims=True)
    o4 = jnp.dot(prob.astype(BF16), v, preferred_element_type=F32)
    out = o4[0:SUBLANES]
    for h in range(1, H_C):
        out = jnp.where(head[h], o4[h * SUBLANES:(h + 1) * SUBLANES], out)
    return out


def _attend_flat(q, kf, vf):
    qs = jnp.concatenate([q[:, h * DH_C:(h + 1) * DH_C] for h in range(H_C)], axis=0).astype(BF16)
    s = lax.dot_general(qs, kf, (((1,), (1,)), ((), ())), preferred_element_type=F32) * (DH_C ** -0.5)
    shape = (H_C * SUBLANES, MEM_LEN * H_C)
    row_head = lax.broadcasted_iota(jnp.int32, shape, 0) // SUBLANES
    col_head = lax.broadcasted_iota(jnp.int32, shape, 1) % H_C
    s = jnp.where(row_head == col_head, s, NEG_INF)
    e = jnp.exp(s - jnp.max(s, axis=-1, keepdims=True))
    prob = e / jnp.sum(e, axis=-1, keepdims=True)
    o = jnp.dot(prob.astype(BF16), vf, preferred_element_type=F32)
    return jnp.concatenate([o[h * SUBLANES:(h + 1) * SUBLANES] for h in range(H_C)], axis=1)


def _mem_rows(ref, i):
    if len(ref.shape) == 4:
        return jnp.concatenate([ref[i, :, h, :] for h in range(H_C)], axis=1).astype(BF16)
    return ref[i * MEM_LEN:(i + 1) * MEM_LEN, :].astype(BF16)


def _xattn_kernel(q_ref, k_ref, v_ref, o_ref, *, nseq, rq, valid):
    per = rq // valid
    for t in range(nseq // per):
        tile = q_ref[t * rq:(t + 1) * rq, :]
        parts = []
        for j in range(per):
            i = t * per + j
            if per == 1:
                parts.append(_attend(tile.astype(BF16), _mem_rows(k_ref, i), _mem_rows(v_ref, i)))
            else:
                kf = k_ref[i].reshape(MEM_LEN * H_C, DH_C).astype(BF16)
                vf = v_ref[i].reshape(MEM_LEN * H_C, DH_C).astype(BF16)
                parts.append(_attend_flat(_unpack(tile, j, valid), kf, vf))
        o = parts[0] if per == 1 else _pack(parts, valid)
        o_ref[t * rq:(t + 1) * rq, :] = o.astype(o_ref.dtype)


def xattn(proj, k_arr, v_arr, k_col, v_col, *, nbatch, nseq, rq, out_dtype):
    r = proj.shape[0]
    seq = r // nbatch
    valid = min(seq, rq)
    rows = nseq * valid
    nq = seq // valid
    assert nseq == 1 or nq == 1
    q_col = (2 * D_MODEL + 2 * D_B) // D_C

    def mem_spec(arr, col):
        if arr.ndim == 4:
            return pl.BlockSpec((nseq, MEM_LEN, H_C, DH_C), lambda g, t: (g, 0, 0, 0))
        return pl.BlockSpec((nseq * MEM_LEN, D_C), lambda g, t: (g, col))

    kern = functools.partial(_xattn_kernel, nseq=nseq, rq=rq, valid=valid)
    return pl.pallas_call(
        kern,
        grid=(nbatch // nseq, nq),
        in_specs=[pl.BlockSpec((rows, D_C), lambda g, t: (g * nq + t, q_col)),
                  mem_spec(k_arr, k_col), mem_spec(v_arr, v_col)],
        out_specs=pl.BlockSpec((rows, D_C), lambda g, t: (g * nq + t, 0)),
        out_shape=jax.ShapeDtypeStruct((r, D_C), out_dtype),
        compiler_params=_cparams("parallel", "arbitrary"),
        name="xattn",
    )(proj, k_arr, v_arr)


def _mix_out_kernel(ya_ref, yb_ref, yc_ref, ga0, ga1, gb0, gb1, gc0, gc1, x_ref,
                    wa_ref, wb_ref, wc_ref, wo_ref, g1_ref, g2_ref, x1_ref, h2_ref):
    def branch(y_ref, w_ref, lo, hi):
        gate = jnp.concatenate([lo[...], hi[...]], axis=1)
        return jax.nn.sigmoid(gate) * jnp.dot(y_ref[...].astype(BF16), w_ref[...],
                                              preferred_element_type=F32)

    merged = (branch(ya_ref, wa_ref, ga0, ga1) + branch(yb_ref, wb_ref, gb0, gb1)
              + branch(yc_ref, wc_ref, gc0, gc1))
    o = jnp.dot(merged.astype(BF16), wo_ref[...], preferred_element_type=F32)
    x1 = x_ref[...] + _rms(o) * g1_ref[...]
    x1_ref[...] = x1
    h2_ref[...] = (_rms(x1) * g2_ref[...]).astype(h2_ref.dtype)


def mix_out(ya, yb, yc, proj, x, w_a, w_b, w_c, w_o, g_post, g_pre, bm=256):
    r = x.shape[0]
    half = D_MODEL // 2
    g0 = GATE_COL0 // half
    gate = lambda k: pl.BlockSpec((bm, half), lambda i, k=k: (i, g0 + k))
    rowspec = lambda d: pl.BlockSpec((bm, d), lambda i: (i, 0))
    vec = pl.BlockSpec((1, D_MODEL), lambda i: (0, 0))
    return pl.pallas_call(
        _mix_out_kernel,
        grid=(r // bm,),
        in_specs=[rowspec(D_MODEL), rowspec(D_B), rowspec(D_C)] + [gate(k) for k in range(6)]
                 + [rowspec(D_MODEL), _resident((D_MODEL, D_MODEL)), _resident((D_B, D_MODEL)),
                    _resident((D_C, D_MODEL)), _resident((D_MODEL, D_MODEL)), vec, vec],
        out_specs=[rowspec(D_MODEL), rowspec(D_MODEL)],
        out_shape=[jax.ShapeDtypeStruct((r, D_MODEL), F32), jax.ShapeDtypeStruct((r, D_MODEL), BF16)],
        compiler_params=_cparams("parallel"),
        name="mix_out",
    )(ya, yb, yc, proj, proj, proj, proj, proj, proj, x, w_a, w_b, w_c, w_o,
      g_post.reshape(1, D_MODEL), g_pre.reshape(1, D_MODEL))


def _gated_act(gate, val):
    return jax.nn.gelu(gate) * val


def _ffn_up_act_kernel(a_ref, wg_ref, wv_ref, cwg_ref, cwv_ref, cbg_ref, cbv_ref,
                       o_ref, sg_ref, sv_ref, ext_ref, *, bm):
    @pl.when(pl.program_id(2) == 0)
    def _():
        ext_ref[:, 0:SUBLANES, :] = jnp.zeros((2, SUBLANES, ext_ref.shape[2]), F32)

    a = a_ref[...]

    def conv(which, w_ref, cw_ref, cb_ref, st_ref):
        u = jnp.dot(a, w_ref[...].astype(BF16), preferred_element_type=F32)
        ext_ref[which, SUBLANES:SUBLANES + bm, :] = u
        cw = cw_ref[...]
        y = cb_ref[...] + u * cw[CONV_F - 1:CONV_F, :]
        for j in range(CONV_F - 1):
            y = y + ext_ref[which, pl.ds(SUBLANES - (CONV_F - 1) + j, bm), :] * cw[j:j + 1, :]
        tail = u[bm - SUBLANES:bm, :]
        ext_ref[which, 0:SUBLANES, :] = tail
        st_ref[0] = tail
        return y

    gate = conv(0, wg_ref, cwg_ref, cbg_ref, sg_ref)
    val = conv(1, wv_ref, cwv_ref, cbv_ref, sv_ref)
    o_ref[...] = _gated_act(gate, val).astype(o_ref.dtype)


def ffn_up_act(h2, w_up, conv_w, conv_b, *, nseq, bm=1024, bn=512):
    r = h2.shape[0]
    tiles = r // nseq // bm
    ncol = D_FF // bn
    wspec = lambda off: pl.BlockSpec((D_MODEL, bn), lambda j, b, t, off=off: (0, off + j))
    cwspec = lambda off: pl.BlockSpec((CONV_F, bn), lambda j, b, t, off=off: (0, off + j))
    cbspec = lambda off: pl.BlockSpec((1, bn), lambda j, b, t, off=off: (0, off + j))
    st_spec = pl.BlockSpec((1, SUBLANES, bn), lambda j, b, t: (b, 0, j))
    kern = functools.partial(_ffn_up_act_kernel, bm=bm)
    return pl.pallas_call(
        kern,
        grid=(ncol, nseq, tiles),
        in_specs=[pl.BlockSpec((bm, D_MODEL), lambda j, b, t: (b * tiles + t, 0)),
                  wspec(0), wspec(ncol), cwspec(0), cwspec(ncol), cbspec(0), cbspec(ncol)],
        out_specs=[pl.BlockSpec((bm, bn), lambda j, b, t: (b * tiles + t, j)), st_spec, st_spec],
        out_shape=[jax.ShapeDtypeStruct((r, D_FF), BF16),
                   jax.ShapeDtypeStruct((nseq, SUBLANES, D_FF), F32),
                   jax.ShapeDtypeStruct((nseq, SUBLANES, D_FF), F32)],
        scratch_shapes=[pltpu.VMEM((2, SUBLANES + bm, bn), F32)],
        compiler_params=_cparams("parallel", "arbitrary", "arbitrary"),
        name="ffn_up_act",
    )(h2, w_up, w_up, conv_w, conv_w, conv_b, conv_b)


def _ffn_act_kernel(g_ref, v_ref, sg_ref, sv_ref, wg_ref, wv_ref, bg_ref, bv_ref,
                    o_ref, ng_ref, nv_ref, *, nseq, valid):
    per = SUBLANES // valid
    wg, wv, bg, bv = wg_ref[...], wv_ref[...], bg_ref[...], bv_ref[...]
    for t in range(nseq // per):
        rows = slice(t * SUBLANES, (t + 1) * SUBLANES)
        g_tile, v_tile = g_ref[rows, :], v_ref[rows, :]
        parts = []
        for j in range(per):
            i = t * per + j
            xg, xv = _unpack(g_tile, j, valid), _unpack(v_tile, j, valid)
            parts.append(_gated_act(_packed_conv(xg, sg_ref[i], wg, bg),
                                    _packed_conv(xv, sv_ref[i], wv, bv)))
            ng_ref[i] = _last_rows(xg, valid, CONV_F - 1)
            nv_ref[i] = _last_rows(xv, valid, CONV_F - 1)
        o_ref[rows, :] = _pack(parts, valid).astype(o_ref.dtype)


def ffn_act(up, hist, conv_w, conv_b, *, nbatch, nseq, bc=2048):
    r = up.shape[0]
    valid = r // nbatch
    ncol = D_FF // bc
    nh = CONV_F - 1
    x_spec = lambda off: pl.BlockSpec((nseq * valid, bc), lambda g, j, off=off: (g, off + j))
    s_spec = lambda off: pl.BlockSpec((nseq, nh, bc), lambda g, j, off=off: (g, 0, off + j))
    w_spec = lambda off: pl.BlockSpec((CONV_F, bc), lambda g, j, off=off: (0, off + j))
    b_spec = lambda off: pl.BlockSpec((1, bc), lambda g, j, off=off: (0, off + j))
    kern = functools.partial(_ffn_act_kernel, nseq=nseq, valid=valid)
    return pl.pallas_call(
        kern,
        grid=(nbatch // nseq, ncol),
        in_specs=[x_spec(0), x_spec(ncol), s_spec(0), s_spec(ncol), w_spec(0), w_spec(ncol),
                  b_spec(0), b_spec(ncol)],
        out_specs=[pl.BlockSpec((nseq * valid, bc), lambda g, j: (g, j)), s_spec(0), s_spec(0)],
        out_shape=[jax.ShapeDtypeStruct((r, D_FF), F32),
                   jax.ShapeDtypeStruct((nbatch, nh, D_FF), F32),
                   jax.ShapeDtypeStruct((nbatch, nh, D_FF), F32)],
        compiler_params=_cparams("parallel", "parallel"),
        name="ffn_act",
    )(up, up, hist, hist, conv_w, conv_w, conv_b, conv_b)


def _ffn_down_kernel(a_ref, w_ref, x_ref, g_ref, y_ref):
    f = jnp.dot(a_ref[...].astype(BF16), w_ref[...], preferred_element_type=F32)
    y_ref[...] = x_ref[...] + _rms(f) * g_ref[...]


def ffn_down_post(act, w_down, x1, g, bm=256):
    r = x1.shape[0]
    return pl.pallas_call(
        _ffn_down_kernel,
        grid=(r // bm,),
        in_specs=[pl.BlockSpec((bm, D_FF), lambda i: (i, 0)), _resident((D_FF, D_MODEL)),
                  pl.BlockSpec((bm, D_MODEL), lambda i: (i, 0)),
                  pl.BlockSpec((1, D_MODEL), lambda i: (0, 0))],
        out_specs=pl.BlockSpec((bm, D_MODEL), lambda i: (i, 0)),
        out_shape=jax.ShapeDtypeStruct((r, D_MODEL), F32),
        compiler_params=_cparams("parallel"),
        name="ffn_down_post",
    )(act, w_down, x1, g.reshape(1, D_MODEL))


def _blockdiag_tiles(w):
    nblk = D_MODEL // BD_BLOCK
    rows = w.reshape(nblk, BD_BLOCK, QKV_BLOCK)
    tiled = jnp.tile(rows, (1, 1, BD_BLOCK // QKV_BLOCK))
    idx = jnp.arange(BD_BLOCK) // QKV_BLOCK
    same_block = idx[:, None] == idx[None, :]
    return jnp.where(same_block[None], tiled, 0.0).astype(BF16)


def _prep_params(conv_a_w, conv_a_b, w_q_a, w_k_a, w_v_a, w_if_a, b_if_a, norm_h_a, skip_a):
    pad = GATE_LANES - 2 * H_A
    return {
        "conv_a_w": conv_a_w,
        "conv_a_b": conv_a_b.reshape(1, D_MODEL),
        "wq_bd": _blockdiag_tiles(w_q_a),
        "wk_bd": _blockdiag_tiles(w_k_a),
        "wv_bd": _blockdiag_tiles(w_v_a),
        "w_if": jnp.pad(w_if_a, ((0, 0), (0, pad))).astype(BF16),
        "b_if": jnp.pad(b_if_a, (0, pad)).reshape(1, GATE_LANES),
        "norm_h_a": norm_h_a.reshape(1, D_MODEL),
        "skip_a": skip_a.reshape(1, D_MODEL),
    }


def _trunk(x, mem, states, w, pa, *, nseq, seq):
    prompt = states is None
    k_arr, v_arr, k_col, v_col = mem

    h = rms_cast(x, w["norm_mix_pre"])
    proj = matmul(h, w["w_in"], 1024, 1024)

    yb, vn = gmlp(proj, w["norm_v_b"], w["w_s_b"], w["b_s_b"], seq, emit_vn=not prompt)
    ya, c_new, n_new, m_new, conv_new = branch_a(proj, None if prompt else states[:4], pa,
                                                 nseq=nseq, seq=seq,
                                                 out_dtype=BF16 if prompt else F32)
    if prompt:
        yc = xattn(proj, k_arr, v_arr, k_col, v_col, nbatch=nseq, nseq=1, rq=256, out_dtype=BF16)
        conv_new = proj.reshape(nseq, seq, N_IN)[:, seq - (CONV_A - 1):, :D_MODEL]
    else:
        yc = xattn(proj, k_arr, v_arr, k_col, v_col, nbatch=nseq, nseq=4, rq=SUBLANES, out_dtype=F32)

    x1, h2 = mix_out(ya, yb, yc, proj, x, w["w_br_a"], w["w_br_b"], w["w_br_c"], w["w_out"],
                     w["norm_mix_post"], w["norm_ffn_pre"])

    conv_b = w["conv_ffn_b"].reshape(1, 2 * D_FF)
    if prompt:
        act, tail_g, tail_v = ffn_up_act(h2, w["w_ffn_up"], w["conv_ffn_w"], conv_b, nseq=nseq)
        ffn_new = jnp.concatenate([tail_g, tail_v], axis=-1)[:, SUBLANES - (CONV_F - 1):]
    else:
        up = matmul(h2, w["w_ffn_up"], 1024, 1024)
        act, new_g, new_v = ffn_act(up, states[4], w["conv_ffn_w"], conv_b, nbatch=nseq, nseq=32)
        ffn_new = jnp.concatenate([new_g, new_v], axis=-1)
    y = ffn_down_post(act, w["w_ffn_down"], x1, w["norm_ffn_post"])
    return y, conv_new, ffn_new, vn, c_new, n_new, m_new


def kernel(x_prompt, x_sample, state_conv_a, state_mlstm_c, state_mlstm_n, state_mlstm_m, cache_mem_k, cache_mem_v, state_ffn_conv, mem_prompt, norm_mix_pre, w_in, conv_a_w, conv_a_b, w_q_a, w_k_a, w_v_a, w_if_a, b_if_a, norm_h_a, skip_a, w_br_a, norm_v_b, w_s_b, b_s_b, w_br_b, norm_mem, w_mem_kv, w_br_c, w_out, norm_mix_post, norm_ffn_pre, w_ffn_up, conv_ffn_w, conv_ffn_b, w_ffn_down, norm_ffn_post):
    nb, seq, _ = x_prompt.shape
    db, dseq, _ = x_sample.shape
    w = {
        "norm_mix_pre": norm_mix_pre[0], "w_in": w_in[0], "norm_v_b": norm_v_b[0],
        "w_s_b": w_s_b[0], "b_s_b": b_s_b[0],
        "w_br_a": w_br_a[0].astype(BF16), "w_br_b": w_br_b[0].astype(BF16),
        "w_br_c": w_br_c[0].astype(BF16), "w_out": w_out[0].astype(BF16),
        "norm_mix_post": norm_mix_post[0], "norm_ffn_pre": norm_ffn_pre[0], "w_ffn_up": w_ffn_up[0],
        "conv_ffn_w": conv_ffn_w[0], "conv_ffn_b": conv_ffn_b[0],
        "w_ffn_down": w_ffn_down[0].astype(BF16), "norm_ffn_post": norm_ffn_post[0],
    }
    pa = _prep_params(conv_a_w[0], conv_a_b[0], w_q_a[0], w_k_a[0], w_v_a[0], w_if_a[0], b_if_a[0],
                      norm_h_a[0], skip_a[0])

    mem_h = rms_cast(mem_prompt.reshape(nb * MEM_LEN, D_MODEL), norm_mem[0])
    mem_k = matmul(mem_h, w_mem_kv[0], 1024, 1024, 0, D_C)
    mem_v = matmul(mem_h, w_mem_kv[0], 1024, 1024, D_C, D_C)
    yp, conv_p, ffn_p, _, c_p, n_p, m_p = _trunk(
        x_prompt.reshape(nb * seq, D_MODEL), (mem_k, mem_v, 0, 0), None, w, pa, nseq=nb, seq=seq)

    m0 = jnp.pad(state_mlstm_m[0][:, None, :], ((0, 0), (0, SUBLANES - 1), (0, GATE_LANES - H_A)))
    ys, conv_s, ffn_s, vn_s, c_s, n_s, m_s = _trunk(
        x_sample.reshape(db * dseq, D_MODEL), (cache_mem_k[0], cache_mem_v[0], 0, 0),
        (state_conv_a[0], state_mlstm_c[0], state_mlstm_n[0], m0, state_ffn_conv[0]),
        w, pa, nseq=db, seq=dseq)

    return (
        yp.reshape(nb, seq, D_MODEL), ys.reshape(db, dseq, D_MODEL),
        conv_p[None], c_p[None], n_p[None], m_p[None],
        mem_k.reshape(1, nb, MEM_LEN, H_C, DH_C), mem_v.reshape(1, nb, MEM_LEN, H_C, DH_C),
        ffn_p[None],
        conv_s[None], c_s[None], n_s[None], m_s[None],
        vn_s.reshape(1, db, dseq, D_B), ffn_s[None],
    )
```

```python
import functools

import jax
import jax.numpy as jnp
from jax import lax
from jax.experimental import pallas as pl
from jax.experimental.pallas import tpu as pltpu

F32 = jnp.float32
BF16 = jnp.bfloat16

D_MODEL = 2048
H_A = 4
DH_A = 512
QKV_BLOCK = 4
CONV_A = 4
CONV_F = 3
D_B = 1024
G_B = 4
DG_B = 256
H_C = 4
DH_C = 256
D_C = 1024
MEM_LEN = 256
D_FF = 6144
N_IN = 13312
GATE_COL0 = 2 * D_MODEL + 2 * D_B + D_C
EPS = 1e-6
NEG_INF = -1e30

CHUNK = 128
BD_BLOCK = 256
GATE_LANES = 128
SUBLANES = 8
V7X_VMEM_BYTES = 64 * 1024 * 1024
VMEM_LIMIT = V7X_VMEM_BYTES - 8 * 1024 * 1024


def _cparams(*sem):
    return pltpu.CompilerParams(dimension_semantics=sem, vmem_limit_bytes=VMEM_LIMIT)


def _resident(shape):
    return pl.BlockSpec(shape, lambda *_: (0,) * len(shape), pipeline_mode=pl.Buffered(1))


def _rms(x):
    return x * lax.rsqrt(jnp.mean(x * x, axis=-1, keepdims=True) + EPS)


def _unpack(tile, i, valid):
    return tile if i == 0 else pltpu.roll(tile, SUBLANES - i * valid, axis=0)


def _packed_conv(x, hist, w, b):
    nh = hist.shape[0]
    row = lax.broadcasted_iota(jnp.int32, (SUBLANES, 1), 0)
    xs = x
    for k in range(nh):
        xs = jnp.where(row == SUBLANES - nh + k, hist[k:k + 1, :], xs)
    y = b + x * w[nh:nh + 1, :]
    for j in range(nh):
        y = y + pltpu.roll(xs, nh - j, axis=0) * w[j:j + 1, :]
    return y


def _last_rows(x, valid, n):
    return pltpu.roll(x, SUBLANES - (valid - n), axis=0)[0:n]


def _pack(parts, valid):
    row = lax.broadcasted_iota(jnp.int32, (SUBLANES, 1), 0)
    out = parts[0]
    for i in range(1, len(parts)):
        out = jnp.where(row < i * valid, out, pltpu.roll(parts[i], i * valid, axis=0))
    return out


def _rms_kernel(x_ref, g_ref, o_ref):
    o_ref[...] = (_rms(x_ref[...]) * g_ref[...]).astype(o_ref.dtype)


def rms_cast(x, g, tile=512):
    r, d = x.shape
    return pl.pallas_call(
        _rms_kernel,
        grid=(r // tile,),
        in_specs=[pl.BlockSpec((tile, d), lambda i: (i, 0)),
                  pl.BlockSpec((1, d), lambda i: (0, 0))],
        out_specs=pl.BlockSpec((tile, d), lambda i: (i, 0)),
        out_shape=jax.ShapeDtypeStruct((r, d), BF16),
        compiler_params=_cparams("parallel"),
        name="rms_cast",
    )(x, g.reshape(1, d))


def _mm_kernel(a_ref, w_ref, o_ref):
    o_ref[...] = jnp.dot(a_ref[...].astype(BF16), w_ref[...].astype(BF16),
                         preferred_element_type=F32)


def matmul(a, w, bm, bn, col0=0, n=None):
    r, k = a.shape
    n = w.shape[1] if n is None else n
    bm = min(bm, r)
    j0 = col0 // bn
    return pl.pallas_call(
        _mm_kernel,
        grid=(n // bn, r // bm),
        in_specs=[pl.BlockSpec((bm, k), lambda j, i: (i, 0)),
                  pl.BlockSpec((k, bn), lambda j, i: (0, j0 + j))],
        out_specs=pl.BlockSpec((bm, bn), lambda j, i: (i, j)),
        out_shape=jax.ShapeDtypeStruct((r, n), F32),
        compiler_params=_cparams("parallel", "arbitrary"),
        name="matmul",
    )(a, w)


def _log_sigmoid(x):
    return jnp.minimum(x, 0.0) - jnp.log1p(jnp.exp(-jnp.abs(x)))


def _layer_norm_rows(x):
    xc = x - jnp.mean(x, axis=-1, keepdims=True)
    return xc * lax.rsqrt(jnp.mean(xc * xc, axis=-1, keepdims=True) + EPS)


def _branch_a_kernel(*refs, nsub, tq, valid, nchunk, zero_state):
    n_in = 3 if zero_state else 6
    xm_ref, z_ref, st_ref = refs[:3]
    (cw_ref, cb_ref, wq_ref, wk_ref, wv_ref, wif_ref, bif_ref, nh_ref, sk_ref,
     ya_ref, c_ref, n_ref, m_ref) = refs[n_in:n_in + 13]
    lc = CHUNK
    packed = valid < tq
    carried = nchunk > 1
    chunk = pl.program_id(1)
    cs_ref, ext_ref = (refs[n_in + 13], None) if packed else (None, refs[n_in + 13])

    if carried:
        @pl.when(chunk == 0)
        def _init():
            ext_ref[...] = st_ref[...]
            if zero_state:
                c_ref[...] = jnp.zeros(c_ref.shape, F32)
                n_ref[...] = jnp.zeros(n_ref.shape, F32)
                m_ref[...] = jnp.zeros(m_ref.shape, F32)
            else:
                c_ref[...] = refs[3][...]
                n_ref[...] = refs[4][...]
                m_ref[...] = refs[5][...]
        cin_ref, nin_ref, min_ref = c_ref, n_ref, m_ref
    else:
        cin_ref, nin_ref, min_ref = refs[3:6]

    lane = lax.broadcasted_iota(jnp.int32, (1, GATE_LANES), 1)
    r = lax.broadcasted_iota(jnp.int32, (tq, lc), 0)
    s_idx = lax.broadcasted_iota(jnp.int32, (tq, lc), 1)
    tril = s_idx <= r
    eye = s_idx == r
    row_valid = lax.broadcasted_iota(jnp.int32, (tq, 1), 0) < valid
    lane_valid = lax.broadcasted_iota(jnp.int32, (1, lc), 1) < valid
    scale = DH_A ** -0.5
    cw = cw_ref[...]

    def blockdiag(xb, w_ref):
        nblk = D_MODEL // BD_BLOCK
        return jnp.concatenate(
            [jnp.dot(xb[:, j * BD_BLOCK:(j + 1) * BD_BLOCK], w_ref[j], preferred_element_type=F32)
             for j in range(nblk)], axis=1)

    def pad_rows(x):
        if tq == lc:
            return x
        return jnp.concatenate([x, jnp.zeros((lc - tq, x.shape[1]), x.dtype)], axis=0)

    packed_out = []
    for i in range(nsub):
        if packed:
            x_in = _unpack(xm_ref[0], i, valid)
            z_in = _unpack(z_ref[0], i, valid)
        else:
            x_in = xm_ref[i]

        if packed:
            y = _packed_conv(x_in, st_ref[i], cw, cb_ref[...])
            cs_ref[i] = _last_rows(x_in, valid, CONV_A - 1)
        else:
            hist = ext_ref[i] if carried else st_ref[i]
            xe = jnp.concatenate([hist, x_in], axis=0)
            y = cb_ref[...] + x_in * cw[CONV_A - 1:CONV_A, :]
            for j in range(CONV_A - 1):
                shifted = pltpu.roll(xe, CONV_A - 1 - j, axis=0)[SUBLANES:, :]
                y = y + shifted * cw[j:j + 1, :]
            if carried:
                ext_ref[i] = x_in[tq - SUBLANES:tq, :]
        xconv = y * jax.nn.sigmoid(y)

        xcb = xconv.astype(BF16)
        q32 = blockdiag(xcb, wq_ref)
        k32 = blockdiag(xcb, wk_ref)
        v32 = blockdiag(x_in.astype(BF16), wv_ref)
        qb, kb, vb = q32.astype(BF16), k32.astype(BF16), v32.astype(BF16)
        if_pre = (jnp.dot(qb, wif_ref[0:D_MODEL, :], preferred_element_type=F32)
                  + jnp.dot(kb, wif_ref[D_MODEL:2 * D_MODEL, :], preferred_element_type=F32)
                  + jnp.dot(vb, wif_ref[2 * D_MODEL:3 * D_MODEL, :], preferred_element_type=F32)
                  + bif_ref[...])
        log_f_all = _log_sigmoid(if_pre)
        kpad = pad_rows(k32).astype(BF16)
        vpad = pad_rows(v32).astype(BF16)

        m_all = min_ref[i, 0:1, :]
        m_out = m_all
        heads = []
        for h in range(H_A):
            hs = slice(h * DH_A, (h + 1) * DH_A)
            qh, kh, vh = qb[:, hs], kpad[:, hs], vpad[:, hs]
            li_col = jnp.sum(jnp.where(lane == h, if_pre, 0.0), axis=1, keepdims=True)
            lf_col = jnp.sum(jnp.where(lane == H_A + h, log_f_all, 0.0), axis=1, keepdims=True)
            if packed:
                li_col = jnp.where(row_valid, li_col, NEG_INF)
                lf_col = jnp.where(row_valid, lf_col, 0.0)
            m_prev = jnp.sum(jnp.where(lane == h, m_all, 0.0), axis=1, keepdims=True)

            b_row = jnp.sum(jnp.where(r <= s_idx, lf_col, 0.0), axis=0, keepdims=True)
            li_row = jnp.sum(jnp.where(eye, li_col, 0.0), axis=0, keepdims=True)
            if packed:
                li_row = jnp.where(lane_valid, li_row, NEG_INF)
            b_col = jnp.sum(jnp.where(eye, b_row, 0.0), axis=1, keepdims=True)

            d = jnp.where(tril, b_col - b_row + li_row, NEG_INF)
            inter = m_prev + b_col
            m_t = jnp.maximum(jnp.max(d, axis=1, keepdims=True), inter)
            w = jnp.exp(d - m_t)
            g = jnp.exp(inter - m_t)
            qk = lax.dot_general(qh, kh, (((1,), (1,)), ((), ())), preferred_element_type=F32)
            s = qk * (w * scale)
            c_h = cin_ref[i, h]
            n_h = nin_ref[i, h:h + 1, :]
            num = (g * jnp.dot(qh, c_h.astype(BF16), preferred_element_type=F32)
                   + jnp.dot(s.astype(BF16), vh, preferred_element_type=F32))
            den = (g * jnp.sum(q32[:, hs] * n_h, axis=1, keepdims=True)
                   + jnp.sum(s, axis=1, keepdims=True))
            ht = num / jnp.maximum(jnp.abs(den), jnp.exp(-m_t))

            b_last = jnp.sum(lf_col, axis=0, keepdims=True)
            a_col = b_last - b_col + li_col
            m_new = jnp.maximum(m_prev + b_last, jnp.max(a_col, axis=0, keepdims=True))
            decay = jnp.exp(m_prev + b_last - m_new)
            kw = k32[:, hs] * (jnp.exp(a_col - m_new) * scale)
            kwt = jnp.transpose(pad_rows(kw)).astype(BF16)
            c_ref[i, h] = decay * c_h + jnp.dot(kwt, vh, preferred_element_type=F32)
            n_ref[i, h:h + 1, :] = decay * n_h + jnp.sum(kw, axis=0, keepdims=True)
            m_out = jnp.where(lane == h, m_new, m_out)

            hn = _layer_norm_rows(ht) * nh_ref[:, hs]
            z_h = z_in[:, hs] if packed else z_ref[i, :, hs]
            ya_h = jax.nn.sigmoid(z_h) * (hn + sk_ref[:, hs] * xconv[:, hs])
            if packed:
                heads.append(ya_h)
            else:
                ya_ref[i, :, hs] = ya_h.astype(ya_ref.dtype)
        m_ref[i] = jnp.broadcast_to(m_out, (SUBLANES, GATE_LANES))
        if packed:
            packed_out.append(heads)
    if packed:
        for h in range(H_A):
            tile = _pack([heads[h] for heads in packed_out], valid)
            ya_ref[0, :, h * DH_A:(h + 1) * DH_A] = tile.astype(ya_ref.dtype)


def branch_a(proj, states, p, *, nseq, seq, out_dtype):
    r = proj.shape[0]
    zero_state = states is None
    if seq >= CHUNK:
        nsub, tq, valid, nchunk = 2, CHUNK, CHUNK, seq // CHUNK
        proj3 = proj.reshape(nseq, seq, N_IN)
        row = lambda col: pl.BlockSpec((nsub, tq, D_MODEL), lambda g, c, col=col: (g, c, col))
        ya_shape = (nseq, seq, D_MODEL)
        assert zero_state
        hist = jnp.zeros((nseq, SUBLANES, D_MODEL), F32)
    else:
        nsub, tq, valid, nchunk = SUBLANES // seq, SUBLANES, seq, 1
        proj3 = proj.reshape(r // SUBLANES, SUBLANES, N_IN)
        row = lambda col: pl.BlockSpec((1, tq, D_MODEL), lambda g, c, col=col: (g, 0, col))
        ya_shape = (r // SUBLANES, SUBLANES, D_MODEL)
        assert not zero_state
        hist = states[0]
    packed = valid < tq
    full = lambda shape: pl.BlockSpec(shape, lambda g, c: (0,) * len(shape))
    c_spec = pl.BlockSpec((nsub, H_A, DH_A, DH_A), lambda g, c: (g, 0, 0, 0))
    n_spec = pl.BlockSpec((nsub, H_A, DH_A), lambda g, c: (g, 0, 0))
    m_spec = pl.BlockSpec((nsub, SUBLANES, GATE_LANES), lambda g, c: (g, 0, 0))
    hist_spec = pl.BlockSpec((nsub,) + hist.shape[1:], lambda g, c: (g, 0, 0))
    in_specs = [row(0), row(1), hist_spec]
    args = [proj3, proj3, hist]
    if not zero_state:
        in_specs += [c_spec, n_spec, m_spec]
        args += list(states[1:])
    nblk = D_MODEL // BD_BLOCK
    in_specs += [full((CONV_A, D_MODEL)), full((1, D_MODEL)),
                 full((nblk, BD_BLOCK, BD_BLOCK)), full((nblk, BD_BLOCK, BD_BLOCK)),
                 full((nblk, BD_BLOCK, BD_BLOCK)),
                 full((3 * D_MODEL, GATE_LANES)), full((1, GATE_LANES)),
                 full((1, D_MODEL)), full((1, D_MODEL))]
    args += [p["conv_a_w"], p["conv_a_b"], p["wq_bd"], p["wk_bd"], p["wv_bd"], p["w_if"], p["b_if"],
             p["norm_h_a"], p["skip_a"]]
    kern = functools.partial(_branch_a_kernel, nsub=nsub, tq=tq, valid=valid, nchunk=nchunk,
                             zero_state=zero_state)
    out_specs = [row(0), c_spec, n_spec, m_spec]
    out_shape = [jax.ShapeDtypeStruct(ya_shape, out_dtype),
                 jax.ShapeDtypeStruct((nseq, H_A, DH_A, DH_A), F32),
                 jax.ShapeDtypeStruct((nseq, H_A, DH_A), F32),
                 jax.ShapeDtypeStruct((nseq, SUBLANES, GATE_LANES), F32)]
    scratch = []
    if packed:
        out_specs.append(hist_spec)
        out_shape.append(jax.ShapeDtypeStruct(hist.shape, F32))
    else:
        scratch.append(pltpu.VMEM((nsub, SUBLANES, D_MODEL), F32))
    outs = pl.pallas_call(
        kern,
        grid=(nseq // nsub, nchunk),
        in_specs=in_specs,
        out_specs=out_specs,
        out_shape=out_shape,
        scratch_shapes=scratch,
        compiler_params=_cparams("parallel", "arbitrary"),
        name="branch_a",
    )(*args)
    ya, c_new, n_new, m_new = outs[:4]
    hist_new = outs[4] if packed else None
    return ya.reshape(r, D_MODEL), c_new, n_new, m_new[:, 0, :H_A], hist_new


def _gmlp_kernel(u_ref, v_ref, nv_ref, wmix_ref, bmix_ref, yb_ref, *vn_ref):
    vn = _layer_norm_rows(jax.nn.gelu(v_ref[...])) * nv_ref[...]
    if vn_ref:
        vn_ref[0][...] = vn
    vnb = vn.astype(BF16)
    bias = bmix_ref[...]
    lane = lax.broadcasted_iota(jnp.int32, (1, GATE_LANES), 1)
    mixed = []
    for g in range(G_B):
        b_col = jnp.sum(jnp.where(lane == g, bias, 0.0), axis=1, keepdims=True)
        mixed.append(jnp.dot(wmix_ref[g], vnb[:, g * DG_B:(g + 1) * DG_B],
                             preferred_element_type=F32) + b_col)
    yb_ref[...] = (jax.nn.gelu(u_ref[...]) * jnp.concatenate(mixed, axis=1)).astype(yb_ref.dtype)


def gmlp(proj, norm_v, w_s, b_s, seq, *, emit_vn):
    r = proj.shape[0]
    lc = min(seq, CHUNK)
    per = CHUNK // lc
    tril = jnp.tril(jnp.ones((lc, lc), bool))
    w_small = jnp.where(tril[None], w_s[:, :lc, :lc], 0.0)
    wmix = jnp.einsum("ab,gts->gatbs", jnp.eye(per, dtype=F32), w_small)
    wmix = wmix.reshape(G_B, CHUNK, CHUNK).astype(BF16)
    bmix = jnp.pad(jnp.tile(b_s[:, :lc].T, (per, 1)), ((0, 0), (0, GATE_LANES - G_B)))
    u_col = 2 * D_MODEL // D_B
    row = pl.BlockSpec((CHUNK, D_B), lambda i: (i, 0))
    outs = pl.pallas_call(
        _gmlp_kernel,
        grid=(r // CHUNK,),
        in_specs=[pl.BlockSpec((CHUNK, D_B), lambda i: (i, u_col)),
                  pl.BlockSpec((CHUNK, D_B), lambda i: (i, u_col + 1)),
                  pl.BlockSpec((1, D_B), lambda i: (0, 0)),
                  pl.BlockSpec((G_B, CHUNK, CHUNK), lambda i: (0, 0, 0)),
                  pl.BlockSpec((CHUNK, GATE_LANES), lambda i: (0, 0))],
        out_specs=[row, row] if emit_vn else [row],
        out_shape=[jax.ShapeDtypeStruct((r, D_B), BF16)]
                  + ([jax.ShapeDtypeStruct((r, D_B), F32)] if emit_vn else []),
        compiler_params=_cparams("parallel"),
        name="gmlp",
    )(proj, proj, norm_v.reshape(1, D_B), wmix, bmix)
    return (outs[0], outs[1]) if emit_vn else (outs[0], None)


def _attend(q, k, v):
    outs = []
    for h in range(H_C):
        hs = slice(h * DH_C, (h + 1) * DH_C)
        s = lax.dot_general(q[:, hs], k[:, hs], (((1,), (1,)), ((), ())),
                            preferred_element_type=F32) * (DH_C ** -0.5)
        e = jnp.exp(s - jnp.max(s, axis=-1, keepdims=True))
        prob = e / jnp.sum(e, axis=-1, keepdims=True)
        outs.append(jnp.dot(prob.astype(BF16), v[:, hs], preferred_element_type=F32))
    return jnp.concatenate(outs, axis=1)


def _attend_flat(q, kf, vf):
    qs = jnp.concatenate([q[:, h * DH_C:(h + 1) * DH_C] for h in range(H_C)], axis=0).astype(BF16)
    s = lax.dot_general(qs, kf, (((1,), (1,)), ((), ())), preferred_element_type=F32) * (DH_C ** -0.5)
    shape = (H_C * SUBLANES, MEM_LEN * H_C)
    row_head = lax.broadcasted_iota(jnp.int32, shape, 0) // SUBLANES
    col_head = lax.broadcasted_iota(jnp.int32, shape, 1) % H_C
    s = jnp.where(row_head == col_head, s, NEG_INF)
    e = jnp.exp(s - jnp.max(s, axis=-1, keepdims=True))
    prob = e / jnp.sum(e, axis=-1, keepdims=True)
    o = jnp.dot(prob.astype(BF16), vf, preferred_element_type=F32)
    return jnp.concatenate([o[h * SUBLANES:(h + 1) * SUBLANES] for h in range(H_C)], axis=1)


def _xattn_kernel(q_ref, k_ref, v_ref, o_ref, *, nseq, rq, valid):
    per = rq // valid
    for t in range(nseq // per):
        tile = q_ref[t * rq:(t + 1) * rq, :]
        parts = []
        for j in range(per):
            i = t * per + j
            if per == 1:
                mem = slice(i * MEM_LEN, (i + 1) * MEM_LEN)
                parts.append(_attend(tile.astype(BF16), k_ref[mem, :].astype(BF16),
                                     v_ref[mem, :].astype(BF16)))
            else:
                kf = k_ref[i].reshape(MEM_LEN * H_C, DH_C).astype(BF16)
                vf = v_ref[i].reshape(MEM_LEN * H_C, DH_C).astype(BF16)
                parts.append(_attend_flat(_unpack(tile, j, valid), kf, vf))
        o = parts[0] if per == 1 else _pack(parts, valid)
        o_ref[t * rq:(t + 1) * rq, :] = o.astype(o_ref.dtype)


def xattn(proj, k_arr, v_arr, k_col, v_col, *, nbatch, nseq, rq, out_dtype):
    r = proj.shape[0]
    seq = r // nbatch
    valid = min(seq, rq)
    rows = nseq * valid
    nq = seq // valid
    assert nseq == 1 or nq == 1
    q_col = (2 * D_MODEL + 2 * D_B) // D_C

    def mem_spec(arr, col):
        if arr.ndim == 4:
            return pl.BlockSpec((nseq, MEM_LEN, H_C, DH_C), lambda g, t: (g, 0, 0, 0))
        return pl.BlockSpec((nseq * MEM_LEN, D_C), lambda g, t: (g, col))

    kern = functools.partial(_xattn_kernel, nseq=nseq, rq=rq, valid=valid)
    return pl.pallas_call(
        kern,
        grid=(nbatch // nseq, nq),
        in_specs=[pl.BlockSpec((rows, D_C), lambda g, t: (g * nq + t, q_col)),
                  mem_spec(k_arr, k_col), mem_spec(v_arr, v_col)],
        out_specs=pl.BlockSpec((rows, D_C), lambda g, t: (g * nq + t, 0)),
        out_shape=jax.ShapeDtypeStruct((r, D_C), out_dtype),
        compiler_params=_cparams("parallel", "arbitrary"),
        name="xattn",
    )(proj, k_arr, v_arr)


def _mix_out_kernel(ya_ref, yb_ref, yc_ref, ga0, ga1, gb0, gb1, gc0, gc1, x_ref,
                    wa_ref, wb_ref, wc_ref, wo_ref, g1_ref, g2_ref, x1_ref, h2_ref):
    def branch(y_ref, w_ref, lo, hi):
        gate = jnp.concatenate([lo[...], hi[...]], axis=1)
        return jax.nn.sigmoid(gate) * jnp.dot(y_ref[...].astype(BF16), w_ref[...],
                                              preferred_element_type=F32)

    merged = (branch(ya_ref, wa_ref, ga0, ga1) + branch(yb_ref, wb_ref, gb0, gb1)
              + branch(yc_ref, wc_ref, gc0, gc1))
    o = jnp.dot(merged.astype(BF16), wo_ref[...], preferred_element_type=F32)
    x1 = x_ref[...] + _rms(o) * g1_ref[...]
    x1_ref[...] = x1
    h2_ref[...] = (_rms(x1) * g2_ref[...]).astype(h2_ref.dtype)


def mix_out(ya, yb, yc, proj, x, w_a, w_b, w_c, w_o, g_post, g_pre, bm=256):
    r = x.shape[0]
    half = D_MODEL // 2
    g0 = GATE_COL0 // half
    gate = lambda k: pl.BlockSpec((bm, half), lambda i, k=k: (i, g0 + k))
    rowspec = lambda d: pl.BlockSpec((bm, d), lambda i: (i, 0))
    vec = pl.BlockSpec((1, D_MODEL), lambda i: (0, 0))
    return pl.pallas_call(
        _mix_out_kernel,
        grid=(r // bm,),
        in_specs=[rowspec(D_MODEL), rowspec(D_B), rowspec(D_C)] + [gate(k) for k in range(6)]
                 + [rowspec(D_MODEL), _resident((D_MODEL, D_MODEL)), _resident((D_B, D_MODEL)),
                    _resident((D_C, D_MODEL)), _resident((D_MODEL, D_MODEL)), vec, vec],
        out_specs=[rowspec(D_MODEL), rowspec(D_MODEL)],
        out_shape=[jax.ShapeDtypeStruct((r, D_MODEL), F32), jax.ShapeDtypeStruct((r, D_MODEL), BF16)],
        compiler_params=_cparams("parallel"),
        name="mix_out",
    )(ya, yb, yc, proj, proj, proj, proj, proj, proj, x, w_a, w_b, w_c, w_o,
      g_post.reshape(1, D_MODEL), g_pre.reshape(1, D_MODEL))


def _gated_act(gate, val):
    return jax.nn.gelu(gate) * val


def _ffn_up_act_kernel(a_ref, wg_ref, wv_ref, cwg_ref, cwv_ref, cbg_ref, cbv_ref,
                       o_ref, sg_ref, sv_ref, ext_ref, *, bm):
    @pl.when(pl.program_id(2) == 0)
    def _():
        ext_ref[:, 0:SUBLANES, :] = jnp.zeros((2, SUBLANES, ext_ref.shape[2]), F32)

    a = a_ref[...]

    def conv(which, w_ref, cw_ref, cb_ref, st_ref):
        u = jnp.dot(a, w_ref[...].astype(BF16), preferred_element_type=F32)
        ext_ref[which, SUBLANES:SUBLANES + bm, :] = u
        cw = cw_ref[...]
        y = cb_ref[...] + u * cw[CONV_F - 1:CONV_F, :]
        for j in range(CONV_F - 1):
            y = y + ext_ref[which, pl.ds(SUBLANES - (CONV_F - 1) + j, bm), :] * cw[j:j + 1, :]
        tail = u[bm - SUBLANES:bm, :]
        ext_ref[which, 0:SUBLANES, :] = tail
        st_ref[0] = tail
        return y

    gate = conv(0, wg_ref, cwg_ref, cbg_ref, sg_ref)
    val = conv(1, wv_ref, cwv_ref, cbv_ref, sv_ref)
    o_ref[...] = _gated_act(gate, val).astype(o_ref.dtype)


def ffn_up_act(h2, w_up, conv_w, conv_b, *, nseq, bm=1024, bn=512):
    r = h2.shape[0]
    tiles = r // nseq // bm
    ncol = D_FF // bn
    wspec = lambda off: pl.BlockSpec((D_MODEL, bn), lambda j, b, t, off=off: (0, off + j))
    cwspec = lambda off: pl.BlockSpec((CONV_F, bn), lambda j, b, t, off=off: (0, off + j))
    cbspec = lambda off: pl.BlockSpec((1, bn), lambda j, b, t, off=off: (0, off + j))
    st_spec = pl.BlockSpec((1, SUBLANES, bn), lambda j, b, t: (b, 0, j))
    kern = functools.partial(_ffn_up_act_kernel, bm=bm)
    return pl.pallas_call(
        kern,
        grid=(ncol, nseq, tiles),
        in_specs=[pl.BlockSpec((bm, D_MODEL), lambda j, b, t: (b * tiles + t, 0)),
                  wspec(0), wspec(ncol), cwspec(0), cwspec(ncol), cbspec(0), cbspec(ncol)],
        out_specs=[pl.BlockSpec((bm, bn), lambda j, b, t: (b * tiles + t, j)), st_spec, st_spec],
        out_shape=[jax.ShapeDtypeStruct((r, D_FF), BF16),
                   jax.ShapeDtypeStruct((nseq, SUBLANES, D_FF), F32),
                   jax.ShapeDtypeStruct((nseq, SUBLANES, D_FF), F32)],
        scratch_shapes=[pltpu.VMEM((2, SUBLANES + bm, bn), F32)],
        compiler_params=_cparams("parallel", "arbitrary", "arbitrary"),
        name="ffn_up_act",
    )(h2, w_up, w_up, conv_w, conv_w, conv_b, conv_b)


def _ffn_act_kernel(g_ref, v_ref, sg_ref, sv_ref, wg_ref, wv_ref, bg_ref, bv_ref,
                    o_ref, ng_ref, nv_ref, *, nseq, valid):
    per = SUBLANES // valid
    wg, wv, bg, bv = wg_ref[...], wv_ref[...], bg_ref[...], bv_ref[...]
    for t in range(nseq // per):
        rows = slice(t * SUBLANES, (t + 1) * SUBLANES)
        g_tile, v_tile = g_ref[rows, :], v_ref[rows, :]
        parts = []
        for j in range(per):
            i = t * per + j
            xg, xv = _unpack(g_tile, j, valid), _unpack(v_tile, j, valid)
            parts.append(_gated_act(_packed_conv(xg, sg_ref[i], wg, bg),
                                    _packed_conv(xv, sv_ref[i], wv, bv)))
            ng_ref[i] = _last_rows(xg, valid, CONV_F - 1)
            nv_ref[i] = _last_rows(xv, valid, CONV_F - 1)
        o_ref[rows, :] = _pack(parts, valid).astype(o_ref.dtype)


def ffn_act(up, hist, conv_w, conv_b, *, nbatch, nseq, bc=2048):
    r = up.shape[0]
    valid = r // nbatch
    ncol = D_FF // bc
    nh = CONV_F - 1
    x_spec = lambda off: pl.BlockSpec((nseq * valid, bc), lambda g, j, off=off: (g, off + j))
    s_spec = lambda off: pl.BlockSpec((nseq, nh, bc), lambda g, j, off=off: (g, 0, off + j))
    w_spec = lambda off: pl.BlockSpec((CONV_F, bc), lambda g, j, off=off: (0, off + j))
    b_spec = lambda off: pl.BlockSpec((1, bc), lambda g, j, off=off: (0, off + j))
    kern = functools.partial(_ffn_act_kernel, nseq=nseq, valid=valid)
    return pl.pallas_call(
        kern,
        grid=(nbatch // nseq, ncol),
        in_specs=[x_spec(0), x_spec(ncol), s_spec(0), s_spec(ncol), w_spec(0), w_spec(ncol),
                  b_spec(0), b_spec(ncol)],
        out_specs=[pl.BlockSpec((nseq * valid, bc), lambda g, j: (g, j)), s_spec(0), s_spec(0)],
        out_shape=[jax.ShapeDtypeStruct((r, D_FF), F32),
                   jax.ShapeDtypeStruct((nbatch, nh, D_FF), F32),
                   jax.ShapeDtypeStruct((nbatch, nh, D_FF), F32)],
        compiler_params=_cparams("parallel", "parallel"),
        name="ffn_act",
    )(up, up, hist, hist, conv_w, conv_w, conv_b, conv_b)


def _ffn_down_kernel(a_ref, w_ref, x_ref, g_ref, y_ref):
    f = jnp.dot(a_ref[...].astype(BF16), w_ref[...], preferred_element_type=F32)
    y_ref[...] = x_ref[...] + _rms(f) * g_ref[...]


def ffn_down_post(act, w_down, x1, g, bm=256):
    r = x1.shape[0]
    return pl.pallas_call(
        _ffn_down_kernel,
        grid=(r // bm,),
        in_specs=[pl.BlockSpec((bm, D_FF), lambda i: (i, 0)), _resident((D_FF, D_MODEL)),
                  pl.BlockSpec((bm, D_MODEL), lambda i: (i, 0)),
                  pl.BlockSpec((1, D_MODEL), lambda i: (0, 0))],
        out_specs=pl.BlockSpec((bm, D_MODEL), lambda i: (i, 0)),
        out_shape=jax.ShapeDtypeStruct((r, D_MODEL), F32),
        compiler_params=_cparams("parallel"),
        name="ffn_down_post",
    )(act, w_down, x1, g.reshape(1, D_MODEL))


def _blockdiag_tiles(w):
    nblk = D_MODEL // BD_BLOCK
    rows = w.reshape(nblk, BD_BLOCK, QKV_BLOCK)
    tiled = jnp.tile(rows, (1, 1, BD_BLOCK // QKV_BLOCK))
    idx = jnp.arange(BD_BLOCK) // QKV_BLOCK
    same_block = idx[:, None] == idx[None, :]
    return jnp.where(same_block[None], tiled, 0.0).astype(BF16)


def _prep_params(conv_a_w, conv_a_b, w_q_a, w_k_a, w_v_a, w_if_a, b_if_a, norm_h_a, skip_a):
    pad = GATE_LANES - 2 * H_A
    return {
        "conv_a_w": conv_a_w,
        "conv_a_b": conv_a_b.reshape(1, D_MODEL),
        "wq_bd": _blockdiag_tiles(w_q_a),
        "wk_bd": _blockdiag_tiles(w_k_a),
        "wv_bd": _blockdiag_tiles(w_v_a),
        "w_if": jnp.pad(w_if_a, ((0, 0), (0, pad))).astype(BF16),
        "b_if": jnp.pad(b_if_a, (0, pad)).reshape(1, GATE_LANES),
        "norm_h_a": norm_h_a.reshape(1, D_MODEL),
        "skip_a": skip_a.reshape(1, D_MODEL),
    }


def _trunk(x, mem, states, w, pa, *, nseq, seq):
    prompt = states is None
    k_arr, v_arr, k_col, v_col = mem

    h = rms_cast(x, w["norm_mix_pre"])
    proj = matmul(h, w["w_in"], 1024, 1024)

    yb, vn = gmlp(proj, w["norm_v_b"], w["w_s_b"], w["b_s_b"], seq, emit_vn=not prompt)
    ya, c_new, n_new, m_new, conv_new = branch_a(proj, None if prompt else states[:4], pa,
                                                 nseq=nseq, seq=seq,
                                                 out_dtype=BF16 if prompt else F32)
    if prompt:
        yc = xattn(proj, k_arr, v_arr, k_col, v_col, nbatch=nseq, nseq=1, rq=256, out_dtype=BF16)
        conv_new = proj.reshape(nseq, seq, N_IN)[:, seq - (CONV_A - 1):, :D_MODEL]
    else:
        yc = xattn(proj, k_arr, v_arr, k_col, v_col, nbatch=nseq, nseq=8, rq=SUBLANES, out_dtype=F32)

    x1, h2 = mix_out(ya, yb, yc, proj, x, w["w_br_a"], w["w_br_b"], w["w_br_c"], w["w_out"],
                     w["norm_mix_post"], w["norm_ffn_pre"])

    conv_b = w["conv_ffn_b"].reshape(1, 2 * D_FF)
    if prompt:
        act, tail_g, tail_v = ffn_up_act(h2, w["w_ffn_up"], w["conv_ffn_w"], conv_b, nseq=nseq)
        ffn_new = jnp.concatenate([tail_g, tail_v], axis=-1)[:, SUBLANES - (CONV_F - 1):]
    else:
        up = matmul(h2, w["w_ffn_up"], 1024, 1024)
        act, new_g, new_v = ffn_act(up, states[4], w["conv_ffn_w"], conv_b, nbatch=nseq, nseq=32)
        ffn_new = jnp.concatenate([new_g, new_v], axis=-1)
    y = ffn_down_post(act, w["w_ffn_down"], x1, w["norm_ffn_post"])
    return y, conv_new, ffn_new, vn, c_new, n_new, m_new


def kernel(x_prompt, x_sample, state_conv_a, state_mlstm_c, state_mlstm_n, state_mlstm_m, cache_mem_k, cache_mem_v, state_ffn_conv, mem_prompt, norm_mix_pre, w_in, conv_a_w, conv_a_b, w_q_a, w_k_a, w_v_a, w_if_a, b_if_a, norm_h_a, skip_a, w_br_a, norm_v_b, w_s_b, b_s_b, w_br_b, norm_mem, w_mem_kv, w_br_c, w_out, norm_mix_post, norm_ffn_pre, w_ffn_up, conv_ffn_w, conv_ffn_b, w_ffn_down, norm_ffn_post):
    nb, seq, _ = x_prompt.shape
    db, dseq, _ = x_sample.shape
    w = {
        "norm_mix_pre": norm_mix_pre[0], "w_in": w_in[0], "norm_v_b": norm_v_b[0],
        "w_s_b": w_s_b[0], "b_s_b": b_s_b[0],
        "w_br_a": w_br_a[0].astype(BF16), "w_br_b": w_br_b[0].astype(BF16),
        "w_br_c": w_br_c[0].astype(BF16), "w_out": w_out[0].astype(BF16),
        "norm_mix_post": norm_mix_post[0], "norm_ffn_pre": norm_ffn_pre[0], "w_ffn_up": w_ffn_up[0],
        "conv_ffn_w": conv_ffn_w[0], "conv_ffn_b": conv_ffn_b[0],
        "w_ffn_down": w_ffn_down[0].astype(BF16), "norm_ffn_post": norm_ffn_post[0],
    }
    pa = _prep_params(conv_a_w[0], conv_a_b[0], w_q_a[0], w_k_a[0], w_v_a[0], w_if_a[0], b_if_a[0],
                      norm_h_a[0], skip_a[0])

    mem_h = rms_cast(mem_prompt.reshape(nb * MEM_LEN, D_MODEL), norm_mem[0])
    mem_k = matmul(mem_h, w_mem_kv[0], 1024, 1024, 0, D_C)
    mem_v = matmul(mem_h, w_mem_kv[0], 1024, 1024, D_C, D_C)
    yp, conv_p, ffn_p, _, c_p, n_p, m_p = _trunk(
        x_prompt.reshape(nb * seq, D_MODEL), (mem_k, mem_v, 0, 0), None, w, pa, nseq=nb, seq=seq)

    m0 = jnp.pad(state_mlstm_m[0][:, None, :], ((0, 0), (0, SUBLANES - 1), (0, GATE_LANES - H_A)))
    ys, conv_s, ffn_s, vn_s, c_s, n_s, m_s = _trunk(
        x_sample.reshape(db * dseq, D_MODEL), (cache_mem_k[0], cache_mem_v[0], 0, 0),
        (state_conv_a[0], state_mlstm_c[0], state_mlstm_n[0], m0, state_ffn_conv[0]),
        w, pa, nseq=db, seq=dseq)

    return (
        yp.reshape(nb, seq, D_MODEL), ys.reshape(db, dseq, D_MODEL),
        conv_p[None], c_p[None], n_p[None], m_p[None],
        mem_k.reshape(1, nb, MEM_LEN, H_C, DH_C), mem_v.reshape(1, nb, MEM_LEN, H_C, DH_C),
        ffn_p[None],
        conv_s[None], c_s[None], n_s[None], m_s[None],
        vn_s.reshape(1, db, dseq, D_B), ffn_s[None],
    )
```

```python
import functools

import jax
import jax.numpy as jnp
from jax import lax
from jax.experimental import pallas as pl
from jax.experimental.pallas import tpu as pltpu

F32 = jnp.float32
BF16 = jnp.bfloat16

D_MODEL = 2048
H_A = 4
DH_A = 512
QKV_BLOCK = 4
CONV_A = 4
CONV_F = 3
D_B = 1024
G_B = 4
DG_B = 256
H_C = 4
DH_C = 256
D_C = 1024
MEM_LEN = 256
D_FF = 6144
N_IN = 13312
GATE_COL0 = 2 * D_MODEL + 2 * D_B + D_C
EPS = 1e-6
NEG_INF = -1e30

CHUNK = 128
BD_BLOCK = 256
GATE_LANES = 128
SUBLANES = 8
V7X_VMEM_BYTES = 64 * 1024 * 1024
VMEM_LIMIT = V7X_VMEM_BYTES - 8 * 1024 * 1024


def _cparams(*sem):
    return pltpu.CompilerParams(dimension_semantics=sem, vmem_limit_bytes=VMEM_LIMIT)


def _resident(shape):
    return pl.BlockSpec(shape, lambda *_: (0,) * len(shape), pipeline_mode=pl.Buffered(1))


def _rms(x):
    return x * lax.rsqrt(jnp.mean(x * x, axis=-1, keepdims=True) + EPS)


def _unpack(tile, i, valid):
    return tile if i == 0 else pltpu.roll(tile, SUBLANES - i * valid, axis=0)


def _packed_conv(x, hist, w, b):
    nh = hist.shape[0]
    row = lax.broadcasted_iota(jnp.int32, (SUBLANES, 1), 0)
    xs = x
    for k in range(nh):
        xs = jnp.where(row == SUBLANES - nh + k, hist[k:k + 1, :], xs)
    y = b + x * w[nh:nh + 1, :]
    for j in range(nh):
        y = y + pltpu.roll(xs, nh - j, axis=0) * w[j:j + 1, :]
    return y


def _packed_tile_conv(tile, hists, w, b, valid):
    nh = hists[0].shape[0]
    row = lax.broadcasted_iota(jnp.int32, (SUBLANES, 1), 0)
    y = b + tile * w[nh:nh + 1, :]
    for j in range(nh):
        s = nh - j
        tap = pltpu.roll(tile, s, axis=0)
        for i, hist in enumerate(hists):
            for t in range(min(s, valid)):
                tap = jnp.where(row == i * valid + t, hist[nh - s + t:nh - s + t + 1, :], tap)
        y = y + tap * w[j:j + 1, :]
    return y


def _last_rows(x, valid, n):
    return pltpu.roll(x, SUBLANES - (valid - n), axis=0)[0:n]


def _pack(parts, valid):
    row = lax.broadcasted_iota(jnp.int32, (SUBLANES, 1), 0)
    out = parts[0]
    for i in range(1, len(parts)):
        out = jnp.where(row < i * valid, out, pltpu.roll(parts[i], i * valid, axis=0))
    return out


def _rms_kernel(x_ref, g_ref, o_ref):
    o_ref[...] = (_rms(x_ref[...]) * g_ref[...]).astype(o_ref.dtype)


def rms_cast(x, g, tile=512):
    r, d = x.shape
    return pl.pallas_call(
        _rms_kernel,
        grid=(r // tile,),
        in_specs=[pl.BlockSpec((tile, d), lambda i: (i, 0)),
                  pl.BlockSpec((1, d), lambda i: (0, 0))],
        out_specs=pl.BlockSpec((tile, d), lambda i: (i, 0)),
        out_shape=jax.ShapeDtypeStruct((r, d), BF16),
        compiler_params=_cparams("parallel"),
        name="rms_cast",
    )(x, g.reshape(1, d))


def _mm_kernel(a_ref, w_ref, o_ref):
    o_ref[...] = jnp.dot(a_ref[...].astype(BF16), w_ref[...].astype(BF16),
                         preferred_element_type=F32)


def matmul(a, w, bm, bn, col0=0, n=None):
    r, k = a.shape
    n = w.shape[1] if n is None else n
    bm = min(bm, r)
    j0 = col0 // bn
    return pl.pallas_call(
        _mm_kernel,
        grid=(n // bn, r // bm),
        in_specs=[pl.BlockSpec((bm, k), lambda j, i: (i, 0)),
                  pl.BlockSpec((k, bn), lambda j, i: (0, j0 + j))],
        out_specs=pl.BlockSpec((bm, bn), lambda j, i: (i, j)),
        out_shape=jax.ShapeDtypeStruct((r, n), F32),
        compiler_params=_cparams("parallel", "arbitrary"),
        name="matmul",
    )(a, w)


def _log_sigmoid(x):
    return jnp.minimum(x, 0.0) - jnp.log1p(jnp.exp(-jnp.abs(x)))


def _layer_norm_rows(x):
    xc = x - jnp.mean(x, axis=-1, keepdims=True)
    return xc * lax.rsqrt(jnp.mean(xc * xc, axis=-1, keepdims=True) + EPS)


def _branch_a_kernel(*refs, nsub, tq, valid, nchunk, zero_state):
    n_in = 3 if zero_state else 6
    xm_ref, z_ref, st_ref = refs[:3]
    (cw_ref, cb_ref, wq_ref, wk_ref, wv_ref, wif_ref, bif_ref, nh_ref, sk_ref,
     ya_ref, c_ref, n_ref, m_ref) = refs[n_in:n_in + 13]
    lc = CHUNK
    packed = valid < tq
    carried = nchunk > 1
    chunk = pl.program_id(1)
    cs_ref, ext_ref = (refs[n_in + 13], None) if packed else (None, refs[n_in + 13])

    if carried:
        @pl.when(chunk == 0)
        def _init():
            ext_ref[...] = st_ref[...]
            if zero_state:
                c_ref[...] = jnp.zeros(c_ref.shape, F32)
                n_ref[...] = jnp.zeros(n_ref.shape, F32)
                m_ref[...] = jnp.zeros(m_ref.shape, F32)
            else:
                c_ref[...] = refs[3][...]
                n_ref[...] = refs[4][...]
                m_ref[...] = refs[5][...]
        cin_ref, nin_ref, min_ref = c_ref, n_ref, m_ref
    else:
        cin_ref, nin_ref, min_ref = refs[3:6]

    lane = lax.broadcasted_iota(jnp.int32, (1, GATE_LANES), 1)
    r = lax.broadcasted_iota(jnp.int32, (tq, lc), 0)
    s_idx = lax.broadcasted_iota(jnp.int32, (tq, lc), 1)
    tril = s_idx <= r
    eye = s_idx == r
    row_valid = lax.broadcasted_iota(jnp.int32, (tq, 1), 0) < valid
    lane_valid = lax.broadcasted_iota(jnp.int32, (1, lc), 1) < valid
    scale = DH_A ** -0.5
    cw = cw_ref[...]

    def blockdiag(xb, w_ref):
        nblk = D_MODEL // BD_BLOCK
        return jnp.concatenate(
            [jnp.dot(xb[:, j * BD_BLOCK:(j + 1) * BD_BLOCK], w_ref[j], preferred_element_type=F32)
             for j in range(nblk)], axis=1)

    def pad_rows(x):
        if tq == lc:
            return x
        return jnp.concatenate([x, jnp.zeros((lc - tq, x.shape[1]), x.dtype)], axis=0)

    def front(x_in, y):
        xconv = y * jax.nn.sigmoid(y)
        xcb = xconv.astype(BF16)
        q32 = blockdiag(xcb, wq_ref)
        k32 = blockdiag(xcb, wk_ref)
        v32 = blockdiag(x_in.astype(BF16), wv_ref)
        if_pre = (jnp.dot(q32.astype(BF16), wif_ref[0:D_MODEL, :], preferred_element_type=F32)
                  + jnp.dot(k32.astype(BF16), wif_ref[D_MODEL:2 * D_MODEL, :], preferred_element_type=F32)
                  + jnp.dot(v32.astype(BF16), wif_ref[2 * D_MODEL:3 * D_MODEL, :], preferred_element_type=F32)
                  + bif_ref[...])
        return xconv, q32, k32, v32, if_pre, _log_sigmoid(if_pre)

    if packed:
        tile = xm_ref[0]
        hists = [st_ref[i] for i in range(nsub)]
        shared = front(tile, _packed_tile_conv(tile, hists, cw, cb_ref[...], valid))

    packed_out = []
    for i in range(nsub):
        if packed:
            x_in = _unpack(tile, i, valid)
            z_in = _unpack(z_ref[0], i, valid)
            cs_ref[i] = _last_rows(x_in, valid, CONV_A - 1)
            xconv, q32, k32, v32, if_pre, log_f_all = (_unpack(a, i, valid) for a in shared)
        else:
            x_in = xm_ref[i]
            hist = ext_ref[i] if carried else st_ref[i]
            xe = jnp.concatenate([hist, x_in], axis=0)
            y = cb_ref[...] + x_in * cw[CONV_A - 1:CONV_A, :]
            for j in range(CONV_A - 1):
                shifted = pltpu.roll(xe, CONV_A - 1 - j, axis=0)[SUBLANES:, :]
                y = y + shifted * cw[j:j + 1, :]
            if carried:
                ext_ref[i] = x_in[tq - SUBLANES:tq, :]
            xconv, q32, k32, v32, if_pre, log_f_all = front(x_in, y)
        qb = q32.astype(BF16)
        kpad = pad_rows(k32).astype(BF16)
        vpad = pad_rows(v32).astype(BF16)

        m_all = min_ref[i, 0:1, :]
        m_out = m_all
        heads = []
        for h in range(H_A):
            hs = slice(h * DH_A, (h + 1) * DH_A)
            qh, kh, vh = qb[:, hs], kpad[:, hs], vpad[:, hs]
            li_col = jnp.sum(jnp.where(lane == h, if_pre, 0.0), axis=1, keepdims=True)
            lf_col = jnp.sum(jnp.where(lane == H_A + h, log_f_all, 0.0), axis=1, keepdims=True)
            if packed:
                li_col = jnp.where(row_valid, li_col, NEG_INF)
                lf_col = jnp.where(row_valid, lf_col, 0.0)
            m_prev = jnp.sum(jnp.where(lane == h, m_all, 0.0), axis=1, keepdims=True)

            b_row = jnp.sum(jnp.where(r <= s_idx, lf_col, 0.0), axis=0, keepdims=True)
            li_row = jnp.sum(jnp.where(eye, li_col, 0.0), axis=0, keepdims=True)
            if packed:
                li_row = jnp.where(lane_valid, li_row, NEG_INF)
            b_col = jnp.sum(jnp.where(eye, b_row, 0.0), axis=1, keepdims=True)

            d = jnp.where(tril, b_col - b_row + li_row, NEG_INF)
            inter = m_prev + b_col
            m_t = jnp.maximum(jnp.max(d, axis=1, keepdims=True), inter)
            w = jnp.exp(d - m_t)
            g = jnp.exp(inter - m_t)
            qk = lax.dot_general(qh, kh, (((1,), (1,)), ((), ())), preferred_element_type=F32)
            s = qk * (w * scale)
            c_h = cin_ref[i, h]
            n_h = nin_ref[i, h:h + 1, :]
            num = (g * jnp.dot(qh, c_h.astype(BF16), preferred_element_type=F32)
                   + jnp.dot(s.astype(BF16), vh, preferred_element_type=F32))
            den = (g * jnp.sum(q32[:, hs] * n_h, axis=1, keepdims=True)
                   + jnp.sum(s, axis=1, keepdims=True))
            ht = num / jnp.maximum(jnp.abs(den), jnp.exp(-m_t))

            b_last = jnp.sum(lf_col, axis=0, keepdims=True)
            a_col = b_last - b_col + li_col
            m_new = jnp.maximum(m_prev + b_last, jnp.max(a_col, axis=0, keepdims=True))
            decay = jnp.exp(m_prev + b_last - m_new)
            kw = k32[:, hs] * (jnp.exp(a_col - m_new) * scale)
            kwt = jnp.transpose(pad_rows(kw)).astype(BF16)
            c_ref[i, h] = decay * c_h + jnp.dot(kwt, vh, preferred_element_type=F32)
            n_ref[i, h:h + 1, :] = decay * n_h + jnp.sum(kw, axis=0, keepdims=True)
            m_out = jnp.where(lane == h, m_new, m_out)

            hn = _layer_norm_rows(ht) * nh_ref[:, hs]
            z_h = z_in[:, hs] if packed else z_ref[i, :, hs]
            ya_h = jax.nn.sigmoid(z_h) * (hn + sk_ref[:, hs] * xconv[:, hs])
            if packed:
                heads.append(ya_h)
            else:
                ya_ref[i, :, hs] = ya_h.astype(ya_ref.dtype)
        m_ref[i] = jnp.broadcast_to(m_out, (SUBLANES, GATE_LANES))
        if packed:
            packed_out.append(heads)
    if packed:
        for h in range(H_A):
            tile = _pack([heads[h] for heads in packed_out], valid)
            ya_ref[0, :, h * DH_A:(h + 1) * DH_A] = tile.astype(ya_ref.dtype)


def branch_a(proj, states, p, *, nseq, seq, out_dtype):
    r = proj.shape[0]
    zero_state = states is None
    if seq >= CHUNK:
        nsub, tq, valid, nchunk = 2, CHUNK, CHUNK, seq // CHUNK
        proj3 = proj.reshape(nseq, seq, N_IN)
        row = lambda col: pl.BlockSpec((nsub, tq, D_MODEL), lambda g, c, col=col: (g, c, col))
        ya_shape = (nseq, seq, D_MODEL)
        assert zero_state
        hist = jnp.zeros((nseq, SUBLANES, D_MODEL), F32)
    else:
        nsub, tq, valid, nchunk = SUBLANES // seq, SUBLANES, seq, 1
        proj3 = proj.reshape(r // SUBLANES, SUBLANES, N_IN)
        row = lambda col: pl.BlockSpec((1, tq, D_MODEL), lambda g, c, col=col: (g, 0, col))
        ya_shape = (r // SUBLANES, SUBLANES, D_MODEL)
        assert not zero_state
        hist = states[0]
    packed = valid < tq
    full = lambda shape: pl.BlockSpec(shape, lambda g, c: (0,) * len(shape))
    c_spec = pl.BlockSpec((nsub, H_A, DH_A, DH_A), lambda g, c: (g, 0, 0, 0))
    n_spec = pl.BlockSpec((nsub, H_A, DH_A), lambda g, c: (g, 0, 0))
    m_spec = pl.BlockSpec((nsub, SUBLANES, GATE_LANES), lambda g, c: (g, 0, 0))
    hist_spec = pl.BlockSpec((nsub,) + hist.shape[1:], lambda g, c: (g, 0, 0))
    in_specs = [row(0), row(1), hist_spec]
    args = [proj3, proj3, hist]
    if not zero_state:
        in_specs += [c_spec, n_spec, m_spec]
        args += list(states[1:])
    nblk = D_MODEL // BD_BLOCK
    in_specs += [full((CONV_A, D_MODEL)), full((1, D_MODEL)),
                 full((nblk, BD_BLOCK, BD_BLOCK)), full((nblk, BD_BLOCK, BD_BLOCK)),
                 full((nblk, BD_BLOCK, BD_BLOCK)),
                 full((3 * D_MODEL, GATE_LANES)), full((1, GATE_LANES)),
                 full((1, D_MODEL)), full((1, D_MODEL))]
    args += [p["conv_a_w"], p["conv_a_b"], p["wq_bd"], p["wk_bd"], p["wv_bd"], p["w_if"], p["b_if"],
             p["norm_h_a"], p["skip_a"]]
    kern = functools.partial(_branch_a_kernel, nsub=nsub, tq=tq, valid=valid, nchunk=nchunk,
                             zero_state=zero_state)
    out_specs = [row(0), c_spec, n_spec, m_spec]
    out_shape = [jax.ShapeDtypeStruct(ya_shape, out_dtype),
                 jax.ShapeDtypeStruct((nseq, H_A, DH_A, DH_A), F32),
                 jax.ShapeDtypeStruct((nseq, H_A, DH_A), F32),
                 jax.ShapeDtypeStruct((nseq, SUBLANES, GATE_LANES), F32)]
    scratch = []
    if packed:
        out_specs.append(hist_spec)
        out_shape.append(jax.ShapeDtypeStruct(hist.shape, F32))
    else:
        scratch.append(pltpu.VMEM((nsub, SUBLANES, D_MODEL), F32))
    outs = pl.pallas_call(
        kern,
        grid=(nseq // nsub, nchunk),
        in_specs=in_specs,
        out_specs=out_specs,
        out_shape=out_shape,
        scratch_shapes=scratch,
        compiler_params=_cparams("parallel", "arbitrary"),
        name="branch_a",
    )(*args)
    ya, c_new, n_new, m_new = outs[:4]
    hist_new = outs[4] if packed else None
    return ya.reshape(r, D_MODEL), c_new, n_new, m_new[:, 0, :H_A], hist_new


def _gmlp_kernel(u_ref, v_ref, nv_ref, wmix_ref, bmix_ref, yb_ref, *vn_ref):
    bias = bmix_ref[...]
    lane = lax.broadcasted_iota(jnp.int32, (1, GATE_LANES), 1)
    b_cols = [jnp.sum(jnp.where(lane == g, bias, 0.0), axis=1, keepdims=True) for g in range(G_B)]
    for c in range(u_ref.shape[0] // CHUNK):
        rows = slice(c * CHUNK, (c + 1) * CHUNK)
        vn = _layer_norm_rows(jax.nn.gelu(v_ref[rows, :])) * nv_ref[...]
        if vn_ref:
            vn_ref[0][rows, :] = vn
        vnb = vn.astype(BF16)
        mixed = [jnp.dot(wmix_ref[g], vnb[:, g * DG_B:(g + 1) * DG_B],
                         preferred_element_type=F32) + b_cols[g] for g in range(G_B)]
        yb_ref[rows, :] = (jax.nn.gelu(u_ref[rows, :])
                           * jnp.concatenate(mixed, axis=1)).astype(yb_ref.dtype)


def gmlp(proj, norm_v, w_s, b_s, seq, *, emit_vn, tile=512):
    r = proj.shape[0]
    lc = min(seq, CHUNK)
    per = CHUNK // lc
    tril = jnp.tril(jnp.ones((lc, lc), bool))
    w_small = jnp.where(tril[None], w_s[:, :lc, :lc], 0.0)
    wmix = jnp.einsum("ab,gts->gatbs", jnp.eye(per, dtype=F32), w_small)
    wmix = wmix.reshape(G_B, CHUNK, CHUNK).astype(BF16)
    bmix = jnp.pad(jnp.tile(b_s[:, :lc].T, (per, 1)), ((0, 0), (0, GATE_LANES - G_B)))
    u_col = 2 * D_MODEL // D_B
    row = pl.BlockSpec((tile, D_B), lambda i: (i, 0))
    outs = pl.pallas_call(
        _gmlp_kernel,
        grid=(r // tile,),
        in_specs=[pl.BlockSpec((tile, D_B), lambda i: (i, u_col)),
                  pl.BlockSpec((tile, D_B), lambda i: (i, u_col + 1)),
                  pl.BlockSpec((1, D_B), lambda i: (0, 0)),
                  pl.BlockSpec((G_B, CHUNK, CHUNK), lambda i: (0, 0, 0)),
                  pl.BlockSpec((CHUNK, GATE_LANES), lambda i: (0, 0))],
        out_specs=[row, row] if emit_vn else [row],
        out_shape=[jax.ShapeDtypeStruct((r, D_B), BF16)]
                  + ([jax.ShapeDtypeStruct((r, D_B), F32)] if emit_vn else []),
        compiler_params=_cparams("parallel"),
        name="gmlp",
    )(proj, proj, norm_v.reshape(1, D_B), wmix, bmix)
    return (outs[0], outs[1]) if emit_vn else (outs[0], None)


def _attend(q, k, v):
    outs = []
    for h in range(H_C):
        hs = slice(h * DH_C, (h + 1) * DH_C)
        s = lax.dot_general(q[:, hs], k[:, hs], (((1,), (1,)), ((), ())),
                            preferred_element_type=F32) * (DH_C ** -0.5)
        e = jnp.exp(s - jnp.max(s, axis=-1, keepdims=True))
        prob = e / jnp.sum(e, axis=-1, keepdims=True)
        outs.append(jnp.dot(prob.astype(BF16), v[:, hs], preferred_element_type=F32))
    return jnp.concatenate(outs, axis=1)


def _attend_flat(q, kf, vf):
    qs = jnp.concatenate([q[:, h * DH_C:(h + 1) * DH_C] for h in range(H_C)], axis=0).astype(BF16)
    s = lax.dot_general(qs, kf, (((1,), (1,)), ((), ())), preferred_element_type=F32) * (DH_C ** -0.5)
    shape = (H_C * SUBLANES, MEM_LEN * H_C)
    row_head = lax.broadcasted_iota(jnp.int32, shape, 0) // SUBLANES
    col_head = lax.broadcasted_iota(jnp.int32, shape, 1) % H_C
    s = jnp.where(row_head == col_head, s, NEG_INF)
    e = jnp.exp(s - jnp.max(s, axis=-1, keepdims=True))
    prob = e / jnp.sum(e, axis=-1, keepdims=True)
    o = jnp.dot(prob.astype(BF16), vf, preferred_element_type=F32)
    return jnp.concatenate([o[h * SUBLANES:(h + 1) * SUBLANES] for h in range(H_C)], axis=1)


def _xattn_kernel(q_ref, k_ref, v_ref, o_ref, *, nseq, rq, valid):
    per = rq // valid
    for t in range(nseq // per):
        tile = q_ref[t * rq:(t + 1) * rq, :]
        parts = []
        for j in range(per):
            i = t * per + j
            if per == 1:
                mem = slice(i * MEM_LEN, (i + 1) * MEM_LEN)
                parts.append(_attend(tile.astype(BF16), k_ref[mem, :].astype(BF16),
                                     v_ref[mem, :].astype(BF16)))
            else:
                kf = k_ref[i].reshape(MEM_LEN * H_C, DH_C).astype(BF16)
                vf = v_ref[i].reshape(MEM_LEN * H_C, DH_C).astype(BF16)
                parts.append(_attend_flat(_unpack(tile, j, valid), kf, vf))
        o = parts[0] if per == 1 else _pack(parts, valid)
        o_ref[t * rq:(t + 1) * rq, :] = o.astype(o_ref.dtype)


def xattn(proj, k_arr, v_arr, k_col, v_col, *, nbatch, nseq, rq, out_dtype):
    r = proj.shape[0]
    seq = r // nbatch
    valid = min(seq, rq)
    rows = nseq * valid
    nq = seq // valid
    assert nseq == 1 or nq == 1
    q_col = (2 * D_MODEL + 2 * D_B) // D_C

    def mem_spec(arr, col):
        if arr.ndim == 4:
            return pl.BlockSpec((nseq, MEM_LEN, H_C, DH_C), lambda g, t: (g, 0, 0, 0))
        return pl.BlockSpec((nseq * MEM_LEN, D_C), lambda g, t: (g, col))

    kern = functools.partial(_xattn_kernel, nseq=nseq, rq=rq, valid=valid)
    return pl.pallas_call(
        kern,
        grid=(nbatch // nseq, nq),
        in_specs=[pl.BlockSpec((rows, D_C), lambda g, t: (g * nq + t, q_col)),
                  mem_spec(k_arr, k_col), mem_spec(v_arr, v_col)],
        out_specs=pl.BlockSpec((rows, D_C), lambda g, t: (g * nq + t, 0)),
        out_shape=jax.ShapeDtypeStruct((r, D_C), out_dtype),
        compiler_params=_cparams("parallel", "arbitrary"),
        name="xattn",
    )(proj, k_arr, v_arr)


def _mix_out_kernel(ya_ref, yb_ref, yc_ref, ga0, ga1, gb0, gb1, gc0, gc1, x_ref,
                    wa_ref, wb_ref, wc_ref, wo_ref, g1_ref, g2_ref, x1_ref, h2_ref):
    def branch(y_ref, w_ref, lo, hi):
        gate = jnp.concatenate([lo[...], hi[...]], axis=1)
        return jax.nn.sigmoid(gate) * jnp.dot(y_ref[...].astype(BF16), w_ref[...],
                                              preferred_element_type=F32)

    merged = (branch(ya_ref, wa_ref, ga0, ga1) + branch(yb_ref, wb_ref, gb0, gb1)
              + branch(yc_ref, wc_ref, gc0, gc1))
    o = jnp.dot(merged.astype(BF16), wo_ref[...], preferred_element_type=F32)
    x1 = x_ref[...] + _rms(o) * g1_ref[...]
    x1_ref[...] = x1
    h2_ref[...] = (_rms(x1) * g2_ref[...]).astype(h2_ref.dtype)


def mix_out(ya, yb, yc, proj, x, w_a, w_b, w_c, w_o, g_post, g_pre, bm=256):
    r = x.shape[0]
    half = D_MODEL // 2
    g0 = GATE_COL0 // half
    gate = lambda k: pl.BlockSpec((bm, half), lambda i, k=k: (i, g0 + k))
    rowspec = lambda d: pl.BlockSpec((bm, d), lambda i: (i, 0))
    vec = pl.BlockSpec((1, D_MODEL), lambda i: (0, 0))
    return pl.pallas_call(
        _mix_out_kernel,
        grid=(r // bm,),
        in_specs=[rowspec(D_MODEL), rowspec(D_B), rowspec(D_C)] + [gate(k) for k in range(6)]
                 + [rowspec(D_MODEL), _resident((D_MODEL, D_MODEL)), _resident((D_B, D_MODEL)),
                    _resident((D_C, D_MODEL)), _resident((D_MODEL, D_MODEL)), vec, vec],
        out_specs=[rowspec(D_MODEL), rowspec(D_MODEL)],
        out_shape=[jax.ShapeDtypeStruct((r, D_MODEL), F32), jax.ShapeDtypeStruct((r, D_MODEL), BF16)],
        compiler_params=_cparams("parallel"),
        name="mix_out",
    )(ya, yb, yc, proj, proj, proj, proj, proj, proj, x, w_a, w_b, w_c, w_o,
      g_post.reshape(1, D_MODEL), g_pre.reshape(1, D_MODEL))


def _gated_act(gate, val):
    return jax.nn.gelu(gate) * val


def _ffn_up_act_kernel(a_ref, wg_ref, wv_ref, cwg_ref, cwv_ref, cbg_ref, cbv_ref,
                       o_ref, sg_ref, sv_ref, ext_ref, *, bm):
    @pl.when(pl.program_id(2) == 0)
    def _():
        ext_ref[:, 0:SUBLANES, :] = jnp.zeros((2, SUBLANES, ext_ref.shape[2]), F32)

    a = a_ref[...]

    def conv(which, w_ref, cw_ref, cb_ref, st_ref):
        u = jnp.dot(a, w_ref[...].astype(BF16), preferred_element_type=F32)
        ext_ref[which, SUBLANES:SUBLANES + bm, :] = u
        cw = cw_ref[...]
        y = cb_ref[...] + u * cw[CONV_F - 1:CONV_F, :]
        for j in range(CONV_F - 1):
            y = y + ext_ref[which, pl.ds(SUBLANES - (CONV_F - 1) + j, bm), :] * cw[j:j + 1, :]
        tail = u[bm - SUBLANES:bm, :]
        ext_ref[which, 0:SUBLANES, :] = tail
        st_ref[0] = tail
        return y

    gate = conv(0, wg_ref, cwg_ref, cbg_ref, sg_ref)
    val = conv(1, wv_ref, cwv_ref, cbv_ref, sv_ref)
    o_ref[...] = _gated_act(gate, val).astype(o_ref.dtype)


def ffn_up_act(h2, w_up, conv_w, conv_b, *, nseq, bm=1024, bn=512):
    r = h2.shape[0]
    tiles = r // nseq // bm
    ncol = D_FF // bn
    wspec = lambda off: pl.BlockSpec((D_MODEL, bn), lambda j, b, t, off=off: (0, off + j))
    cwspec = lambda off: pl.BlockSpec((CONV_F, bn), lambda j, b, t, off=off: (0, off + j))
    cbspec = lambda off: pl.BlockSpec((1, bn), lambda j, b, t, off=off: (0, off + j))
    st_spec = pl.BlockSpec((1, SUBLANES, bn), lambda j, b, t: (b, 0, j))
    kern = functools.partial(_ffn_up_act_kernel, bm=bm)
    return pl.pallas_call(
        kern,
        grid=(ncol, nseq, tiles),
        in_specs=[pl.BlockSpec((bm, D_MODEL), lambda j, b, t: (b * tiles + t, 0)),
                  wspec(0), wspec(ncol), cwspec(0), cwspec(ncol), cbspec(0), cbspec(ncol)],
        out_specs=[pl.BlockSpec((bm, bn), lambda j, b, t: (b * tiles + t, j)), st_spec, st_spec],
        out_shape=[jax.ShapeDtypeStruct((r, D_FF), BF16),
                   jax.ShapeDtypeStruct((nseq, SUBLANES, D_FF), F32),
                   jax.ShapeDtypeStruct((nseq, SUBLANES, D_FF), F32)],
        scratch_shapes=[pltpu.VMEM((2, SUBLANES + bm, bn), F32)],
        compiler_params=_cparams("parallel", "arbitrary", "arbitrary"),
        name="ffn_up_act",
    )(h2, w_up, w_up, conv_w, conv_w, conv_b, conv_b)


def _ffn_act_kernel(g_ref, v_ref, sg_ref, sv_ref, wg_ref, wv_ref, bg_ref, bv_ref,
                    o_ref, ng_ref, nv_ref, *, nseq, valid):
    per = SUBLANES // valid
    wg, wv, bg, bv = wg_ref[...], wv_ref[...], bg_ref[...], bv_ref[...]
    for t in range(nseq // per):
        rows = slice(t * SUBLANES, (t + 1) * SUBLANES)
        g_tile, v_tile = g_ref[rows, :], v_ref[rows, :]
        parts = []
        for j in range(per):
            i = t * per + j
            xg, xv = _unpack(g_tile, j, valid), _unpack(v_tile, j, valid)
            parts.append(_gated_act(_packed_conv(xg, sg_ref[i], wg, bg),
                                    _packed_conv(xv, sv_ref[i], wv, bv)))
            ng_ref[i] = _last_rows(xg, valid, CONV_F - 1)
            nv_ref[i] = _last_rows(xv, valid, CONV_F - 1)
        o_ref[rows, :] = _pack(parts, valid).astype(o_ref.dtype)


def ffn_act(up, hist, conv_w, conv_b, *, nbatch, nseq, bc=2048):
    r = up.shape[0]
    valid = r // nbatch
    ncol = D_FF // bc
    nh = CONV_F - 1
    x_spec = lambda off: pl.BlockSpec((nseq * valid, bc), lambda g, j, off=off: (g, off + j))
    s_spec = lambda off: pl.BlockSpec((nseq, nh, bc), lambda g, j, off=off: (g, 0, off + j))
    w_spec = lambda off: pl.BlockSpec((CONV_F, bc), lambda g, j, off=off: (0, off + j))
    b_spec = lambda off: pl.BlockSpec((1, bc), lambda g, j, off=off: (0, off + j))
    kern = functools.partial(_ffn_act_kernel, nseq=nseq, valid=valid)
    return pl.pallas_call(
        kern,
        grid=(nbatch // nseq, ncol),
        in_specs=[x_spec(0), x_spec(ncol), s_spec(0), s_spec(ncol), w_spec(0), w_spec(ncol),
                  b_spec(0), b_spec(ncol)],
        out_specs=[pl.BlockSpec((nseq * valid, bc), lambda g, j: (g, j)), s_spec(0), s_spec(0)],
        out_shape=[jax.ShapeDtypeStruct((r, D_FF), F32),
                   jax.ShapeDtypeStruct((nbatch, nh, D_FF), F32),
                   jax.ShapeDtypeStruct((nbatch, nh, D_FF), F32)],
        compiler_params=_cparams("parallel", "parallel"),
        name="ffn_act",
    )(up, up, hist, hist, conv_w, conv_w, conv_b, conv_b)


def _ffn_down_kernel(a_ref, w_ref, x_ref, g_ref, y_ref):
    f = jnp.dot(a_ref[...].astype(BF16), w_ref[...], preferred_element_type=F32)
    y_ref[...] = x_ref[...] + _rms(f) * g_ref[...]


def ffn_down_post(act, w_down, x1, g, bm=256):
    r = x1.shape[0]
    return pl.pallas_call(
        _ffn_down_kernel,
        grid=(r // bm,),
        in_specs=[pl.BlockSpec((bm, D_FF), lambda i: (i, 0)), _resident((D_FF, D_MODEL)),
                  pl.BlockSpec((bm, D_MODEL), lambda i: (i, 0)),
                  pl.BlockSpec((1, D_MODEL), lambda i: (0, 0))],
        out_specs=pl.BlockSpec((bm, D_MODEL), lambda i: (i, 0)),
        out_shape=jax.ShapeDtypeStruct((r, D_MODEL), F32),
        compiler_params=_cparams("parallel"),
        name="ffn_down_post",
    )(act, w_down, x1, g.reshape(1, D_MODEL))


def _blockdiag_tiles(w):
    nblk = D_MODEL // BD_BLOCK
    rows = w.reshape(nblk, BD_BLOCK, QKV_BLOCK)
    tiled = jnp.tile(rows, (1, 1, BD_BLOCK // QKV_BLOCK))
    idx = jnp.arange(BD_BLOCK) // QKV_BLOCK
    same_block = idx[:, None] == idx[None, :]
    return jnp.where(same_block[None], tiled, 0.0).astype(BF16)


def _prep_params(conv_a_w, conv_a_b, w_q_a, w_k_a, w_v_a, w_if_a, b_if_a, norm_h_a, skip_a):
    pad = GATE_LANES - 2 * H_A
    return {
        "conv_a_w": conv_a_w,
        "conv_a_b": conv_a_b.reshape(1, D_MODEL),
        "wq_bd": _blockdiag_tiles(w_q_a),
        "wk_bd": _blockdiag_tiles(w_k_a),
        "wv_bd": _blockdiag_tiles(w_v_a),
        "w_if": jnp.pad(w_if_a, ((0, 0), (0, pad))).astype(BF16),
        "b_if": jnp.pad(b_if_a, (0, pad)).reshape(1, GATE_LANES),
        "norm_h_a": norm_h_a.reshape(1, D_MODEL),
        "skip_a": skip_a.reshape(1, D_MODEL),
    }


def _trunk(x, mem, states, w, pa, *, nseq, seq):
    prompt = states is None
    k_arr, v_arr, k_col, v_col = mem

    h = rms_cast(x, w["norm_mix_pre"])
    proj = matmul(h, w["w_in"], 1024, 1024)

    yb, vn = gmlp(proj, w["norm_v_b"], w["w_s_b"], w["b_s_b"], seq, emit_vn=not prompt)
    ya, c_new, n_new, m_new, conv_new = branch_a(proj, None if prompt else states[:4], pa,
                                                 nseq=nseq, seq=seq,
                                                 out_dtype=BF16 if prompt else F32)
    if prompt:
        yc = xattn(proj, k_arr, v_arr, k_col, v_col, nbatch=nseq, nseq=1, rq=1024, out_dtype=BF16)
        conv_new = proj.reshape(nseq, seq, N_IN)[:, seq - (CONV_A - 1):, :D_MODEL]
    else:
        yc = xattn(proj, k_arr, v_arr, k_col, v_col, nbatch=nseq, nseq=8, rq=SUBLANES, out_dtype=F32)

    x1, h2 = mix_out(ya, yb, yc, proj, x, w["w_br_a"], w["w_br_b"], w["w_br_c"], w["w_out"],
                     w["norm_mix_post"], w["norm_ffn_pre"])

    conv_b = w["conv_ffn_b"].reshape(1, 2 * D_FF)
    if prompt:
        act, tail_g, tail_v = ffn_up_act(h2, w["w_ffn_up"], w["conv_ffn_w"], conv_b, nseq=nseq)
        ffn_new = jnp.concatenate([tail_g, tail_v], axis=-1)[:, SUBLANES - (CONV_F - 1):]
    else:
        up = matmul(h2, w["w_ffn_up"], 1024, 1024)
        act, new_g, new_v = ffn_act(up, states[4], w["conv_ffn_w"], conv_b, nbatch=nseq, nseq=32)
        ffn_new = jnp.concatenate([new_g, new_v], axis=-1)
    y = ffn_down_post(act, w["w_ffn_down"], x1, w["norm_ffn_post"])
    return y, conv_new, ffn_new, vn, c_new, n_new, m_new


def kernel(x_prompt, x_sample, state_conv_a, state_mlstm_c, state_mlstm_n, state_mlstm_m, cache_mem_k, cache_mem_v, state_ffn_conv, mem_prompt, norm_mix_pre, w_in, conv_a_w, conv_a_b, w_q_a, w_k_a, w_v_a, w_if_a, b_if_a, norm_h_a, skip_a, w_br_a, norm_v_b, w_s_b, b_s_b, w_br_b, norm_mem, w_mem_kv, w_br_c, w_out, norm_mix_post, norm_ffn_pre, w_ffn_up, conv_ffn_w, conv_ffn_b, w_ffn_down, norm_ffn_post):
    nb, seq, _ = x_prompt.shape
    db, dseq, _ = x_sample.shape
    w = {
        "norm_mix_pre": norm_mix_pre[0], "w_in": w_in[0], "norm_v_b": norm_v_b[0],
        "w_s_b": w_s_b[0], "b_s_b": b_s_b[0],
        "w_br_a": w_br_a[0].astype(BF16), "w_br_b": w_br_b[0].astype(BF16),
        "w_br_c": w_br_c[0].astype(BF16), "w_out": w_out[0].astype(BF16),
        "norm_mix_post": norm_mix_post[0], "norm_ffn_pre": norm_ffn_pre[0], "w_ffn_up": w_ffn_up[0],
        "conv_ffn_w": conv_ffn_w[0], "conv_ffn_b": conv_ffn_b[0],
        "w_ffn_down": w_ffn_down[0].astype(BF16), "norm_ffn_post": norm_ffn_post[0],
    }
    pa = _prep_params(conv_a_w[0], conv_a_b[0], w_q_a[0], w_k_a[0], w_v_a[0], w_if_a[0], b_if_a[0],
                      norm_h_a[0], skip_a[0])

    mem_h = rms_cast(mem_prompt.reshape(nb * MEM_LEN, D_MODEL), norm_mem[0])
    mem_k = matmul(mem_h, w_mem_kv[0], 1024, 1024, 0, D_C)
    mem_v = matmul(mem_h, w_mem_kv[0], 1024, 1024, D_C, D_C)
    yp, conv_p, ffn_p, _, c_p, n_p, m_p = _trunk(
        x_prompt.reshape(nb * seq, D_MODEL), (mem_k, mem_v, 0, 0), None, w, pa, nseq=nb, seq=seq)

    m0 = jnp.pad(state_mlstm_m[0][:, None, :], ((0, 0), (0, SUBLANES - 1), (0, GATE_LANES - H_A)))
    ys, conv_s, ffn_s, vn_s, c_s, n_s, m_s = _trunk(
        x_sample.reshape(db * dseq, D_MODEL), (cache_mem_k[0], cache_mem_v[0], 0, 0),
        (state_conv_a[0], state_mlstm_c[0], state_mlstm_n[0], m0, state_ffn_conv[0]),
        w, pa, nseq=db, seq=dseq)

    return (
        yp.reshape(nb, seq, D_MODEL), ys.reshape(db, dseq, D_MODEL),
        conv_p[None], c_p[None], n_p[None], m_p[None],
        mem_k.reshape(1, nb, MEM_LEN, H_C, DH_C), mem_v.reshape(1, nb, MEM_LEN, H_C, DH_C),
        ffn_p[None],
        conv_s[None], c_s[None], n_s[None], m_s[None],
        vn_s.reshape(1, db, dseq, D_B), ffn_s[None],
    )
```

```python
import functools

import jax
import jax.numpy as jnp
from jax import lax
from jax.experimental import pallas as pl
from jax.experimental.pallas import tpu as pltpu

F32 = jnp.float32
BF16 = jnp.bfloat16

D_MODEL = 2048
H_A = 4
DH_A = 512
QKV_BLOCK = 4
CONV_A = 4
CONV_F = 3
D_B = 1024
G_B = 4
DG_B = 256
H_C = 4
DH_C = 256
D_C = 1024
MEM_LEN = 256
D_FF = 6144
N_IN = 13312
GATE_COL0 = 2 * D_MODEL + 2 * D_B + D_C
EPS = 1e-6
NEG_INF = -1e30

CHUNK = 128
BD_BLOCK = 256
GATE_LANES = 128
SUBLANES = 8
V7X_VMEM_BYTES = 64 * 1024 * 1024
VMEM_LIMIT = V7X_VMEM_BYTES - 8 * 1024 * 1024


def _cparams(*sem):
    return pltpu.CompilerParams(dimension_semantics=sem, vmem_limit_bytes=VMEM_LIMIT)


def _resident(shape):
    return pl.BlockSpec(shape, lambda *_: (0,) * len(shape), pipeline_mode=pl.Buffered(1))


def _rms(x):
    return x * lax.rsqrt(jnp.mean(x * x, axis=-1, keepdims=True) + EPS)


def _unpack(tile, i, valid):
    return tile if i == 0 else pltpu.roll(tile, SUBLANES - i * valid, axis=0)


def _packed_tile_conv(tile, hists, w, b, valid):
    nh = hists[0].shape[0]
    row = lax.broadcasted_iota(jnp.int32, (SUBLANES, 1), 0)
    y = b + tile * w[nh:nh + 1, :]
    for j in range(nh):
        s = nh - j
        tap = pltpu.roll(tile, s, axis=0)
        for i, hist in enumerate(hists):
            for t in range(min(s, valid)):
                tap = jnp.where(row == i * valid + t, hist[nh - s + t:nh - s + t + 1, :], tap)
        y = y + tap * w[j:j + 1, :]
    return y


def _last_rows(x, valid, n):
    return pltpu.roll(x, SUBLANES - (valid - n), axis=0)[0:n]


def _pack(parts, valid):
    row = lax.broadcasted_iota(jnp.int32, (SUBLANES, 1), 0)
    out = parts[0]
    for i in range(1, len(parts)):
        out = jnp.where(row < i * valid, out, pltpu.roll(parts[i], i * valid, axis=0))
    return out


def _rms_kernel(x_ref, g_ref, o_ref):
    o_ref[...] = (_rms(x_ref[...]) * g_ref[...]).astype(o_ref.dtype)


def rms_cast(x, g, tile=1024):
    r, d = x.shape
    tile = min(tile, r)
    return pl.pallas_call(
        _rms_kernel,
        grid=(r // tile,),
        in_specs=[pl.BlockSpec((tile, d), lambda i: (i, 0)),
                  pl.BlockSpec((1, d), lambda i: (0, 0))],
        out_specs=pl.BlockSpec((tile, d), lambda i: (i, 0)),
        out_shape=jax.ShapeDtypeStruct((r, d), BF16),
        compiler_params=_cparams("parallel"),
        name="rms_cast",
    )(x, g.reshape(1, d))


def _mm_kernel(a_ref, w_ref, o_ref):
    o_ref[...] = jnp.dot(a_ref[...].astype(BF16), w_ref[...].astype(BF16),
                         preferred_element_type=F32)


def matmul(a, w, bm, bn, col0=0, n=None):
    r, k = a.shape
    n = w.shape[1] if n is None else n
    bm = min(bm, r)
    j0 = col0 // bn
    return pl.pallas_call(
        _mm_kernel,
        grid=(n // bn, r // bm),
        in_specs=[pl.BlockSpec((bm, k), lambda j, i: (i, 0)),
                  pl.BlockSpec((k, bn), lambda j, i: (0, j0 + j))],
        out_specs=pl.BlockSpec((bm, bn), lambda j, i: (i, j)),
        out_shape=jax.ShapeDtypeStruct((r, n), F32),
        compiler_params=_cparams("parallel", "arbitrary"),
        name="matmul",
    )(a, w)


def _log_sigmoid(x):
    return jnp.minimum(x, 0.0) - jnp.log1p(jnp.exp(-jnp.abs(x)))


def _layer_norm_rows(x):
    xc = x - jnp.mean(x, axis=-1, keepdims=True)
    return xc * lax.rsqrt(jnp.mean(xc * xc, axis=-1, keepdims=True) + EPS)


def _branch_a_kernel(*refs, nsub, tq, valid, nchunk, zero_state):
    n_in = 3 if zero_state else 6
    xm_ref, z_ref, st_ref = refs[:3]
    (cw_ref, cb_ref, wq_ref, wk_ref, wv_ref, wif_ref, bif_ref, nh_ref, sk_ref,
     ya_ref, c_ref, n_ref, m_ref) = refs[n_in:n_in + 13]
    lc = CHUNK
    packed = valid < tq
    carried = nchunk > 1
    chunk = pl.program_id(1)
    cs_ref, ext_ref = (refs[n_in + 13], None) if packed else (None, refs[n_in + 13])

    if carried:
        @pl.when(chunk == 0)
        def _init():
            ext_ref[...] = st_ref[...]
            if zero_state:
                c_ref[...] = jnp.zeros(c_ref.shape, F32)
                n_ref[...] = jnp.zeros(n_ref.shape, F32)
                m_ref[...] = jnp.zeros(m_ref.shape, F32)
            else:
                c_ref[...] = refs[3][...]
                n_ref[...] = refs[4][...]
                m_ref[...] = refs[5][...]
        cin_ref, nin_ref, min_ref = c_ref, n_ref, m_ref
    else:
        cin_ref, nin_ref, min_ref = refs[3:6]

    lane = lax.broadcasted_iota(jnp.int32, (1, GATE_LANES), 1)
    r = lax.broadcasted_iota(jnp.int32, (tq, lc), 0)
    s_idx = lax.broadcasted_iota(jnp.int32, (tq, lc), 1)
    tril = s_idx <= r
    eye = s_idx == r
    row_valid = lax.broadcasted_iota(jnp.int32, (tq, 1), 0) < valid
    lane_valid = lax.broadcasted_iota(jnp.int32, (1, lc), 1) < valid
    scale = DH_A ** -0.5
    cw = cw_ref[...]

    def blockdiag(xb, w_ref):
        nblk = D_MODEL // BD_BLOCK
        return jnp.concatenate(
            [jnp.dot(xb[:, j * BD_BLOCK:(j + 1) * BD_BLOCK], w_ref[j], preferred_element_type=F32)
             for j in range(nblk)], axis=1)

    def pad_rows(x):
        if tq == lc:
            return x
        return jnp.concatenate([x, jnp.zeros((lc - tq, x.shape[1]), x.dtype)], axis=0)

    def front(x_in, y):
        xconv = y * jax.nn.sigmoid(y)
        xcb = xconv.astype(BF16)
        q32 = blockdiag(xcb, wq_ref)
        k32 = blockdiag(xcb, wk_ref)
        v32 = blockdiag(x_in.astype(BF16), wv_ref)
        if_pre = (jnp.dot(q32.astype(BF16), wif_ref[0:D_MODEL, :], preferred_element_type=F32)
                  + jnp.dot(k32.astype(BF16), wif_ref[D_MODEL:2 * D_MODEL, :], preferred_element_type=F32)
                  + jnp.dot(v32.astype(BF16), wif_ref[2 * D_MODEL:3 * D_MODEL, :], preferred_element_type=F32)
                  + bif_ref[...])
        return xconv, q32, k32, v32, if_pre, _log_sigmoid(if_pre)

    if packed:
        tile = xm_ref[0]
        hists = [st_ref[i] for i in range(nsub)]
        shared = front(tile, _packed_tile_conv(tile, hists, cw, cb_ref[...], valid))

    packed_out = []
    for i in range(nsub):
        if packed:
            x_in = _unpack(tile, i, valid)
            z_in = _unpack(z_ref[0], i, valid)
            cs_ref[i] = _last_rows(x_in, valid, CONV_A - 1)
            xconv, q32, k32, v32, if_pre, log_f_all = (_unpack(a, i, valid) for a in shared)
        else:
            x_in = xm_ref[i]
            hist = ext_ref[i] if carried else st_ref[i]
            xe = jnp.concatenate([hist, x_in], axis=0)
            y = cb_ref[...] + x_in * cw[CONV_A - 1:CONV_A, :]
            for j in range(CONV_A - 1):
                shifted = pltpu.roll(xe, CONV_A - 1 - j, axis=0)[SUBLANES:, :]
                y = y + shifted * cw[j:j + 1, :]
            if carried:
                ext_ref[i] = x_in[tq - SUBLANES:tq, :]
            xconv, q32, k32, v32, if_pre, log_f_all = front(x_in, y)
        qb = q32.astype(BF16)
        kpad = pad_rows(k32).astype(BF16)
        vpad = pad_rows(v32).astype(BF16)

        m_all = min_ref[i, 0:1, :]
        m_out = m_all
        heads = []
        for h in range(H_A):
            hs = slice(h * DH_A, (h + 1) * DH_A)
            qh, kh, vh = qb[:, hs], kpad[:, hs], vpad[:, hs]
            li_col = jnp.sum(jnp.where(lane == h, if_pre, 0.0), axis=1, keepdims=True)
            lf_col = jnp.sum(jnp.where(lane == H_A + h, log_f_all, 0.0), axis=1, keepdims=True)
            if packed:
                li_col = jnp.where(row_valid, li_col, NEG_INF)
                lf_col = jnp.where(row_valid, lf_col, 0.0)
            m_prev = jnp.sum(jnp.where(lane == h, m_all, 0.0), axis=1, keepdims=True)

            b_row = jnp.sum(jnp.where(r <= s_idx, lf_col, 0.0), axis=0, keepdims=True)
            li_row = jnp.sum(jnp.where(eye, li_col, 0.0), axis=0, keepdims=True)
            if packed:
                li_row = jnp.where(lane_valid, li_row, NEG_INF)
            b_col = jnp.sum(jnp.where(eye, b_row, 0.0), axis=1, keepdims=True)

            d = jnp.where(tril, b_col - b_row + li_row, NEG_INF)
            inter = m_prev + b_col
            m_t = jnp.maximum(jnp.max(d, axis=1, keepdims=True), inter)
            w = jnp.exp(d - m_t)
            g = jnp.exp(inter - m_t)
            qk = lax.dot_general(qh, kh, (((1,), (1,)), ((), ())), preferred_element_type=F32)
            s = qk * (w * scale)
            c_h = cin_ref[i, h]
            n_h = nin_ref[i, h:h + 1, :]
            num = (g * jnp.dot(qh, c_h.astype(BF16), preferred_element_type=F32)
                   + jnp.dot(s.astype(BF16), vh, preferred_element_type=F32))
            den = (g * jnp.sum(q32[:, hs] * n_h, axis=1, keepdims=True)
                   + jnp.sum(s, axis=1, keepdims=True))
            ht = num / jnp.maximum(jnp.abs(den), jnp.exp(-m_t))

            b_last = jnp.sum(lf_col, axis=0, keepdims=True)
            a_col = b_last - b_col + li_col
            m_new = jnp.maximum(m_prev + b_last, jnp.max(a_col, axis=0, keepdims=True))
            decay = jnp.exp(m_prev + b_last - m_new)
            kw = k32[:, hs] * (jnp.exp(a_col - m_new) * scale)
            kwt = jnp.transpose(pad_rows(kw)).astype(BF16)
            c_ref[i, h] = decay * c_h + jnp.dot(kwt, vh, preferred_element_type=F32)
            n_ref[i, h:h + 1, :] = decay * n_h + jnp.sum(kw, axis=0, keepdims=True)
            m_out = jnp.where(lane == h, m_new, m_out)

            hn = _layer_norm_rows(ht) * nh_ref[:, hs]
            z_h = z_in[:, hs] if packed else z_ref[i, :, hs]
            ya_h = jax.nn.sigmoid(z_h) * (hn + sk_ref[:, hs] * xconv[:, hs])
            if packed:
                heads.append(ya_h)
            else:
                ya_ref[i, :, hs] = ya_h.astype(ya_ref.dtype)
        m_ref[i] = jnp.broadcast_to(m_out, (SUBLANES, GATE_LANES))
        if packed:
            packed_out.append(heads)
    if packed:
        for h in range(H_A):
            tile = _pack([heads[h] for heads in packed_out], valid)
            ya_ref[0, :, h * DH_A:(h + 1) * DH_A] = tile.astype(ya_ref.dtype)


def branch_a(proj, states, p, *, nseq, seq, out_dtype):
    r = proj.shape[0]
    zero_state = states is None
    if seq >= CHUNK:
        nsub, tq, valid, nchunk = 2, CHUNK, CHUNK, seq // CHUNK
        proj3 = proj.reshape(nseq, seq, N_IN)
        row = lambda col: pl.BlockSpec((nsub, tq, D_MODEL), lambda g, c, col=col: (g, c, col))
        ya_shape = (nseq, seq, D_MODEL)
        assert zero_state
        hist = jnp.zeros((nseq, SUBLANES, D_MODEL), F32)
    else:
        nsub, tq, valid, nchunk = SUBLANES // seq, SUBLANES, seq, 1
        proj3 = proj.reshape(r // SUBLANES, SUBLANES, N_IN)
        row = lambda col: pl.BlockSpec((1, tq, D_MODEL), lambda g, c, col=col: (g, 0, col))
        ya_shape = (r // SUBLANES, SUBLANES, D_MODEL)
        assert not zero_state
        hist = states[0]
    packed = valid < tq
    full = lambda shape: pl.BlockSpec(shape, lambda g, c: (0,) * len(shape))
    c_spec = pl.BlockSpec((nsub, H_A, DH_A, DH_A), lambda g, c: (g, 0, 0, 0))
    n_spec = pl.BlockSpec((nsub, H_A, DH_A), lambda g, c: (g, 0, 0))
    m_spec = pl.BlockSpec((nsub, SUBLANES, GATE_LANES), lambda g, c: (g, 0, 0))
    hist_spec = pl.BlockSpec((nsub,) + hist.shape[1:], lambda g, c: (g, 0, 0))
    in_specs = [row(0), row(1), hist_spec]
    args = [proj3, proj3, hist]
    if not zero_state:
        in_specs += [c_spec, n_spec, m_spec]
        args += list(states[1:])
    nblk = D_MODEL // BD_BLOCK
    in_specs += [full((CONV_A, D_MODEL)), full((1, D_MODEL)),
                 full((nblk, BD_BLOCK, BD_BLOCK)), full((nblk, BD_BLOCK, BD_BLOCK)),
                 full((nblk, BD_BLOCK, BD_BLOCK)),
                 full((3 * D_MODEL, GATE_LANES)), full((1, GATE_LANES)),
                 full((1, D_MODEL)), full((1, D_MODEL))]
    args += [p["conv_a_w"], p["conv_a_b"], p["wq_bd"], p["wk_bd"], p["wv_bd"], p["w_if"], p["b_if"],
             p["norm_h_a"], p["skip_a"]]
    kern = functools.partial(_branch_a_kernel, nsub=nsub, tq=tq, valid=valid, nchunk=nchunk,
                             zero_state=zero_state)
    out_specs = [row(0), c_spec, n_spec, m_spec]
    out_shape = [jax.ShapeDtypeStruct(ya_shape, out_dtype),
                 jax.ShapeDtypeStruct((nseq, H_A, DH_A, DH_A), F32),
                 jax.ShapeDtypeStruct((nseq, H_A, DH_A), F32),
                 jax.ShapeDtypeStruct((nseq, SUBLANES, GATE_LANES), F32)]
    scratch = []
    if packed:
        out_specs.append(hist_spec)
        out_shape.append(jax.ShapeDtypeStruct(hist.shape, F32))
    else:
        scratch.append(pltpu.VMEM((nsub, SUBLANES, D_MODEL), F32))
    outs = pl.pallas_call(
        kern,
        grid=(nseq // nsub, nchunk),
        in_specs=in_specs,
        out_specs=out_specs,
        out_shape=out_shape,
        scratch_shapes=scratch,
        compiler_params=_cparams("parallel", "arbitrary"),
        name="branch_a",
    )(*args)
    ya, c_new, n_new, m_new = outs[:4]
    hist_new = outs[4] if packed else None
    return ya.reshape(r, D_MODEL), c_new, n_new, m_new[:, 0, :H_A], hist_new


def _gmlp_kernel(u_ref, v_ref, nv_ref, wmix_ref, bmix_ref, yb_ref, *vn_ref):
    bias = bmix_ref[...]
    lane = lax.broadcasted_iota(jnp.int32, (1, GATE_LANES), 1)
    b_cols = [jnp.sum(jnp.where(lane == g, bias, 0.0), axis=1, keepdims=True) for g in range(G_B)]
    for c in range(u_ref.shape[0] // CHUNK):
        rows = slice(c * CHUNK, (c + 1) * CHUNK)
        vn = _layer_norm_rows(jax.nn.gelu(v_ref[rows, :])) * nv_ref[...]
        if vn_ref:
            vn_ref[0][rows, :] = vn
        vnb = vn.astype(BF16)
        mixed = [jnp.dot(wmix_ref[g], vnb[:, g * DG_B:(g + 1) * DG_B],
                         preferred_element_type=F32) + b_cols[g] for g in range(G_B)]
        yb_ref[rows, :] = (jax.nn.gelu(u_ref[rows, :])
                           * jnp.concatenate(mixed, axis=1)).astype(yb_ref.dtype)


def gmlp(proj, norm_v, w_s, b_s, seq, *, emit_vn, tile=512):
    r = proj.shape[0]
    lc = min(seq, CHUNK)
    per = CHUNK // lc
    tril = jnp.tril(jnp.ones((lc, lc), bool))
    w_small = jnp.where(tril[None], w_s[:, :lc, :lc], 0.0)
    wmix = jnp.einsum("ab,gts->gatbs", jnp.eye(per, dtype=F32), w_small)
    wmix = wmix.reshape(G_B, CHUNK, CHUNK).astype(BF16)
    bmix = jnp.pad(jnp.tile(b_s[:, :lc].T, (per, 1)), ((0, 0), (0, GATE_LANES - G_B)))
    u_col = 2 * D_MODEL // D_B
    row = pl.BlockSpec((tile, D_B), lambda i: (i, 0))
    outs = pl.pallas_call(
        _gmlp_kernel,
        grid=(r // tile,),
        in_specs=[pl.BlockSpec((tile, D_B), lambda i: (i, u_col)),
                  pl.BlockSpec((tile, D_B), lambda i: (i, u_col + 1)),
                  pl.BlockSpec((1, D_B), lambda i: (0, 0)),
                  pl.BlockSpec((G_B, CHUNK, CHUNK), lambda i: (0, 0, 0)),
                  pl.BlockSpec((CHUNK, GATE_LANES), lambda i: (0, 0))],
        out_specs=[row, row] if emit_vn else [row],
        out_shape=[jax.ShapeDtypeStruct((r, D_B), BF16)]
                  + ([jax.ShapeDtypeStruct((r, D_B), F32)] if emit_vn else []),
        compiler_params=_cparams("parallel"),
        name="gmlp",
    )(proj, proj, norm_v.reshape(1, D_B), wmix, bmix)
    return (outs[0], outs[1]) if emit_vn else (outs[0], None)


def _attend(q, k, v):
    outs = []
    for h in range(H_C):
        hs = slice(h * DH_C, (h + 1) * DH_C)
        s = lax.dot_general(q[:, hs], k[:, hs], (((1,), (1,)), ((), ())),
                            preferred_element_type=F32) * (DH_C ** -0.5)
        e = jnp.exp(s - jnp.max(s, axis=-1, keepdims=True))
        prob = e / jnp.sum(e, axis=-1, keepdims=True)
        outs.append(jnp.dot(prob.astype(BF16), v[:, hs], preferred_element_type=F32))
    return jnp.concatenate(outs, axis=1)


def _attend_flat(q, kf, vf):
    qs = jnp.concatenate([q[:, h * DH_C:(h + 1) * DH_C] for h in range(H_C)], axis=0).astype(BF16)
    s = lax.dot_general(qs, kf, (((1,), (1,)), ((), ())), preferred_element_type=F32) * (DH_C ** -0.5)
    shape = (H_C * SUBLANES, MEM_LEN * H_C)
    row_head = lax.broadcasted_iota(jnp.int32, shape, 0) // SUBLANES
    col_head = lax.broadcasted_iota(jnp.int32, shape, 1) % H_C
    s = jnp.where(row_head == col_head, s, NEG_INF)
    e = jnp.exp(s - jnp.max(s, axis=-1, keepdims=True))
    prob = e / jnp.sum(e, axis=-1, keepdims=True)
    o = jnp.dot(prob.astype(BF16), vf, preferred_element_type=F32)
    return jnp.concatenate([o[h * SUBLANES:(h + 1) * SUBLANES] for h in range(H_C)], axis=1)


def _xattn_kernel(q_ref, k_ref, v_ref, o_ref, *, nseq, rq, valid):
    per = rq // valid
    for t in range(nseq // per):
        tile = q_ref[t * rq:(t + 1) * rq, :]
        parts = []
        for j in range(per):
            i = t * per + j
            if per == 1:
                mem = slice(i * MEM_LEN, (i + 1) * MEM_LEN)
                parts.append(_attend(tile.astype(BF16), k_ref[mem, :].astype(BF16),
                                     v_ref[mem, :].astype(BF16)))
            else:
                kf = k_ref[i].reshape(MEM_LEN * H_C, DH_C).astype(BF16)
                vf = v_ref[i].reshape(MEM_LEN * H_C, DH_C).astype(BF16)
                parts.append(_attend_flat(_unpack(tile, j, valid), kf, vf))
        o = parts[0] if per == 1 else _pack(parts, valid)
        o_ref[t * rq:(t + 1) * rq, :] = o.astype(o_ref.dtype)


def xattn(proj, k_arr, v_arr, k_col, v_col, *, nbatch, nseq, rq, out_dtype):
    r = proj.shape[0]
    seq = r // nbatch
    valid = min(seq, rq)
    rows = nseq * valid
    nq = seq // valid
    assert nseq == 1 or nq == 1
    q_col = (2 * D_MODEL + 2 * D_B) // D_C

    def mem_spec(arr, col):
        if arr.ndim == 4:
            return pl.BlockSpec((nseq, MEM_LEN, H_C, DH_C), lambda g, t: (g, 0, 0, 0))
        return pl.BlockSpec((nseq * MEM_LEN, D_C), lambda g, t: (g, col))

    kern = functools.partial(_xattn_kernel, nseq=nseq, rq=rq, valid=valid)
    return pl.pallas_call(
        kern,
        grid=(nbatch // nseq, nq),
        in_specs=[pl.BlockSpec((rows, D_C), lambda g, t: (g * nq + t, q_col)),
                  mem_spec(k_arr, k_col), mem_spec(v_arr, v_col)],
        out_specs=pl.BlockSpec((rows, D_C), lambda g, t: (g * nq + t, 0)),
        out_shape=jax.ShapeDtypeStruct((r, D_C), out_dtype),
        compiler_params=_cparams("parallel", "arbitrary"),
        name="xattn",
    )(proj, k_arr, v_arr)


def _mix_out_kernel(ya_ref, yb_ref, yc_ref, ga0, ga1, gb0, gb1, gc0, gc1, x_ref,
                    wa_ref, wb_ref, wc_ref, wo_ref, g1_ref, g2_ref, x1_ref, h2_ref):
    def branch(y_ref, w_ref, lo, hi):
        gate = jnp.concatenate([lo[...], hi[...]], axis=1)
        return jax.nn.sigmoid(gate) * jnp.dot(y_ref[...].astype(BF16), w_ref[...],
                                              preferred_element_type=F32)

    merged = (branch(ya_ref, wa_ref, ga0, ga1) + branch(yb_ref, wb_ref, gb0, gb1)
              + branch(yc_ref, wc_ref, gc0, gc1))
    o = jnp.dot(merged.astype(BF16), wo_ref[...], preferred_element_type=F32)
    x1 = x_ref[...] + _rms(o) * g1_ref[...]
    x1_ref[...] = x1
    h2_ref[...] = (_rms(x1) * g2_ref[...]).astype(h2_ref.dtype)


def mix_out(ya, yb, yc, proj, x, w_a, w_b, w_c, w_o, g_post, g_pre, bm=256):
    r = x.shape[0]
    half = D_MODEL // 2
    g0 = GATE_COL0 // half
    gate = lambda k: pl.BlockSpec((bm, half), lambda i, k=k: (i, g0 + k))
    rowspec = lambda d: pl.BlockSpec((bm, d), lambda i: (i, 0))
    vec = pl.BlockSpec((1, D_MODEL), lambda i: (0, 0))
    return pl.pallas_call(
        _mix_out_kernel,
        grid=(r // bm,),
        in_specs=[rowspec(D_MODEL), rowspec(D_B), rowspec(D_C)] + [gate(k) for k in range(6)]
                 + [rowspec(D_MODEL), _resident((D_MODEL, D_MODEL)), _resident((D_B, D_MODEL)),
                    _resident((D_C, D_MODEL)), _resident((D_MODEL, D_MODEL)), vec, vec],
        out_specs=[rowspec(D_MODEL), rowspec(D_MODEL)],
        out_shape=[jax.ShapeDtypeStruct((r, D_MODEL), F32), jax.ShapeDtypeStruct((r, D_MODEL), BF16)],
        compiler_params=_cparams("parallel"),
        name="mix_out",
    )(ya, yb, yc, proj, proj, proj, proj, proj, proj, x, w_a, w_b, w_c, w_o,
      g_post.reshape(1, D_MODEL), g_pre.reshape(1, D_MODEL))


def _gated_act(gate, val):
    return jax.nn.gelu(gate) * val


def _ffn_up_act_kernel(a_ref, wg_ref, wv_ref, cwg_ref, cwv_ref, cbg_ref, cbv_ref,
                       o_ref, sg_ref, sv_ref, ext_ref, *, bm):
    @pl.when(pl.program_id(2) == 0)
    def _():
        ext_ref[:, 0:SUBLANES, :] = jnp.zeros((2, SUBLANES, ext_ref.shape[2]), F32)

    a = a_ref[...]

    def conv(which, w_ref, cw_ref, cb_ref, st_ref):
        u = jnp.dot(a, w_ref[...].astype(BF16), preferred_element_type=F32)
        ext_ref[which, SUBLANES:SUBLANES + bm, :] = u
        cw = cw_ref[...]
        y = cb_ref[...] + u * cw[CONV_F - 1:CONV_F, :]
        for j in range(CONV_F - 1):
            y = y + ext_ref[which, pl.ds(SUBLANES - (CONV_F - 1) + j, bm), :] * cw[j:j + 1, :]
        tail = u[bm - SUBLANES:bm, :]
        ext_ref[which, 0:SUBLANES, :] = tail
        st_ref[0] = tail
        return y

    gate = conv(0, wg_ref, cwg_ref, cbg_ref, sg_ref)
    val = conv(1, wv_ref, cwv_ref, cbv_ref, sv_ref)
    o_ref[...] = _gated_act(gate, val).astype(o_ref.dtype)


def ffn_up_act(h2, w_up, conv_w, conv_b, *, nseq, bm=1024, bn=512):
    r = h2.shape[0]
    tiles = r // nseq // bm
    ncol = D_FF // bn
    wspec = lambda off: pl.BlockSpec((D_MODEL, bn), lambda j, b, t, off=off: (0, off + j))
    cwspec = lambda off: pl.BlockSpec((CONV_F, bn), lambda j, b, t, off=off: (0, off + j))
    cbspec = lambda off: pl.BlockSpec((1, bn), lambda j, b, t, off=off: (0, off + j))
    st_spec = pl.BlockSpec((1, SUBLANES, bn), lambda j, b, t: (b, 0, j))
    kern = functools.partial(_ffn_up_act_kernel, bm=bm)
    return pl.pallas_call(
        kern,
        grid=(ncol, nseq, tiles),
        in_specs=[pl.BlockSpec((bm, D_MODEL), lambda j, b, t: (b * tiles + t, 0)),
                  wspec(0), wspec(ncol), cwspec(0), cwspec(ncol), cbspec(0), cbspec(ncol)],
        out_specs=[pl.BlockSpec((bm, bn), lambda j, b, t: (b * tiles + t, j)), st_spec, st_spec],
        out_shape=[jax.ShapeDtypeStruct((r, D_FF), BF16),
                   jax.ShapeDtypeStruct((nseq, SUBLANES, D_FF), F32),
                   jax.ShapeDtypeStruct((nseq, SUBLANES, D_FF), F32)],
        scratch_shapes=[pltpu.VMEM((2, SUBLANES + bm, bn), F32)],
        compiler_params=_cparams("parallel", "arbitrary", "arbitrary"),
        name="ffn_up_act",
    )(h2, w_up, w_up, conv_w, conv_w, conv_b, conv_b)


def _ffn_act_kernel(g_ref, v_ref, sg_ref, sv_ref, wg_ref, wv_ref, bg_ref, bv_ref,
                    o_ref, ng_ref, nv_ref, *, nseq, valid):
    per = SUBLANES // valid
    wg, wv, bg, bv = wg_ref[...], wv_ref[...], bg_ref[...], bv_ref[...]
    for t in range(nseq // per):
        rows = slice(t * SUBLANES, (t + 1) * SUBLANES)
        seqs = range(t * per, (t + 1) * per)
        g_tile, v_tile = g_ref[rows, :], v_ref[rows, :]
        gate = _packed_tile_conv(g_tile, [sg_ref[i] for i in seqs], wg, bg, valid)
        val = _packed_tile_conv(v_tile, [sv_ref[i] for i in seqs], wv, bv, valid)
        o_ref[rows, :] = _gated_act(gate, val).astype(o_ref.dtype)
        for j, i in enumerate(seqs):
            ng_ref[i] = _last_rows(_unpack(g_tile, j, valid), valid, CONV_F - 1)
            nv_ref[i] = _last_rows(_unpack(v_tile, j, valid), valid, CONV_F - 1)


def ffn_act(up, hist, conv_w, conv_b, *, nbatch, nseq, bc=2048):
    r = up.shape[0]
    valid = r // nbatch
    ncol = D_FF // bc
    nh = CONV_F - 1
    x_spec = lambda off: pl.BlockSpec((nseq * valid, bc), lambda g, j, off=off: (g, off + j))
    s_spec = lambda off: pl.BlockSpec((nseq, nh, bc), lambda g, j, off=off: (g, 0, off + j))
    w_spec = lambda off: pl.BlockSpec((CONV_F, bc), lambda g, j, off=off: (0, off + j))
    b_spec = lambda off: pl.BlockSpec((1, bc), lambda g, j, off=off: (0, off + j))
    kern = functools.partial(_ffn_act_kernel, nseq=nseq, valid=valid)
    return pl.pallas_call(
        kern,
        grid=(nbatch // nseq, ncol),
        in_specs=[x_spec(0), x_spec(ncol), s_spec(0), s_spec(ncol), w_spec(0), w_spec(ncol),
                  b_spec(0), b_spec(ncol)],
        out_specs=[pl.BlockSpec((nseq * valid, bc), lambda g, j: (g, j)), s_spec(0), s_spec(0)],
        out_shape=[jax.ShapeDtypeStruct((r, D_FF), F32),
                   jax.ShapeDtypeStruct((nbatch, nh, D_FF), F32),
                   jax.ShapeDtypeStruct((nbatch, nh, D_FF), F32)],
        compiler_params=_cparams("parallel", "parallel"),
        name="ffn_act",
    )(up, up, hist, hist, conv_w, conv_w, conv_b, conv_b)


def _ffn_down_kernel(a_ref, w_ref, x_ref, g_ref, y_ref):
    f = jnp.dot(a_ref[...].astype(BF16), w_ref[...], preferred_element_type=F32)
    y_ref[...] = x_ref[...] + _rms(f) * g_ref[...]


def ffn_down_post(act, w_down, x1, g, bm=256):
    r = x1.shape[0]
    return pl.pallas_call(
        _ffn_down_kernel,
        grid=(r // bm,),
        in_specs=[pl.BlockSpec((bm, D_FF), lambda i: (i, 0)), _resident((D_FF, D_MODEL)),
                  pl.BlockSpec((bm, D_MODEL), lambda i: (i, 0)),
                  pl.BlockSpec((1, D_MODEL), lambda i: (0, 0))],
        out_specs=pl.BlockSpec((bm, D_MODEL), lambda i: (i, 0)),
        out_shape=jax.ShapeDtypeStruct((r, D_MODEL), F32),
        compiler_params=_cparams("parallel"),
        name="ffn_down_post",
    )(act, w_down, x1, g.reshape(1, D_MODEL))


def _blockdiag_tiles(w):
    nblk = D_MODEL // BD_BLOCK
    rows = w.reshape(nblk, BD_BLOCK, QKV_BLOCK)
    tiled = jnp.tile(rows, (1, 1, BD_BLOCK // QKV_BLOCK))
    idx = jnp.arange(BD_BLOCK) // QKV_BLOCK
    same_block = idx[:, None] == idx[None, :]
    return jnp.where(same_block[None], tiled, 0.0).astype(BF16)


def _prep_params(conv_a_w, conv_a_b, w_q_a, w_k_a, w_v_a, w_if_a, b_if_a, norm_h_a, skip_a):
    pad = GATE_LANES - 2 * H_A
    return {
        "conv_a_w": conv_a_w,
        "conv_a_b": conv_a_b.reshape(1, D_MODEL),
        "wq_bd": _blockdiag_tiles(w_q_a),
        "wk_bd": _blockdiag_tiles(w_k_a),
        "wv_bd": _blockdiag_tiles(w_v_a),
        "w_if": jnp.pad(w_if_a, ((0, 0), (0, pad))).astype(BF16),
        "b_if": jnp.pad(b_if_a, (0, pad)).reshape(1, GATE_LANES),
        "norm_h_a": norm_h_a.reshape(1, D_MODEL),
        "skip_a": skip_a.reshape(1, D_MODEL),
    }


def _trunk(x, mem, states, w, pa, *, nseq, seq):
    prompt = states is None
    k_arr, v_arr, k_col, v_col = mem

    h = rms_cast(x, w["norm_mix_pre"])
    proj = matmul(h, w["w_in"], 1024, 1024)

    yb, vn = gmlp(proj, w["norm_v_b"], w["w_s_b"], w["b_s_b"], seq, emit_vn=not prompt)
    ya, c_new, n_new, m_new, conv_new = branch_a(proj, None if prompt else states[:4], pa,
                                                 nseq=nseq, seq=seq,
                                                 out_dtype=BF16 if prompt else F32)
    if prompt:
        yc = xattn(proj, k_arr, v_arr, k_col, v_col, nbatch=nseq, nseq=1, rq=1024, out_dtype=BF16)
        conv_new = proj.reshape(nseq, seq, N_IN)[:, seq - (CONV_A - 1):, :D_MODEL]
    else:
        yc = xattn(proj, k_arr, v_arr, k_col, v_col, nbatch=nseq, nseq=8, rq=SUBLANES, out_dtype=F32)

    x1, h2 = mix_out(ya, yb, yc, proj, x, w["w_br_a"], w["w_br_b"], w["w_br_c"], w["w_out"],
                     w["norm_mix_post"], w["norm_ffn_pre"])

    conv_b = w["conv_ffn_b"].reshape(1, 2 * D_FF)
    if prompt:
        act, tail_g, tail_v = ffn_up_act(h2, w["w_ffn_up"], w["conv_ffn_w"], conv_b, nseq=nseq)
        ffn_new = jnp.concatenate([tail_g, tail_v], axis=-1)[:, SUBLANES - (CONV_F - 1):]
    else:
        up = matmul(h2, w["w_ffn_up"], 1024, 1024)
        act, new_g, new_v = ffn_act(up, states[4], w["conv_ffn_w"], conv_b, nbatch=nseq, nseq=32)
        ffn_new = jnp.concatenate([new_g, new_v], axis=-1)
    y = ffn_down_post(act, w["w_ffn_down"], x1, w["norm_ffn_post"])
    return y, conv_new, ffn_new, vn, c_new, n_new, m_new


def kernel(x_prompt, x_sample, state_conv_a, state_mlstm_c, state_mlstm_n, state_mlstm_m, cache_mem_k, cache_mem_v, state_ffn_conv, mem_prompt, norm_mix_pre, w_in, conv_a_w, conv_a_b, w_q_a, w_k_a, w_v_a, w_if_a, b_if_a, norm_h_a, skip_a, w_br_a, norm_v_b, w_s_b, b_s_b, w_br_b, norm_mem, w_mem_kv, w_br_c, w_out, norm_mix_post, norm_ffn_pre, w_ffn_up, conv_ffn_w, conv_ffn_b, w_ffn_down, norm_ffn_post):
    nb, seq, _ = x_prompt.shape
    db, dseq, _ = x_sample.shape
    w = {
        "norm_mix_pre": norm_mix_pre[0], "w_in": w_in[0], "norm_v_b": norm_v_b[0],
        "w_s_b": w_s_b[0], "b_s_b": b_s_b[0],
        "w_br_a": w_br_a[0].astype(BF16), "w_br_b": w_br_b[0].astype(BF16),
        "w_br_c": w_br_c[0].astype(BF16), "w_out": w_out[0].astype(BF16),
        "norm_mix_post": norm_mix_post[0], "norm_ffn_pre": norm_ffn_pre[0], "w_ffn_up": w_ffn_up[0],
        "conv_ffn_w": conv_ffn_w[0], "conv_ffn_b": conv_ffn_b[0],
        "w_ffn_down": w_ffn_down[0].astype(BF16), "norm_ffn_post": norm_ffn_post[0],
    }
    pa = _prep_params(conv_a_w[0], conv_a_b[0], w_q_a[0], w_k_a[0], w_v_a[0], w_if_a[0], b_if_a[0],
                      norm_h_a[0], skip_a[0])

    mem_h = rms_cast(mem_prompt.reshape(nb * MEM_LEN, D_MODEL), norm_mem[0])
    mem_k = matmul(mem_h, w_mem_kv[0], 1024, 1024, 0, D_C)
    mem_v = matmul(mem_h, w_mem_kv[0], 1024, 1024, D_C, D_C)
    yp, conv_p, ffn_p, _, c_p, n_p, m_p = _trunk(
        x_prompt.reshape(nb * seq, D_MODEL), (mem_k, mem_v, 0, 0), None, w, pa, nseq=nb, seq=seq)

    m0 = jnp.pad(state_mlstm_m[0][:, None, :], ((0, 0), (0, SUBLANES - 1), (0, GATE_LANES - H_A)))
    ys, conv_s, ffn_s, vn_s, c_s, n_s, m_s = _trunk(
        x_sample.reshape(db * dseq, D_MODEL), (cache_mem_k[0], cache_mem_v[0], 0, 0),
        (state_conv_a[0], state_mlstm_c[0], state_mlstm_n[0], m0, state_ffn_conv[0]),
        w, pa, nseq=db, seq=dseq)

    return (
        yp.reshape(nb, seq, D_MODEL), ys.reshape(db, dseq, D_MODEL),
        conv_p[None], c_p[None], n_p[None], m_p[None],
        mem_k.reshape(1, nb, MEM_LEN, H_C, DH_C), mem_v.reshape(1, nb, MEM_LEN, H_C, DH_C),
        ffn_p[None],
        conv_s[None], c_s[None], n_s[None], m_s[None],
        vn_s.reshape(1, db, dseq, D_B), ffn_s[None],
    )
```

```python
import functools

import jax
import jax.numpy as jnp
from jax import lax
from jax.experimental import pallas as pl
from jax.experimental.pallas import tpu as pltpu

F32 = jnp.float32
BF16 = jnp.bfloat16

D_MODEL = 2048
H_A = 4
DH_A = 512
QKV_BLOCK = 4
CONV_A = 4
CONV_F = 3
D_B = 1024
G_B = 4
DG_B = 256
H_C = 4
DH_C = 256
D_C = 1024
MEM_LEN = 256
D_FF = 6144
N_IN = 13312
GATE_COL0 = 2 * D_MODEL + 2 * D_B + D_C
EPS = 1e-6
NEG_INF = -1e30

CHUNK = 128
BD_BLOCK = 256
GATE_LANES = 128
SUBLANES = 8
V7X_VMEM_BYTES = 64 * 1024 * 1024
VMEM_LIMIT = V7X_VMEM_BYTES - 8 * 1024 * 1024


def _cparams(*sem):
    return pltpu.CompilerParams(dimension_semantics=sem, vmem_limit_bytes=VMEM_LIMIT)


def _resident(shape):
    return pl.BlockSpec(shape, lambda *_: (0,) * len(shape), pipeline_mode=pl.Buffered(1))


def _rms(x):
    return x * lax.rsqrt(jnp.mean(x * x, axis=-1, keepdims=True) + EPS)


def _unpack(tile, i, valid):
    return tile if i == 0 else pltpu.roll(tile, SUBLANES - i * valid, axis=0)


def _packed_tile_conv(tile, hists, w, b, valid):
    nh = hists[0].shape[0]
    row = lax.broadcasted_iota(jnp.int32, (SUBLANES, 1), 0)
    y = b + tile * w[nh:nh + 1, :]
    for j in range(nh):
        s = nh - j
        tap = pltpu.roll(tile, s, axis=0)
        for i, hist in enumerate(hists):
            for t in range(min(s, valid)):
                tap = jnp.where(row == i * valid + t, hist[nh - s + t:nh - s + t + 1, :], tap)
        y = y + tap * w[j:j + 1, :]
    return y


def _last_rows(x, valid, n):
    return pltpu.roll(x, SUBLANES - (valid - n), axis=0)[0:n]


def _pack(parts, valid):
    row = lax.broadcasted_iota(jnp.int32, (SUBLANES, 1), 0)
    out = parts[0]
    for i in range(1, len(parts)):
        out = jnp.where(row < i * valid, out, pltpu.roll(parts[i], i * valid, axis=0))
    return out


def _rms_kernel(x_ref, g_ref, o_ref):
    o_ref[...] = (_rms(x_ref[...]) * g_ref[...]).astype(o_ref.dtype)


def rms_cast(x, g, tile=1024):
    r, d = x.shape
    tile = min(tile, r)
    return pl.pallas_call(
        _rms_kernel,
        grid=(r // tile,),
        in_specs=[pl.BlockSpec((tile, d), lambda i: (i, 0)),
                  pl.BlockSpec((1, d), lambda i: (0, 0))],
        out_specs=pl.BlockSpec((tile, d), lambda i: (i, 0)),
        out_shape=jax.ShapeDtypeStruct((r, d), BF16),
        compiler_params=_cparams("parallel"),
        name="rms_cast",
    )(x, g.reshape(1, d))


def _mm_kernel(a_ref, w_ref, o_ref):
    o_ref[...] = jnp.dot(a_ref[...].astype(BF16), w_ref[...].astype(BF16),
                         preferred_element_type=F32)


def matmul(a, w, bm, bn, col0=0, n=None):
    r, k = a.shape
    n = w.shape[1] if n is None else n
    bm = min(bm, r)
    j0 = col0 // bn
    return pl.pallas_call(
        _mm_kernel,
        grid=(n // bn, r // bm),
        in_specs=[pl.BlockSpec((bm, k), lambda j, i: (i, 0)),
                  pl.BlockSpec((k, bn), lambda j, i: (0, j0 + j))],
        out_specs=pl.BlockSpec((bm, bn), lambda j, i: (i, j)),
        out_shape=jax.ShapeDtypeStruct((r, n), F32),
        compiler_params=_cparams("parallel", "arbitrary"),
        name="matmul",
    )(a, w)


def _log_sigmoid(x):
    return jnp.minimum(x, 0.0) - jnp.log1p(jnp.exp(-jnp.abs(x)))


def _layer_norm_rows(x):
    xc = x - jnp.mean(x, axis=-1, keepdims=True)
    return xc * lax.rsqrt(jnp.mean(xc * xc, axis=-1, keepdims=True) + EPS)


def _branch_a_kernel(*refs, nsub, tq, valid, nchunk, zero_state):
    n_in = 3 if zero_state else 6
    xm_ref, z_ref, st_ref = refs[:3]
    (cw_ref, cb_ref, wq_ref, wk_ref, wv_ref, wif_ref, bif_ref, nh_ref, sk_ref,
     ya_ref, c_ref, n_ref, m_ref) = refs[n_in:n_in + 13]
    lc = CHUNK
    packed = valid < tq
    carried = nchunk > 1
    chunk = pl.program_id(1)
    cs_ref, ext_ref = (refs[n_in + 13], None) if packed else (None, refs[n_in + 13])

    if carried:
        @pl.when(chunk == 0)
        def _init():
            ext_ref[...] = st_ref[...]
            if zero_state:
                c_ref[...] = jnp.zeros(c_ref.shape, F32)
                n_ref[...] = jnp.zeros(n_ref.shape, F32)
                m_ref[...] = jnp.zeros(m_ref.shape, F32)
            else:
                c_ref[...] = refs[3][...]
                n_ref[...] = refs[4][...]
                m_ref[...] = refs[5][...]
        cin_ref, nin_ref, min_ref = c_ref, n_ref, m_ref
    else:
        cin_ref, nin_ref, min_ref = refs[3:6]

    lane = lax.broadcasted_iota(jnp.int32, (1, GATE_LANES), 1)
    r = lax.broadcasted_iota(jnp.int32, (tq, lc), 0)
    s_idx = lax.broadcasted_iota(jnp.int32, (tq, lc), 1)
    tril = s_idx <= r
    eye = s_idx == r
    row_valid = lax.broadcasted_iota(jnp.int32, (tq, 1), 0) < valid
    lane_valid = lax.broadcasted_iota(jnp.int32, (1, lc), 1) < valid
    scale = DH_A ** -0.5
    cw = cw_ref[...]

    def blockdiag(xb, w_ref):
        nblk = D_MODEL // BD_BLOCK
        return jnp.concatenate(
            [jnp.dot(xb[:, j * BD_BLOCK:(j + 1) * BD_BLOCK], w_ref[j], preferred_element_type=F32)
             for j in range(nblk)], axis=1)

    def pad_rows(x):
        if tq == lc:
            return x
        return jnp.concatenate([x, jnp.zeros((lc - tq, x.shape[1]), x.dtype)], axis=0)

    def front(x_in, y):
        xconv = y * jax.nn.sigmoid(y)
        xcb = xconv.astype(BF16)
        q32 = blockdiag(xcb, wq_ref)
        k32 = blockdiag(xcb, wk_ref)
        v32 = blockdiag(x_in.astype(BF16), wv_ref)
        if_pre = (jnp.dot(q32.astype(BF16), wif_ref[0:D_MODEL, :], preferred_element_type=F32)
                  + jnp.dot(k32.astype(BF16), wif_ref[D_MODEL:2 * D_MODEL, :], preferred_element_type=F32)
                  + jnp.dot(v32.astype(BF16), wif_ref[2 * D_MODEL:3 * D_MODEL, :], preferred_element_type=F32)
                  + bif_ref[...])
        return xconv, q32, k32, v32, if_pre, _log_sigmoid(if_pre)

    if packed:
        tile = xm_ref[0]
        hists = [st_ref[i] for i in range(nsub)]
        shared = front(tile, _packed_tile_conv(tile, hists, cw, cb_ref[...], valid))

    packed_out = []
    for i in range(nsub):
        if packed:
            x_in = _unpack(tile, i, valid)
            z_in = _unpack(z_ref[0], i, valid)
            cs_ref[i] = _last_rows(x_in, valid, CONV_A - 1)
            xconv, q32, k32, v32, if_pre, log_f_all = (_unpack(a, i, valid) for a in shared)
        else:
            x_in = xm_ref[i]
            hist = ext_ref[i] if carried else st_ref[i]
            xe = jnp.concatenate([hist, x_in], axis=0)
            y = cb_ref[...] + x_in * cw[CONV_A - 1:CONV_A, :]
            for j in range(CONV_A - 1):
                shifted = pltpu.roll(xe, CONV_A - 1 - j, axis=0)[SUBLANES:, :]
                y = y + shifted * cw[j:j + 1, :]
            if carried:
                ext_ref[i] = x_in[tq - SUBLANES:tq, :]
            xconv, q32, k32, v32, if_pre, log_f_all = front(x_in, y)
        qb = q32.astype(BF16)
        kpad = pad_rows(k32).astype(BF16)
        vpad = pad_rows(v32).astype(BF16)

        m_all = min_ref[i, 0:1, :]
        m_out = m_all
        heads = []
        for h in range(H_A):
            hs = slice(h * DH_A, (h + 1) * DH_A)
            qh, kh, vh = qb[:, hs], kpad[:, hs], vpad[:, hs]
            li_col = jnp.sum(jnp.where(lane == h, if_pre, 0.0), axis=1, keepdims=True)
            lf_col = jnp.sum(jnp.where(lane == H_A + h, log_f_all, 0.0), axis=1, keepdims=True)
            if packed:
                li_col = jnp.where(row_valid, li_col, NEG_INF)
                lf_col = jnp.where(row_valid, lf_col, 0.0)
            m_prev = jnp.sum(jnp.where(lane == h, m_all, 0.0), axis=1, keepdims=True)

            b_row = jnp.sum(jnp.where(r <= s_idx, lf_col, 0.0), axis=0, keepdims=True)
            li_row = jnp.sum(jnp.where(eye, li_col, 0.0), axis=0, keepdims=True)
            if packed:
                li_row = jnp.where(lane_valid, li_row, NEG_INF)
            b_col = jnp.sum(jnp.where(eye, b_row, 0.0), axis=1, keepdims=True)

            d = jnp.where(tril, b_col - b_row + li_row, NEG_INF)
            inter = m_prev + b_col
            m_t = jnp.maximum(jnp.max(d, axis=1, keepdims=True), inter)
            w = jnp.exp(d - m_t)
            g = jnp.exp(inter - m_t)
            qk = lax.dot_general(qh, kh, (((1,), (1,)), ((), ())), preferred_element_type=F32)
            s = qk * (w * scale)
            c_h = cin_ref[i, h]
            n_h = nin_ref[i, h:h + 1, :]
            num = (g * jnp.dot(qh, c_h.astype(BF16), preferred_element_type=F32)
                   + jnp.dot(s.astype(BF16), vh, preferred_element_type=F32))
            den = (g * jnp.sum(q32[:, hs] * n_h, axis=1, keepdims=True)
                   + jnp.sum(s, axis=1, keepdims=True))
            ht = num / jnp.maximum(jnp.abs(den), jnp.exp(-m_t))

            b_last = jnp.sum(lf_col, axis=0, keepdims=True)
            a_col = b_last - b_col + li_col
            m_new = jnp.maximum(m_prev + b_last, jnp.max(a_col, axis=0, keepdims=True))
            decay = jnp.exp(m_prev + b_last - m_new)
            kw = k32[:, hs] * (jnp.exp(a_col - m_new) * scale)
            kwt = jnp.transpose(pad_rows(kw)).astype(BF16)
            c_ref[i, h] = decay * c_h + jnp.dot(kwt, vh, preferred_element_type=F32)
            n_ref[i, h:h + 1, :] = decay * n_h + jnp.sum(kw, axis=0, keepdims=True)
            m_out = jnp.where(lane == h, m_new, m_out)

            hn = _layer_norm_rows(ht) * nh_ref[:, hs]
            z_h = z_in[:, hs] if packed else z_ref[i, :, hs]
            ya_h = jax.nn.sigmoid(z_h) * (hn + sk_ref[:, hs] * xconv[:, hs])
            if packed:
                heads.append(ya_h)
            else:
                ya_ref[i, :, hs] = ya_h.astype(ya_ref.dtype)
        m_ref[i] = jnp.broadcast_to(m_out, (SUBLANES, GATE_LANES))
        if packed:
            packed_out.append(heads)
    if packed:
        for h in range(H_A):
            tile = _pack([heads[h] for heads in packed_out], valid)
            ya_ref[0, :, h * DH_A:(h + 1) * DH_A] = tile.astype(ya_ref.dtype)


def branch_a(proj, states, p, *, nseq, seq, out_dtype):
    r = proj.shape[0]
    zero_state = states is None
    if seq >= CHUNK:
        nsub, tq, valid, nchunk = 2, CHUNK, CHUNK, seq // CHUNK
        proj3 = proj.reshape(nseq, seq, N_IN)
        row = lambda col: pl.BlockSpec((nsub, tq, D_MODEL), lambda g, c, col=col: (g, c, col))
        ya_shape = (nseq, seq, D_MODEL)
        assert zero_state
        hist = jnp.zeros((nseq, SUBLANES, D_MODEL), F32)
    else:
        nsub, tq, valid, nchunk = SUBLANES // seq, SUBLANES, seq, 1
        proj3 = proj.reshape(r // SUBLANES, SUBLANES, N_IN)
        row = lambda col: pl.BlockSpec((1, tq, D_MODEL), lambda g, c, col=col: (g, 0, col))
        ya_shape = (r // SUBLANES, SUBLANES, D_MODEL)
        assert not zero_state
        hist = states[0]
    packed = valid < tq
    full = lambda shape: pl.BlockSpec(shape, lambda g, c: (0,) * len(shape))
    c_spec = pl.BlockSpec((nsub, H_A, DH_A, DH_A), lambda g, c: (g, 0, 0, 0))
    n_spec = pl.BlockSpec((nsub, H_A, DH_A), lambda g, c: (g, 0, 0))
    m_spec = pl.BlockSpec((nsub, SUBLANES, GATE_LANES), lambda g, c: (g, 0, 0))
    hist_spec = pl.BlockSpec((nsub,) + hist.shape[1:], lambda g, c: (g, 0, 0))
    in_specs = [row(0), row(1), hist_spec]
    args = [proj3, proj3, hist]
    if not zero_state:
        in_specs += [c_spec, n_spec, m_spec]
        args += list(states[1:])
    nblk = D_MODEL // BD_BLOCK
    in_specs += [full((CONV_A, D_MODEL)), full((1, D_MODEL)),
                 full((nblk, BD_BLOCK, BD_BLOCK)), full((nblk, BD_BLOCK, BD_BLOCK)),
                 full((nblk, BD_BLOCK, BD_BLOCK)),
                 full((3 * D_MODEL, GATE_LANES)), full((1, GATE_LANES)),
                 full((1, D_MODEL)), full((1, D_MODEL))]
    args += [p["conv_a_w"], p["conv_a_b"], p["wq_bd"], p["wk_bd"], p["wv_bd"], p["w_if"], p["b_if"],
             p["norm_h_a"], p["skip_a"]]
    kern = functools.partial(_branch_a_kernel, nsub=nsub, tq=tq, valid=valid, nchunk=nchunk,
                             zero_state=zero_state)
    out_specs = [row(0), c_spec, n_spec, m_spec]
    out_shape = [jax.ShapeDtypeStruct(ya_shape, out_dtype),
                 jax.ShapeDtypeStruct((nseq, H_A, DH_A, DH_A), F32),
                 jax.ShapeDtypeStruct((nseq, H_A, DH_A), F32),
                 jax.ShapeDtypeStruct((nseq, SUBLANES, GATE_LANES), F32)]
    scratch = []
    if packed:
        out_specs.append(hist_spec)
        out_shape.append(jax.ShapeDtypeStruct(hist.shape, F32))
    else:
        scratch.append(pltpu.VMEM((nsub, SUBLANES, D_MODEL), F32))
    outs = pl.pallas_call(
        kern,
        grid=(nseq // nsub, nchunk),
        in_specs=in_specs,
        out_specs=out_specs,
        out_shape=out_shape,
        scratch_shapes=scratch,
        compiler_params=_cparams("parallel", "arbitrary"),
        name="branch_a",
    )(*args)
    ya, c_new, n_new, m_new = outs[:4]
    hist_new = outs[4] if packed else None
    return ya.reshape(r, D_MODEL), c_new, n_new, m_new[:, 0, :H_A], hist_new


def _gmlp_kernel(u_ref, v_ref, nv_ref, wmix_ref, bmix_ref, yb_ref, *vn_ref):
    bias = bmix_ref[...]
    lane = lax.broadcasted_iota(jnp.int32, (1, GATE_LANES), 1)
    b_cols = [jnp.sum(jnp.where(lane == g, bias, 0.0), axis=1, keepdims=True) for g in range(G_B)]
    for c in range(u_ref.shape[0] // CHUNK):
        rows = slice(c * CHUNK, (c + 1) * CHUNK)
        vn = _layer_norm_rows(jax.nn.gelu(v_ref[rows, :])) * nv_ref[...]
        if vn_ref:
            vn_ref[0][rows, :] = vn
        vnb = vn.astype(BF16)
        mixed = [jnp.dot(wmix_ref[g], vnb[:, g * DG_B:(g + 1) * DG_B],
                         preferred_element_type=F32) + b_cols[g] for g in range(G_B)]
        yb_ref[rows, :] = (jax.nn.gelu(u_ref[rows, :])
                           * jnp.concatenate(mixed, axis=1)).astype(yb_ref.dtype)


def gmlp(proj, norm_v, w_s, b_s, seq, *, emit_vn, tile=512):
    r = proj.shape[0]
    lc = min(seq, CHUNK)
    per = CHUNK // lc
    tril = jnp.tril(jnp.ones((lc, lc), bool))
    w_small = jnp.where(tril[None], w_s[:, :lc, :lc], 0.0)
    wmix = jnp.einsum("ab,gts->gatbs", jnp.eye(per, dtype=F32), w_small)
    wmix = wmix.reshape(G_B, CHUNK, CHUNK).astype(BF16)
    bmix = jnp.pad(jnp.tile(b_s[:, :lc].T, (per, 1)), ((0, 0), (0, GATE_LANES - G_B)))
    u_col = 2 * D_MODEL // D_B
    row = pl.BlockSpec((tile, D_B), lambda i: (i, 0))
    outs = pl.pallas_call(
        _gmlp_kernel,
        grid=(r // tile,),
        in_specs=[pl.BlockSpec((tile, D_B), lambda i: (i, u_col)),
                  pl.BlockSpec((tile, D_B), lambda i: (i, u_col + 1)),
                  pl.BlockSpec((1, D_B), lambda i: (0, 0)),
                  pl.BlockSpec((G_B, CHUNK, CHUNK), lambda i: (0, 0, 0)),
                  pl.BlockSpec((CHUNK, GATE_LANES), lambda i: (0, 0))],
        out_specs=[row, row] if emit_vn else [row],
        out_shape=[jax.ShapeDtypeStruct((r, D_B), BF16)]
                  + ([jax.ShapeDtypeStruct((r, D_B), F32)] if emit_vn else []),
        compiler_params=_cparams("parallel"),
        name="gmlp",
    )(proj, proj, norm_v.reshape(1, D_B), wmix, bmix)
    return (outs[0], outs[1]) if emit_vn else (outs[0], None)


def _attend(q, k, v):
    outs = []
    for h in range(H_C):
        hs = slice(h * DH_C, (h + 1) * DH_C)
        s = lax.dot_general(q[:, hs], k[:, hs], (((1,), (1,)), ((), ())),
                            preferred_element_type=F32) * (DH_C ** -0.5)
        e = jnp.exp(s - jnp.max(s, axis=-1, keepdims=True))
        prob = e / jnp.sum(e, axis=-1, keepdims=True)
        outs.append(jnp.dot(prob.astype(BF16), v[:, hs], preferred_element_type=F32))
    return jnp.concatenate(outs, axis=1)


def _attend_flat(qs, kfs, vfs):
    nb = len(qs)
    mem = MEM_LEN * H_C
    q_all = jnp.concatenate([q[:, h * DH_C:(h + 1) * DH_C] for q in qs for h in range(H_C)],
                            axis=0).astype(BF16)
    k_all, v_all = jnp.concatenate(kfs, axis=0), jnp.concatenate(vfs, axis=0)
    s = lax.dot_general(q_all, k_all, (((1,), (1,)), ((), ())),
                        preferred_element_type=F32) * (DH_C ** -0.5)
    row = lax.broadcasted_iota(jnp.int32, s.shape, 0)
    col = lax.broadcasted_iota(jnp.int32, s.shape, 1)
    same = ((row // (H_C * SUBLANES) == col // mem) & (row // SUBLANES % H_C == col % H_C))
    s = jnp.where(same, s, NEG_INF)
    e = jnp.exp(s - jnp.max(s, axis=-1, keepdims=True))
    prob = e / jnp.sum(e, axis=-1, keepdims=True)
    o = jnp.dot(prob.astype(BF16), v_all, preferred_element_type=F32)
    return [jnp.concatenate([o[(b * H_C + h) * SUBLANES:(b * H_C + h + 1) * SUBLANES]
                             for h in range(H_C)], axis=1) for b in range(nb)]


def _xattn_kernel(q_ref, k_ref, v_ref, o_ref, *, nseq, rq, valid):
    per = rq // valid
    for t in range(nseq // per):
        tile = q_ref[t * rq:(t + 1) * rq, :]
        if per == 1:
            mem = slice(t * MEM_LEN, (t + 1) * MEM_LEN)
            o = _attend(tile.astype(BF16), k_ref[mem, :].astype(BF16), v_ref[mem, :].astype(BF16))
        else:
            seqs = range(t * per, (t + 1) * per)
            flat = lambda ref, i: ref[i].reshape(MEM_LEN * H_C, DH_C).astype(BF16)
            o = _pack(_attend_flat([_unpack(tile, j, valid) for j in range(per)],
                                   [flat(k_ref, i) for i in seqs], [flat(v_ref, i) for i in seqs]),
                      valid)
        o_ref[t * rq:(t + 1) * rq, :] = o.astype(o_ref.dtype)


def xattn(proj, k_arr, v_arr, k_col, v_col, *, nbatch, nseq, rq, out_dtype):
    r = proj.shape[0]
    seq = r // nbatch
    valid = min(seq, rq)
    rows = nseq * valid
    nq = seq // valid
    assert nseq == 1 or nq == 1
    q_col = (2 * D_MODEL + 2 * D_B) // D_C

    def mem_spec(arr, col):
        if arr.ndim == 4:
            return pl.BlockSpec((nseq, MEM_LEN, H_C, DH_C), lambda g, t: (g, 0, 0, 0))
        return pl.BlockSpec((nseq * MEM_LEN, D_C), lambda g, t: (g, col))

    kern = functools.partial(_xattn_kernel, nseq=nseq, rq=rq, valid=valid)
    return pl.pallas_call(
        kern,
        grid=(nbatch // nseq, nq),
        in_specs=[pl.BlockSpec((rows, D_C), lambda g, t: (g * nq + t, q_col)),
                  mem_spec(k_arr, k_col), mem_spec(v_arr, v_col)],
        out_specs=pl.BlockSpec((rows, D_C), lambda g, t: (g * nq + t, 0)),
        out_shape=jax.ShapeDtypeStruct((r, D_C), out_dtype),
        compiler_params=_cparams("parallel", "arbitrary"),
        name="xattn",
    )(proj, k_arr, v_arr)


def _mix_out_kernel(ya_ref, yb_ref, yc_ref, ga0, ga1, gb0, gb1, gc0, gc1, x_ref,
                    wa_ref, wb_ref, wc_ref, wo_ref, g1_ref, g2_ref, x1_ref, h2_ref):
    def branch(y_ref, w_ref, lo, hi):
        gate = jnp.concatenate([lo[...], hi[...]], axis=1)
        return jax.nn.sigmoid(gate) * jnp.dot(y_ref[...].astype(BF16), w_ref[...],
                                              preferred_element_type=F32)

    merged = (branch(ya_ref, wa_ref, ga0, ga1) + branch(yb_ref, wb_ref, gb0, gb1)
              + branch(yc_ref, wc_ref, gc0, gc1))
    o = jnp.dot(merged.astype(BF16), wo_ref[...], preferred_element_type=F32)
    x1 = x_ref[...] + _rms(o) * g1_ref[...]
    x1_ref[...] = x1
    h2_ref[...] = (_rms(x1) * g2_ref[...]).astype(h2_ref.dtype)


def mix_out(ya, yb, yc, proj, x, w_a, w_b, w_c, w_o, g_post, g_pre, bm=256):
    r = x.shape[0]
    half = D_MODEL // 2
    g0 = GATE_COL0 // half
    gate = lambda k: pl.BlockSpec((bm, half), lambda i, k=k: (i, g0 + k))
    rowspec = lambda d: pl.BlockSpec((bm, d), lambda i: (i, 0))
    vec = pl.BlockSpec((1, D_MODEL), lambda i: (0, 0))
    return pl.pallas_call(
        _mix_out_kernel,
        grid=(r // bm,),
        in_specs=[rowspec(D_MODEL), rowspec(D_B), rowspec(D_C)] + [gate(k) for k in range(6)]
                 + [rowspec(D_MODEL), _resident((D_MODEL, D_MODEL)), _resident((D_B, D_MODEL)),
                    _resident((D_C, D_MODEL)), _resident((D_MODEL, D_MODEL)), vec, vec],
        out_specs=[rowspec(D_MODEL), rowspec(D_MODEL)],
        out_shape=[jax.ShapeDtypeStruct((r, D_MODEL), F32), jax.ShapeDtypeStruct((r, D_MODEL), BF16)],
        compiler_params=_cparams("parallel"),
        name="mix_out",
    )(ya, yb, yc, proj, proj, proj, proj, proj, proj, x, w_a, w_b, w_c, w_o,
      g_post.reshape(1, D_MODEL), g_pre.reshape(1, D_MODEL))


def _gated_act(gate, val):
    return jax.nn.gelu(gate) * val


def _ffn_up_act_kernel(a_ref, wg_ref, wv_ref, cwg_ref, cwv_ref, cbg_ref, cbv_ref,
                       o_ref, sg_ref, sv_ref, ext_ref, *, bm):
    @pl.when(pl.program_id(2) == 0)
    def _():
        ext_ref[:, 0:SUBLANES, :] = jnp.zeros((2, SUBLANES, ext_ref.shape[2]), F32)

    a = a_ref[...]

    def conv(which, w_ref, cw_ref, cb_ref, st_ref):
        u = jnp.dot(a, w_ref[...].astype(BF16), preferred_element_type=F32)
        ext_ref[which, SUBLANES:SUBLANES + bm, :] = u
        cw = cw_ref[...]
        y = cb_ref[...] + u * cw[CONV_F - 1:CONV_F, :]
        for j in range(CONV_F - 1):
            y = y + ext_ref[which, pl.ds(SUBLANES - (CONV_F - 1) + j, bm), :] * cw[j:j + 1, :]
        tail = u[bm - SUBLANES:bm, :]
        ext_ref[which, 0:SUBLANES, :] = tail
        st_ref[0] = tail
        return y

    gate = conv(0, wg_ref, cwg_ref, cbg_ref, sg_ref)
    val = conv(1, wv_ref, cwv_ref, cbv_ref, sv_ref)
    o_ref[...] = _gated_act(gate, val).astype(o_ref.dtype)


def ffn_up_act(h2, w_up, conv_w, conv_b, *, nseq, bm=1024, bn=512):
    r = h2.shape[0]
    tiles = r // nseq // bm
    ncol = D_FF // bn
    wspec = lambda off: pl.BlockSpec((D_MODEL, bn), lambda j, b, t, off=off: (0, off + j))
    cwspec = lambda off: pl.BlockSpec((CONV_F, bn), lambda j, b, t, off=off: (0, off + j))
    cbspec = lambda off: pl.BlockSpec((1, bn), lambda j, b, t, off=off: (0, off + j))
    st_spec = pl.BlockSpec((1, SUBLANES, bn), lambda j, b, t: (b, 0, j))
    kern = functools.partial(_ffn_up_act_kernel, bm=bm)
    return pl.pallas_call(
        kern,
        grid=(ncol, nseq, tiles),
        in_specs=[pl.BlockSpec((bm, D_MODEL), lambda j, b, t: (b * tiles + t, 0)),
                  wspec(0), wspec(ncol), cwspec(0), cwspec(ncol), cbspec(0), cbspec(ncol)],
        out_specs=[pl.BlockSpec((bm, bn), lambda j, b, t: (b * tiles + t, j)), st_spec, st_spec],
        out_shape=[jax.ShapeDtypeStruct((r, D_FF), BF16),
                   jax.ShapeDtypeStruct((nseq, SUBLANES, D_FF), F32),
                   jax.ShapeDtypeStruct((nseq, SUBLANES, D_FF), F32)],
        scratch_shapes=[pltpu.VMEM((2, SUBLANES + bm, bn), F32)],
        compiler_params=_cparams("parallel", "arbitrary", "arbitrary"),
        name="ffn_up_act",
    )(h2, w_up, w_up, conv_w, conv_w, conv_b, conv_b)


def _ffn_act_kernel(g_ref, v_ref, sg_ref, sv_ref, wg_ref, wv_ref, bg_ref, bv_ref,
                    o_ref, ng_ref, nv_ref, *, nseq, valid):
    per = SUBLANES // valid
    wg, wv, bg, bv = wg_ref[...], wv_ref[...], bg_ref[...], bv_ref[...]
    for t in range(nseq // per):
        rows = slice(t * SUBLANES, (t + 1) * SUBLANES)
        seqs = range(t * per, (t + 1) * per)
        g_tile, v_tile = g_ref[rows, :], v_ref[rows, :]
        gate = _packed_tile_conv(g_tile, [sg_ref[i] for i in seqs], wg, bg, valid)
        val = _packed_tile_conv(v_tile, [sv_ref[i] for i in seqs], wv, bv, valid)
        o_ref[rows, :] = _gated_act(gate, val).astype(o_ref.dtype)
        for j, i in enumerate(seqs):
            ng_ref[i] = _last_rows(_unpack(g_tile, j, valid), valid, CONV_F - 1)
            nv_ref[i] = _last_rows(_unpack(v_tile, j, valid), valid, CONV_F - 1)


def ffn_act(up, hist, conv_w, conv_b, *, nbatch, nseq, bc=2048):
    r = up.shape[0]
    valid = r // nbatch
    ncol = D_FF // bc
    nh = CONV_F - 1
    x_spec = lambda off: pl.BlockSpec((nseq * valid, bc), lambda g, j, off=off: (g, off + j))
    s_spec = lambda off: pl.BlockSpec((nseq, nh, bc), lambda g, j, off=off: (g, 0, off + j))
    w_spec = lambda off: pl.BlockSpec((CONV_F, bc), lambda g, j, off=off: (0, off + j))
    b_spec = lambda off: pl.BlockSpec((1, bc), lambda g, j, off=off: (0, off + j))
    kern = functools.partial(_ffn_act_kernel, nseq=nseq, valid=valid)
    return pl.pallas_call(
        kern,
        grid=(nbatch // nseq, ncol),
        in_specs=[x_spec(0), x_spec(ncol), s_spec(0), s_spec(ncol), w_spec(0), w_spec(ncol),
                  b_spec(0), b_spec(ncol)],
        out_specs=[pl.BlockSpec((nseq * valid, bc), lambda g, j: (g, j)), s_spec(0), s_spec(0)],
        out_shape=[jax.ShapeDtypeStruct((r, D_FF), F32),
                   jax.ShapeDtypeStruct((nbatch, nh, D_FF), F32),
                   jax.ShapeDtypeStruct((nbatch, nh, D_FF), F32)],
        compiler_params=_cparams("parallel", "parallel"),
        name="ffn_act",
    )(up, up, hist, hist, conv_w, conv_w, conv_b, conv_b)


def _ffn_down_kernel(a_ref, w_ref, x_ref, g_ref, y_ref):
    f = jnp.dot(a_ref[...].astype(BF16), w_ref[...], preferred_element_type=F32)
    y_ref[...] = x_ref[...] + _rms(f) * g_ref[...]


def ffn_down_post(act, w_down, x1, g, bm=256):
    r = x1.shape[0]
    return pl.pallas_call(
        _ffn_down_kernel,
        grid=(r // bm,),
        in_specs=[pl.BlockSpec((bm, D_FF), lambda i: (i, 0)), _resident((D_FF, D_MODEL)),
                  pl.BlockSpec((bm, D_MODEL), lambda i: (i, 0)),
                  pl.BlockSpec((1, D_MODEL), lambda i: (0, 0))],
        out_specs=pl.BlockSpec((bm, D_MODEL), lambda i: (i, 0)),
        out_shape=jax.ShapeDtypeStruct((r, D_MODEL), F32),
        compiler_params=_cparams("parallel"),
        name="ffn_down_post",
    )(act, w_down, x1, g.reshape(1, D_MODEL))


def _blockdiag_tiles(w):
    nblk = D_MODEL // BD_BLOCK
    rows = w.reshape(nblk, BD_BLOCK, QKV_BLOCK)
    tiled = jnp.tile(rows, (1, 1, BD_BLOCK // QKV_BLOCK))
    idx = jnp.arange(BD_BLOCK) // QKV_BLOCK
    same_block = idx[:, None] == idx[None, :]
    return jnp.where(same_block[None], tiled, 0.0).astype(BF16)


def _prep_params(conv_a_w, conv_a_b, w_q_a, w_k_a, w_v_a, w_if_a, b_if_a, norm_h_a, skip_a):
    pad = GATE_LANES - 2 * H_A
    return {
        "conv_a_w": conv_a_w,
        "conv_a_b": conv_a_b.reshape(1, D_MODEL),
        "wq_bd": _blockdiag_tiles(w_q_a),
        "wk_bd": _blockdiag_tiles(w_k_a),
        "wv_bd": _blockdiag_tiles(w_v_a),
        "w_if": jnp.pad(w_if_a, ((0, 0), (0, pad))).astype(BF16),
        "b_if": jnp.pad(b_if_a, (0, pad)).reshape(1, GATE_LANES),
        "norm_h_a": norm_h_a.reshape(1, D_MODEL),
        "skip_a": skip_a.reshape(1, D_MODEL),
    }


def _trunk(x, mem, states, w, pa, *, nseq, seq):
    prompt = states is None
    k_arr, v_arr, k_col, v_col = mem

    h = rms_cast(x, w["norm_mix_pre"])
    proj = matmul(h, w["w_in"], 1024, 1024)

    yb, vn = gmlp(proj, w["norm_v_b"], w["w_s_b"], w["b_s_b"], seq, emit_vn=not prompt)
    ya, c_new, n_new, m_new, conv_new = branch_a(proj, None if prompt else states[:4], pa,
                                                 nseq=nseq, seq=seq,
                                                 out_dtype=BF16 if prompt else F32)
    if prompt:
        yc = xattn(proj, k_arr, v_arr, k_col, v_col, nbatch=nseq, nseq=1, rq=1024, out_dtype=BF16)
        conv_new = proj.reshape(nseq, seq, N_IN)[:, seq - (CONV_A - 1):, :D_MODEL]
    else:
        yc = xattn(proj, k_arr, v_arr, k_col, v_col, nbatch=nseq, nseq=8, rq=SUBLANES, out_dtype=F32)

    x1, h2 = mix_out(ya, yb, yc, proj, x, w["w_br_a"], w["w_br_b"], w["w_br_c"], w["w_out"],
                     w["norm_mix_post"], w["norm_ffn_pre"])

    conv_b = w["conv_ffn_b"].reshape(1, 2 * D_FF)
    if prompt:
        act, tail_g, tail_v = ffn_up_act(h2, w["w_ffn_up"], w["conv_ffn_w"], conv_b, nseq=nseq)
        ffn_new = jnp.concatenate([tail_g, tail_v], axis=-1)[:, SUBLANES - (CONV_F - 1):]
    else:
        up = matmul(h2, w["w_ffn_up"], 1024, 1024)
        act, new_g, new_v = ffn_act(up, states[4], w["conv_ffn_w"], conv_b, nbatch=nseq, nseq=32)
        ffn_new = jnp.concatenate([new_g, new_v], axis=-1)
    y = ffn_down_post(act, w["w_ffn_down"], x1, w["norm_ffn_post"])
    return y, conv_new, ffn_new, vn, c_new, n_new, m_new


def kernel(x_prompt, x_sample, state_conv_a, state_mlstm_c, state_mlstm_n, state_mlstm_m, cache_mem_k, cache_mem_v, state_ffn_conv, mem_prompt, norm_mix_pre, w_in, conv_a_w, conv_a_b, w_q_a, w_k_a, w_v_a, w_if_a, b_if_a, norm_h_a, skip_a, w_br_a, norm_v_b, w_s_b, b_s_b, w_br_b, norm_mem, w_mem_kv, w_br_c, w_out, norm_mix_post, norm_ffn_pre, w_ffn_up, conv_ffn_w, conv_ffn_b, w_ffn_down, norm_ffn_post):
    nb, seq, _ = x_prompt.shape
    db, dseq, _ = x_sample.shape
    w = {
        "norm_mix_pre": norm_mix_pre[0], "w_in": w_in[0], "norm_v_b": norm_v_b[0],
        "w_s_b": w_s_b[0], "b_s_b": b_s_b[0],
        "w_br_a": w_br_a[0].astype(BF16), "w_br_b": w_br_b[0].astype(BF16),
        "w_br_c": w_br_c[0].astype(BF16), "w_out": w_out[0].astype(BF16),
        "norm_mix_post": norm_mix_post[0], "norm_ffn_pre": norm_ffn_pre[0], "w_ffn_up": w_ffn_up[0],
        "conv_ffn_w": conv_ffn_w[0], "conv_ffn_b": conv_ffn_b[0],
        "w_ffn_down": w_ffn_down[0].astype(BF16), "norm_ffn_post": norm_ffn_post[0],
    }
    pa = _prep_params(conv_a_w[0], conv_a_b[0], w_q_a[0], w_k_a[0], w_v_a[0], w_if_a[0], b_if_a[0],
                      norm_h_a[0], skip_a[0])

    mem_h = rms_cast(mem_prompt.reshape(nb * MEM_LEN, D_MODEL), norm_mem[0])
    mem_k = matmul(mem_h, w_mem_kv[0], 1024, 1024, 0, D_C)
    mem_v = matmul(mem_h, w_mem_kv[0], 1024, 1024, D_C, D_C)
    yp, conv_p, ffn_p, _, c_p, n_p, m_p = _trunk(
        x_prompt.reshape(nb * seq, D_MODEL), (mem_k, mem_v, 0, 0), None, w, pa, nseq=nb, seq=seq)

    m0 = jnp.pad(state_mlstm_m[0][:, None, :], ((0, 0), (0, SUBLANES - 1), (0, GATE_LANES - H_A)))
    ys, conv_s, ffn_s, vn_s, c_s, n_s, m_s = _trunk(
        x_sample.reshape(db * dseq, D_MODEL), (cache_mem_k[0], cache_mem_v[0], 0, 0),
        (state_conv_a[0], state_mlstm_c[0], state_mlstm_n[0], m0, state_ffn_conv[0]),
        w, pa, nseq=db, seq=dseq)

    return (
        yp.reshape(nb, seq, D_MODEL), ys.reshape(db, dseq, D_MODEL),
        conv_p[None], c_p[None], n_p[None], m_p[None],
        mem_k.reshape(1, nb, MEM_LEN, H_C, DH_C), mem_v.reshape(1, nb, MEM_LEN, H_C, DH_C),
        ffn_p[None],
        conv_s[None], c_s[None], n_s[None], m_s[None],
        vn_s.reshape(1, db, dseq, D_B), ffn_s[None],
    )
```

```python
import functools

import jax
import jax.numpy as jnp
from jax import lax
from jax.experimental import pallas as pl
from jax.experimental.pallas import tpu as pltpu

F32 = jnp.float32
BF16 = jnp.bfloat16

D_MODEL = 2048
H_A = 4
DH_A = 512
QKV_BLOCK = 4
CONV_A = 4
CONV_F = 3
D_B = 1024
G_B = 4
DG_B = 256
H_C = 4
DH_C = 256
D_C = 1024
MEM_LEN = 256
D_FF = 6144
N_IN = 13312
GATE_COL0 = 2 * D_MODEL + 2 * D_B + D_C
EPS = 1e-6
NEG_INF = -1e30

CHUNK = 128
BD_BLOCK = 256
GATE_LANES = 128
SUBLANES = 8
V7X_VMEM_BYTES = 64 * 1024 * 1024
VMEM_LIMIT = V7X_VMEM_BYTES - 8 * 1024 * 1024


def _cparams(*sem):
    return pltpu.CompilerParams(dimension_semantics=sem, vmem_limit_bytes=VMEM_LIMIT)


def _resident(shape):
    return pl.BlockSpec(shape, lambda *_: (0,) * len(shape), pipeline_mode=pl.Buffered(1))


def _rms(x):
    return x * lax.rsqrt(jnp.mean(x * x, axis=-1, keepdims=True) + EPS)


def _unpack(tile, i, valid):
    return tile if i == 0 else pltpu.roll(tile, SUBLANES - i * valid, axis=0)


def _packed_tile_conv(tile, hists, w, b, valid):
    nh = hists[0].shape[0]
    row = lax.broadcasted_iota(jnp.int32, (SUBLANES, 1), 0)
    y = b + tile * w[nh:nh + 1, :]
    for j in range(nh):
        s = nh - j
        tap = pltpu.roll(tile, s, axis=0)
        for i, hist in enumerate(hists):
            for t in range(min(s, valid)):
                tap = jnp.where(row == i * valid + t, hist[nh - s + t:nh - s + t + 1, :], tap)
        y = y + tap * w[j:j + 1, :]
    return y


def _last_rows(x, valid, n):
    return pltpu.roll(x, SUBLANES - (valid - n), axis=0)[0:n]


def _pack(parts, valid):
    row = lax.broadcasted_iota(jnp.int32, (SUBLANES, 1), 0)
    out = parts[0]
    for i in range(1, len(parts)):
        out = jnp.where(row < i * valid, out, pltpu.roll(parts[i], i * valid, axis=0))
    return out


def _rms_kernel(x_ref, g_ref, o_ref):
    o_ref[...] = (_rms(x_ref[...]) * g_ref[...]).astype(o_ref.dtype)


def rms_cast(x, g, tile=1024):
    r, d = x.shape
    tile = min(tile, r)
    return pl.pallas_call(
        _rms_kernel,
        grid=(r // tile,),
        in_specs=[pl.BlockSpec((tile, d), lambda i: (i, 0)),
                  pl.BlockSpec((1, d), lambda i: (0, 0))],
        out_specs=pl.BlockSpec((tile, d), lambda i: (i, 0)),
        out_shape=jax.ShapeDtypeStruct((r, d), BF16),
        compiler_params=_cparams("parallel"),
        name="rms_cast",
    )(x, g.reshape(1, d))


def _mm_kernel(a_ref, w_ref, o_ref):
    o_ref[...] = jnp.dot(a_ref[...].astype(BF16), w_ref[...].astype(BF16),
                         preferred_element_type=F32)


def matmul(a, w, bm, bn, col0=0, n=None):
    r, k = a.shape
    n = w.shape[1] if n is None else n
    bm = min(bm, r)
    j0 = col0 // bn
    return pl.pallas_call(
        _mm_kernel,
        grid=(n // bn, r // bm),
        in_specs=[pl.BlockSpec((bm, k), lambda j, i: (i, 0)),
                  pl.BlockSpec((k, bn), lambda j, i: (0, j0 + j))],
        out_specs=pl.BlockSpec((bm, bn), lambda j, i: (i, j)),
        out_shape=jax.ShapeDtypeStruct((r, n), F32),
        compiler_params=_cparams("parallel", "arbitrary"),
        name="matmul",
    )(a, w)


def _log_sigmoid(x):
    return jnp.minimum(x, 0.0) - jnp.log1p(jnp.exp(-jnp.abs(x)))


def _layer_norm_rows(x):
    xc = x - jnp.mean(x, axis=-1, keepdims=True)
    return xc * lax.rsqrt(jnp.mean(xc * xc, axis=-1, keepdims=True) + EPS)


def _branch_a_kernel(*refs, nsub, tq, valid, nchunk, zero_state):
    n_in = 3 if zero_state else 6
    xm_ref, z_ref, st_ref = refs[:3]
    (cw_ref, cb_ref, wq_ref, wk_ref, wv_ref, wif_ref, bif_ref, nh_ref, sk_ref,
     ya_ref, c_ref, n_ref, m_ref) = refs[n_in:n_in + 13]
    lc = CHUNK
    packed = valid < tq
    carried = nchunk > 1
    chunk = pl.program_id(1)
    cs_ref, ext_ref = (refs[n_in + 13], None) if packed else (None, refs[n_in + 13])

    if carried:
        @pl.when(chunk == 0)
        def _init():
            ext_ref[...] = st_ref[...]
            if zero_state:
                c_ref[...] = jnp.zeros(c_ref.shape, F32)
                n_ref[...] = jnp.zeros(n_ref.shape, F32)
                m_ref[...] = jnp.zeros(m_ref.shape, F32)
            else:
                c_ref[...] = refs[3][...]
                n_ref[...] = refs[4][...]
                m_ref[...] = refs[5][...]
        cin_ref, nin_ref, min_ref = c_ref, n_ref, m_ref
    else:
        cin_ref, nin_ref, min_ref = refs[3:6]

    lane = lax.broadcasted_iota(jnp.int32, (1, GATE_LANES), 1)
    r = lax.broadcasted_iota(jnp.int32, (tq, lc), 0)
    s_idx = lax.broadcasted_iota(jnp.int32, (tq, lc), 1)
    tril = s_idx <= r
    eye = s_idx == r
    row_valid = lax.broadcasted_iota(jnp.int32, (tq, 1), 0) < valid
    lane_valid = lax.broadcasted_iota(jnp.int32, (1, lc), 1) < valid
    scale = DH_A ** -0.5
    cw = cw_ref[...]

    def blockdiag(xb, w_ref):
        nblk = D_MODEL // BD_BLOCK
        return jnp.concatenate(
            [jnp.dot(xb[:, j * BD_BLOCK:(j + 1) * BD_BLOCK], w_ref[j], preferred_element_type=F32)
             for j in range(nblk)], axis=1)

    def pad_rows(x):
        if tq == lc:
            return x
        return jnp.concatenate([x, jnp.zeros((lc - tq, x.shape[1]), x.dtype)], axis=0)

    def front(x_in, y):
        xconv = y * jax.nn.sigmoid(y)
        xcb = xconv.astype(BF16)
        q32 = blockdiag(xcb, wq_ref)
        k32 = blockdiag(xcb, wk_ref)
        v32 = blockdiag(x_in.astype(BF16), wv_ref)
        if_pre = (jnp.dot(q32.astype(BF16), wif_ref[0:D_MODEL, :], preferred_element_type=F32)
                  + jnp.dot(k32.astype(BF16), wif_ref[D_MODEL:2 * D_MODEL, :], preferred_element_type=F32)
                  + jnp.dot(v32.astype(BF16), wif_ref[2 * D_MODEL:3 * D_MODEL, :], preferred_element_type=F32)
                  + bif_ref[...])
        return xconv, q32, k32, v32, if_pre, _log_sigmoid(if_pre)

    if packed:
        tile = xm_ref[0]
        hists = [st_ref[i] for i in range(nsub)]
        shared = front(tile, _packed_tile_conv(tile, hists, cw, cb_ref[...], valid))

    packed_out = []
    for i in range(nsub):
        if packed:
            x_in = _unpack(tile, i, valid)
            z_in = _unpack(z_ref[0], i, valid)
            cs_ref[i] = _last_rows(x_in, valid, CONV_A - 1)
            xconv, q32, k32, v32, if_pre, log_f_all = (_unpack(a, i, valid) for a in shared)
        else:
            x_in = xm_ref[i]
            hist = ext_ref[i] if carried else st_ref[i]
            xe = jnp.concatenate([hist, x_in], axis=0)
            y = cb_ref[...] + x_in * cw[CONV_A - 1:CONV_A, :]
            for j in range(CONV_A - 1):
                shifted = pltpu.roll(xe, CONV_A - 1 - j, axis=0)[SUBLANES:, :]
                y = y + shifted * cw[j:j + 1, :]
            if carried:
                ext_ref[i] = x_in[tq - SUBLANES:tq, :]
            xconv, q32, k32, v32, if_pre, log_f_all = front(x_in, y)
        qb = q32.astype(BF16)
        kpad = pad_rows(k32).astype(BF16)
        vpad = pad_rows(v32).astype(BF16)

        m_all = min_ref[i, 0:1, :]
        m_out = m_all
        heads = []
        for h in range(H_A):
            hs = slice(h * DH_A, (h + 1) * DH_A)
            qh, kh, vh = qb[:, hs], kpad[:, hs], vpad[:, hs]
            li_col = jnp.sum(jnp.where(lane == h, if_pre, 0.0), axis=1, keepdims=True)
            lf_col = jnp.sum(jnp.where(lane == H_A + h, log_f_all, 0.0), axis=1, keepdims=True)
            if packed:
                li_col = jnp.where(row_valid, li_col, NEG_INF)
                lf_col = jnp.where(row_valid, lf_col, 0.0)
            m_prev = jnp.sum(jnp.where(lane == h, m_all, 0.0), axis=1, keepdims=True)

            b_row = jnp.sum(jnp.where(r <= s_idx, lf_col, 0.0), axis=0, keepdims=True)
            li_row = jnp.sum(jnp.where(eye, li_col, 0.0), axis=0, keepdims=True)
            if packed:
                li_row = jnp.where(lane_valid, li_row, NEG_INF)
            b_col = jnp.sum(jnp.where(eye, b_row, 0.0), axis=1, keepdims=True)

            d = jnp.where(tril, b_col - b_row + li_row, NEG_INF)
            inter = m_prev + b_col
            m_t = jnp.maximum(jnp.max(d, axis=1, keepdims=True), inter)
            w = jnp.exp(d - m_t)
            g = jnp.exp(inter - m_t)
            qk = lax.dot_general(qh, kh, (((1,), (1,)), ((), ())), preferred_element_type=F32)
            s = qk * (w * scale)
            c_h = cin_ref[i, h]
            n_h = nin_ref[i, h:h + 1, :]
            num = (g * jnp.dot(qh, c_h.astype(BF16), preferred_element_type=F32)
                   + jnp.dot(s.astype(BF16), vh, preferred_element_type=F32))
            den = (g * jnp.sum(q32[:, hs] * n_h, axis=1, keepdims=True)
                   + jnp.sum(s, axis=1, keepdims=True))
            ht = num / jnp.maximum(jnp.abs(den), jnp.exp(-m_t))

            b_last = jnp.sum(lf_col, axis=0, keepdims=True)
            a_col = b_last - b_col + li_col
            m_new = jnp.maximum(m_prev + b_last, jnp.max(a_col, axis=0, keepdims=True))
            decay = jnp.exp(m_prev + b_last - m_new)
            kw = k32[:, hs] * (jnp.exp(a_col - m_new) * scale)
            kwt = jnp.transpose(pad_rows(kw)).astype(BF16)
            c_ref[i, h] = decay * c_h + jnp.dot(kwt, vh, preferred_element_type=F32)
            n_ref[i, h:h + 1, :] = decay * n_h + jnp.sum(kw, axis=0, keepdims=True)
            m_out = jnp.where(lane == h, m_new, m_out)

            hn = _layer_norm_rows(ht) * nh_ref[:, hs]
            z_h = z_in[:, hs] if packed else z_ref[i, :, hs]
            ya_h = jax.nn.sigmoid(z_h) * (hn + sk_ref[:, hs] * xconv[:, hs])
            if packed:
                heads.append(ya_h)
            else:
                ya_ref[i, :, hs] = ya_h.astype(ya_ref.dtype)
        m_ref[i] = jnp.broadcast_to(m_out, (SUBLANES, GATE_LANES))
        if packed:
            packed_out.append(heads)
    if packed:
        for h in range(H_A):
            tile = _pack([heads[h] for heads in packed_out], valid)
            ya_ref[0, :, h * DH_A:(h + 1) * DH_A] = tile.astype(ya_ref.dtype)


def branch_a(proj, states, p, *, nseq, seq, out_dtype):
    r = proj.shape[0]
    zero_state = states is None
    if seq >= CHUNK:
        nsub, tq, valid, nchunk = 2, CHUNK, CHUNK, seq // CHUNK
        proj3 = proj.reshape(nseq, seq, N_IN)
        row = lambda col: pl.BlockSpec((nsub, tq, D_MODEL), lambda g, c, col=col: (g, c, col))
        ya_shape = (nseq, seq, D_MODEL)
        assert zero_state
        hist = jnp.zeros((nseq, SUBLANES, D_MODEL), F32)
    else:
        nsub, tq, valid, nchunk = SUBLANES // seq, SUBLANES, seq, 1
        proj3 = proj.reshape(r // SUBLANES, SUBLANES, N_IN)
        row = lambda col: pl.BlockSpec((1, tq, D_MODEL), lambda g, c, col=col: (g, 0, col))
        ya_shape = (r // SUBLANES, SUBLANES, D_MODEL)
        assert not zero_state
        hist = states[0]
    packed = valid < tq
    full = lambda shape: pl.BlockSpec(shape, lambda g, c: (0,) * len(shape))
    c_spec = pl.BlockSpec((nsub, H_A, DH_A, DH_A), lambda g, c: (g, 0, 0, 0))
    n_spec = pl.BlockSpec((nsub, H_A, DH_A), lambda g, c: (g, 0, 0))
    m_spec = pl.BlockSpec((nsub, SUBLANES, GATE_LANES), lambda g, c: (g, 0, 0))
    hist_spec = pl.BlockSpec((nsub,) + hist.shape[1:], lambda g, c: (g, 0, 0))
    in_specs = [row(0), row(1), hist_spec]
    args = [proj3, proj3, hist]
    if not zero_state:
        in_specs += [c_spec, n_spec, m_spec]
        args += list(states[1:])
    nblk = D_MODEL // BD_BLOCK
    in_specs += [full((CONV_A, D_MODEL)), full((1, D_MODEL)),
                 full((nblk, BD_BLOCK, BD_BLOCK)), full((nblk, BD_BLOCK, BD_BLOCK)),
                 full((nblk, BD_BLOCK, BD_BLOCK)),
                 full((3 * D_MODEL, GATE_LANES)), full((1, GATE_LANES)),
                 full((1, D_MODEL)), full((1, D_MODEL))]
    args += [p["conv_a_w"], p["conv_a_b"], p["wq_bd"], p["wk_bd"], p["wv_bd"], p["w_if"], p["b_if"],
             p["norm_h_a"], p["skip_a"]]
    kern = functools.partial(_branch_a_kernel, nsub=nsub, tq=tq, valid=valid, nchunk=nchunk,
                             zero_state=zero_state)
    out_specs = [row(0), c_spec, n_spec, m_spec]
    out_shape = [jax.ShapeDtypeStruct(ya_shape, out_dtype),
                 jax.ShapeDtypeStruct((nseq, H_A, DH_A, DH_A), F32),
                 jax.ShapeDtypeStruct((nseq, H_A, DH_A), F32),
                 jax.ShapeDtypeStruct((nseq, SUBLANES, GATE_LANES), F32)]
    scratch = []
    if packed:
        out_specs.append(hist_spec)
        out_shape.append(jax.ShapeDtypeStruct(hist.shape, F32))
    else:
        scratch.append(pltpu.VMEM((nsub, SUBLANES, D_MODEL), F32))
    outs = pl.pallas_call(
        kern,
        grid=(nseq // nsub, nchunk),
        in_specs=in_specs,
        out_specs=out_specs,
        out_shape=out_shape,
        scratch_shapes=scratch,
        compiler_params=_cparams("parallel", "arbitrary"),
        name="branch_a",
    )(*args)
    ya, c_new, n_new, m_new = outs[:4]
    hist_new = outs[4] if packed else None
    return ya.reshape(r, D_MODEL), c_new, n_new, m_new[:, 0, :H_A], hist_new


def _gmlp_kernel(u_ref, v_ref, nv_ref, wmix_ref, bmix_ref, yb_ref, *vn_ref):
    bias = bmix_ref[...]
    lane = lax.broadcasted_iota(jnp.int32, (1, GATE_LANES), 1)
    b_cols = [jnp.sum(jnp.where(lane == g, bias, 0.0), axis=1, keepdims=True) for g in range(G_B)]
    for c in range(u_ref.shape[0] // CHUNK):
        rows = slice(c * CHUNK, (c + 1) * CHUNK)
        vn = _layer_norm_rows(jax.nn.gelu(v_ref[rows, :])) * nv_ref[...]
        if vn_ref:
            vn_ref[0][rows, :] = vn
        vnb = vn.astype(BF16)
        mixed = [jnp.dot(wmix_ref[g], vnb[:, g * DG_B:(g + 1) * DG_B],
                         preferred_element_type=F32) + b_cols[g] for g in range(G_B)]
        yb_ref[rows, :] = (jax.nn.gelu(u_ref[rows, :])
                           * jnp.concatenate(mixed, axis=1)).astype(yb_ref.dtype)


def gmlp(proj, norm_v, w_s, b_s, seq, *, emit_vn, tile=512):
    r = proj.shape[0]
    lc = min(seq, CHUNK)
    per = CHUNK // lc
    tril = jnp.tril(jnp.ones((lc, lc), bool))
    w_small = jnp.where(tril[None], w_s[:, :lc, :lc], 0.0)
    wmix = jnp.einsum("ab,gts->gatbs", jnp.eye(per, dtype=F32), w_small)
    wmix = wmix.reshape(G_B, CHUNK, CHUNK).astype(BF16)
    bmix = jnp.pad(jnp.tile(b_s[:, :lc].T, (per, 1)), ((0, 0), (0, GATE_LANES - G_B)))
    u_col = 2 * D_MODEL // D_B
    row = pl.BlockSpec((tile, D_B), lambda i: (i, 0))
    outs = pl.pallas_call(
        _gmlp_kernel,
        grid=(r // tile,),
        in_specs=[pl.BlockSpec((tile, D_B), lambda i: (i, u_col)),
                  pl.BlockSpec((tile, D_B), lambda i: (i, u_col + 1)),
                  pl.BlockSpec((1, D_B), lambda i: (0, 0)),
                  pl.BlockSpec((G_B, CHUNK, CHUNK), lambda i: (0, 0, 0)),
                  pl.BlockSpec((CHUNK, GATE_LANES), lambda i: (0, 0))],
        out_specs=[row, row] if emit_vn else [row],
        out_shape=[jax.ShapeDtypeStruct((r, D_B), BF16)]
                  + ([jax.ShapeDtypeStruct((r, D_B), F32)] if emit_vn else []),
        compiler_params=_cparams("parallel"),
        name="gmlp",
    )(proj, proj, norm_v.reshape(1, D_B), wmix, bmix)
    return (outs[0], outs[1]) if emit_vn else (outs[0], None)


def _attend(q, k, v):
    outs = []
    for h in range(H_C):
        hs = slice(h * DH_C, (h + 1) * DH_C)
        s = lax.dot_general(q[:, hs], k[:, hs], (((1,), (1,)), ((), ())),
                            preferred_element_type=F32) * (DH_C ** -0.5)
        e = jnp.exp(s - jnp.max(s, axis=-1, keepdims=True))
        prob = e / jnp.sum(e, axis=-1, keepdims=True)
        outs.append(jnp.dot(prob.astype(BF16), v[:, hs], preferred_element_type=F32))
    return jnp.concatenate(outs, axis=1)


def _attend_flat(qs, kfs, vfs):
    nb = len(qs)
    mem = MEM_LEN * H_C
    q_all = jnp.concatenate([q[:, h * DH_C:(h + 1) * DH_C] for q in qs for h in range(H_C)],
                            axis=0).astype(BF16)
    k_all, v_all = jnp.concatenate(kfs, axis=0), jnp.concatenate(vfs, axis=0)
    s = lax.dot_general(q_all, k_all, (((1,), (1,)), ((), ())),
                        preferred_element_type=F32) * (DH_C ** -0.5)
    row = lax.broadcasted_iota(jnp.int32, s.shape, 0)
    col = lax.broadcasted_iota(jnp.int32, s.shape, 1)
    same = ((row // (H_C * SUBLANES) == col // mem) & (row // SUBLANES % H_C == col % H_C))
    s = jnp.where(same, s, NEG_INF)
    e = jnp.exp(s - jnp.max(s, axis=-1, keepdims=True))
    prob = e / jnp.sum(e, axis=-1, keepdims=True)
    o = jnp.dot(prob.astype(BF16), v_all, preferred_element_type=F32)
    return [jnp.concatenate([o[(b * H_C + h) * SUBLANES:(b * H_C + h + 1) * SUBLANES]
                             for h in range(H_C)], axis=1) for b in range(nb)]


def _xattn_kernel(q_ref, k_ref, v_ref, o_ref, *, nseq, rq, valid):
    per = rq // valid
    for t in range(nseq // per):
        tile = q_ref[t * rq:(t + 1) * rq, :]
        if per == 1:
            mem = slice(t * MEM_LEN, (t + 1) * MEM_LEN)
            o = _attend(tile.astype(BF16), k_ref[mem, :].astype(BF16), v_ref[mem, :].astype(BF16))
        else:
            seqs = range(t * per, (t + 1) * per)
            flat = lambda ref, i: ref[i].reshape(MEM_LEN * H_C, DH_C).astype(BF16)
            o = _pack(_attend_flat([_unpack(tile, j, valid) for j in range(per)],
                                   [flat(k_ref, i) for i in seqs], [flat(v_ref, i) for i in seqs]),
                      valid)
        o_ref[t * rq:(t + 1) * rq, :] = o.astype(o_ref.dtype)


def xattn(proj, k_arr, v_arr, k_col, v_col, *, nbatch, nseq, rq, out_dtype):
    r = proj.shape[0]
    seq = r // nbatch
    valid = min(seq, rq)
    rows = nseq * valid
    nq = seq // valid
    assert nseq == 1 or nq == 1
    q_col = (2 * D_MODEL + 2 * D_B) // D_C

    def mem_spec(arr, col):
        if arr.ndim == 4:
            return pl.BlockSpec((nseq, MEM_LEN, H_C, DH_C), lambda g, t: (g, 0, 0, 0))
        return pl.BlockSpec((nseq * MEM_LEN, D_C), lambda g, t: (g, col))

    kern = functools.partial(_xattn_kernel, nseq=nseq, rq=rq, valid=valid)
    return pl.pallas_call(
        kern,
        grid=(nbatch // nseq, nq),
        in_specs=[pl.BlockSpec((rows, D_C), lambda g, t: (g * nq + t, q_col)),
                  mem_spec(k_arr, k_col), mem_spec(v_arr, v_col)],
        out_specs=pl.BlockSpec((rows, D_C), lambda g, t: (g * nq + t, 0)),
        out_shape=jax.ShapeDtypeStruct((r, D_C), out_dtype),
        compiler_params=_cparams("parallel", "arbitrary"),
        name="xattn",
    )(proj, k_arr, v_arr)


def _mix_out_kernel(ya_ref, yb_ref, yc_ref, ga0, ga1, gb0, gb1, gc0, gc1, x_ref,
                    wa_ref, wb_ref, wc_ref, wo_ref, g1_ref, g2_ref, x1_ref, h2_ref):
    def branch(y_ref, w_ref, lo, hi):
        gate = jnp.concatenate([lo[...], hi[...]], axis=1)
        return jax.nn.sigmoid(gate) * jnp.dot(y_ref[...].astype(BF16), w_ref[...],
                                              preferred_element_type=F32)

    merged = (branch(ya_ref, wa_ref, ga0, ga1) + branch(yb_ref, wb_ref, gb0, gb1)
              + branch(yc_ref, wc_ref, gc0, gc1))
    o = jnp.dot(merged.astype(BF16), wo_ref[...], preferred_element_type=F32)
    x1 = x_ref[...] + _rms(o) * g1_ref[...]
    x1_ref[...] = x1
    h2_ref[...] = (_rms(x1) * g2_ref[...]).astype(h2_ref.dtype)


def mix_out(ya, yb, yc, proj, x, w_a, w_b, w_c, w_o, g_post, g_pre, bm=256):
    r = x.shape[0]
    half = D_MODEL // 2
    g0 = GATE_COL0 // half
    gate = lambda k: pl.BlockSpec((bm, half), lambda i, k=k: (i, g0 + k))
    rowspec = lambda d: pl.BlockSpec((bm, d), lambda i: (i, 0))
    vec = pl.BlockSpec((1, D_MODEL), lambda i: (0, 0))
    return pl.pallas_call(
        _mix_out_kernel,
        grid=(r // bm,),
        in_specs=[rowspec(D_MODEL), rowspec(D_B), rowspec(D_C)] + [gate(k) for k in range(6)]
                 + [rowspec(D_MODEL), _resident((D_MODEL, D_MODEL)), _resident((D_B, D_MODEL)),
                    _resident((D_C, D_MODEL)), _resident((D_MODEL, D_MODEL)), vec, vec],
        out_specs=[rowspec(D_MODEL), rowspec(D_MODEL)],
        out_shape=[jax.ShapeDtypeStruct((r, D_MODEL), F32), jax.ShapeDtypeStruct((r, D_MODEL), BF16)],
        compiler_params=_cparams("parallel"),
        name="mix_out",
    )(ya, yb, yc, proj, proj, proj, proj, proj, proj, x, w_a, w_b, w_c, w_o,
      g_post.reshape(1, D_MODEL), g_pre.reshape(1, D_MODEL))


def _gated_act(gate, val):
    return jax.nn.gelu(gate) * val


def _ffn_up_act_kernel(a_ref, wg_ref, wv_ref, cwg_ref, cwv_ref, cbg_ref, cbv_ref,
                       o_ref, sg_ref, sv_ref, ext_ref, *, bm):
    @pl.when(pl.program_id(2) == 0)
    def _():
        ext_ref[:, 0:SUBLANES, :] = jnp.zeros((2, SUBLANES, ext_ref.shape[2]), F32)

    a = a_ref[...]

    def conv(which, w_ref, cw_ref, cb_ref, st_ref):
        u = jnp.dot(a, w_ref[...].astype(BF16), preferred_element_type=F32)
        ext_ref[which, SUBLANES:SUBLANES + bm, :] = u
        cw = cw_ref[...]
        y = cb_ref[...] + u * cw[CONV_F - 1:CONV_F, :]
        for j in range(CONV_F - 1):
            y = y + ext_ref[which, pl.ds(SUBLANES - (CONV_F - 1) + j, bm), :] * cw[j:j + 1, :]
        tail = u[bm - SUBLANES:bm, :]
        ext_ref[which, 0:SUBLANES, :] = tail
        st_ref[0] = tail
        return y

    gate = conv(0, wg_ref, cwg_ref, cbg_ref, sg_ref)
    val = conv(1, wv_ref, cwv_ref, cbv_ref, sv_ref)
    o_ref[...] = _gated_act(gate, val).astype(o_ref.dtype)


def ffn_up_act(h2, w_up, conv_w, conv_b, *, nseq, bm=1024, bn=768):
    r = h2.shape[0]
    tiles = r // nseq // bm
    ncol = D_FF // bn
    wspec = lambda off: pl.BlockSpec((D_MODEL, bn), lambda j, b, t, off=off: (0, off + j))
    cwspec = lambda off: pl.BlockSpec((CONV_F, bn), lambda j, b, t, off=off: (0, off + j))
    cbspec = lambda off: pl.BlockSpec((1, bn), lambda j, b, t, off=off: (0, off + j))
    st_spec = pl.BlockSpec((1, SUBLANES, bn), lambda j, b, t: (b, 0, j))
    kern = functools.partial(_ffn_up_act_kernel, bm=bm)
    return pl.pallas_call(
        kern,
        grid=(ncol, nseq, tiles),
        in_specs=[pl.BlockSpec((bm, D_MODEL), lambda j, b, t: (b * tiles + t, 0)),
                  wspec(0), wspec(ncol), cwspec(0), cwspec(ncol), cbspec(0), cbspec(ncol)],
        out_specs=[pl.BlockSpec((bm, bn), lambda j, b, t: (b * tiles + t, j)), st_spec, st_spec],
        out_shape=[jax.ShapeDtypeStruct((r, D_FF), BF16),
                   jax.ShapeDtypeStruct((nseq, SUBLANES, D_FF), F32),
                   jax.ShapeDtypeStruct((nseq, SUBLANES, D_FF), F32)],
        scratch_shapes=[pltpu.VMEM((2, SUBLANES + bm, bn), F32)],
        compiler_params=_cparams("parallel", "arbitrary", "arbitrary"),
        name="ffn_up_act",
    )(h2, w_up, w_up, conv_w, conv_w, conv_b, conv_b)


def _ffn_act_kernel(g_ref, v_ref, sg_ref, sv_ref, wg_ref, wv_ref, bg_ref, bv_ref,
                    o_ref, ng_ref, nv_ref, *, nseq, valid):
    per = SUBLANES // valid
    wg, wv, bg, bv = wg_ref[...], wv_ref[...], bg_ref[...], bv_ref[...]
    for t in range(nseq // per):
        rows = slice(t * SUBLANES, (t + 1) * SUBLANES)
        seqs = range(t * per, (t + 1) * per)
        g_tile, v_tile = g_ref[rows, :], v_ref[rows, :]
        gate = _packed_tile_conv(g_tile, [sg_ref[i] for i in seqs], wg, bg, valid)
        val = _packed_tile_conv(v_tile, [sv_ref[i] for i in seqs], wv, bv, valid)
        o_ref[rows, :] = _gated_act(gate, val).astype(o_ref.dtype)
        for j, i in enumerate(seqs):
            ng_ref[i] = _last_rows(_unpack(g_tile, j, valid), valid, CONV_F - 1)
            nv_ref[i] = _last_rows(_unpack(v_tile, j, valid), valid, CONV_F - 1)


def ffn_act(up, hist, conv_w, conv_b, *, nbatch, nseq, bc=2048):
    r = up.shape[0]
    valid = r // nbatch
    ncol = D_FF // bc
    nh = CONV_F - 1
    x_spec = lambda off: pl.BlockSpec((nseq * valid, bc), lambda g, j, off=off: (g, off + j))
    s_spec = lambda off: pl.BlockSpec((nseq, nh, bc), lambda g, j, off=off: (g, 0, off + j))
    w_spec = lambda off: pl.BlockSpec((CONV_F, bc), lambda g, j, off=off: (0, off + j))
    b_spec = lambda off: pl.BlockSpec((1, bc), lambda g, j, off=off: (0, off + j))
    kern = functools.partial(_ffn_act_kernel, nseq=nseq, valid=valid)
    return pl.pallas_call(
        kern,
        grid=(nbatch // nseq, ncol),
        in_specs=[x_spec(0), x_spec(ncol), s_spec(0), s_spec(ncol), w_spec(0), w_spec(ncol),
                  b_spec(0), b_spec(ncol)],
        out_specs=[pl.BlockSpec((nseq * valid, bc), lambda g, j: (g, j)), s_spec(0), s_spec(0)],
        out_shape=[jax.ShapeDtypeStruct((r, D_FF), F32),
                   jax.ShapeDtypeStruct((nbatch, nh, D_FF), F32),
                   jax.ShapeDtypeStruct((nbatch, nh, D_FF), F32)],
        compiler_params=_cparams("parallel", "parallel"),
        name="ffn_act",
    )(up, up, hist, hist, conv_w, conv_w, conv_b, conv_b)


def _ffn_down_kernel(a_ref, w_ref, x_ref, g_ref, y_ref):
    f = jnp.dot(a_ref[...].astype(BF16), w_ref[...], preferred_element_type=F32)
    y_ref[...] = x_ref[...] + _rms(f) * g_ref[...]


def ffn_down_post(act, w_down, x1, g, bm=256):
    r = x1.shape[0]
    return pl.pallas_call(
        _ffn_down_kernel,
        grid=(r // bm,),
        in_specs=[pl.BlockSpec((bm, D_FF), lambda i: (i, 0)), _resident((D_FF, D_MODEL)),
                  pl.BlockSpec((bm, D_MODEL), lambda i: (i, 0)),
                  pl.BlockSpec((1, D_MODEL), lambda i: (0, 0))],
        out_specs=pl.BlockSpec((bm, D_MODEL), lambda i: (i, 0)),
        out_shape=jax.ShapeDtypeStruct((r, D_MODEL), F32),
        compiler_params=_cparams("parallel"),
        name="ffn_down_post",
    )(act, w_down, x1, g.reshape(1, D_MODEL))


def _blockdiag_tiles(w):
    nblk = D_MODEL // BD_BLOCK
    rows = w.reshape(nblk, BD_BLOCK, QKV_BLOCK)
    tiled = jnp.tile(rows, (1, 1, BD_BLOCK // QKV_BLOCK))
    idx = jnp.arange(BD_BLOCK) // QKV_BLOCK
    same_block = idx[:, None] == idx[None, :]
    return jnp.where(same_block[None], tiled, 0.0).astype(BF16)


def _prep_params(conv_a_w, conv_a_b, w_q_a, w_k_a, w_v_a, w_if_a, b_if_a, norm_h_a, skip_a):
    pad = GATE_LANES - 2 * H_A
    return {
        "conv_a_w": conv_a_w,
        "conv_a_b": conv_a_b.reshape(1, D_MODEL),
        "wq_bd": _blockdiag_tiles(w_q_a),
        "wk_bd": _blockdiag_tiles(w_k_a),
        "wv_bd": _blockdiag_tiles(w_v_a),
        "w_if": jnp.pad(w_if_a, ((0, 0), (0, pad))).astype(BF16),
        "b_if": jnp.pad(b_if_a, (0, pad)).reshape(1, GATE_LANES),
        "norm_h_a": norm_h_a.reshape(1, D_MODEL),
        "skip_a": skip_a.reshape(1, D_MODEL),
    }


def _trunk(x, mem, states, w, pa, *, nseq, seq):
    prompt = states is None
    k_arr, v_arr, k_col, v_col = mem

    h = rms_cast(x, w["norm_mix_pre"])
    proj = matmul(h, w["w_in"], 1024, 1024)

    yb, vn = gmlp(proj, w["norm_v_b"], w["w_s_b"], w["b_s_b"], seq, emit_vn=not prompt)
    ya, c_new, n_new, m_new, conv_new = branch_a(proj, None if prompt else states[:4], pa,
                                                 nseq=nseq, seq=seq,
                                                 out_dtype=BF16 if prompt else F32)
    if prompt:
        yc = xattn(proj, k_arr, v_arr, k_col, v_col, nbatch=nseq, nseq=1, rq=1024, out_dtype=BF16)
        conv_new = proj.reshape(nseq, seq, N_IN)[:, seq - (CONV_A - 1):, :D_MODEL]
    else:
        yc = xattn(proj, k_arr, v_arr, k_col, v_col, nbatch=nseq, nseq=8, rq=SUBLANES, out_dtype=F32)

    x1, h2 = mix_out(ya, yb, yc, proj, x, w["w_br_a"], w["w_br_b"], w["w_br_c"], w["w_out"],
                     w["norm_mix_post"], w["norm_ffn_pre"])

    conv_b = w["conv_ffn_b"].reshape(1, 2 * D_FF)
    if prompt:
        act, tail_g, tail_v = ffn_up_act(h2, w["w_ffn_up"], w["conv_ffn_w"], conv_b, nseq=nseq)
        ffn_new = jnp.concatenate([tail_g, tail_v], axis=-1)[:, SUBLANES - (CONV_F - 1):]
    else:
        up = matmul(h2, w["w_ffn_up"], 1024, 1024)
        act, new_g, new_v = ffn_act(up, states[4], w["conv_ffn_w"], conv_b, nbatch=nseq, nseq=32)
        ffn_new = jnp.concatenate([new_g, new_v], axis=-1)
    y = ffn_down_post(act, w["w_ffn_down"], x1, w["norm_ffn_post"])
    return y, conv_new, ffn_new, vn, c_new, n_new, m_new


def kernel(x_prompt, x_sample, state_conv_a, state_mlstm_c, state_mlstm_n, state_mlstm_m, cache_mem_k, cache_mem_v, state_ffn_conv, mem_prompt, norm_mix_pre, w_in, conv_a_w, conv_a_b, w_q_a, w_k_a, w_v_a, w_if_a, b_if_a, norm_h_a, skip_a, w_br_a, norm_v_b, w_s_b, b_s_b, w_br_b, norm_mem, w_mem_kv, w_br_c, w_out, norm_mix_post, norm_ffn_pre, w_ffn_up, conv_ffn_w, conv_ffn_b, w_ffn_down, norm_ffn_post):
    nb, seq, _ = x_prompt.shape
    db, dseq, _ = x_sample.shape
    w = {
        "norm_mix_pre": norm_mix_pre[0], "w_in": w_in[0], "norm_v_b": norm_v_b[0],
        "w_s_b": w_s_b[0], "b_s_b": b_s_b[0],
        "w_br_a": w_br_a[0].astype(BF16), "w_br_b": w_br_b[0].astype(BF16),
        "w_br_c": w_br_c[0].astype(BF16), "w_out": w_out[0].astype(BF16),
        "norm_mix_post": norm_mix_post[0], "norm_ffn_pre": norm_ffn_pre[0], "w_ffn_up": w_ffn_up[0],
        "conv_ffn_w": conv_ffn_w[0], "conv_ffn_b": conv_ffn_b[0],
        "w_ffn_down": w_ffn_down[0].astype(BF16), "norm_ffn_post": norm_ffn_post[0],
    }
    pa = _prep_params(conv_a_w[0], conv_a_b[0], w_q_a[0], w_k_a[0], w_v_a[0], w_if_a[0], b_if_a[0],
                      norm_h_a[0], skip_a[0])

    mem_h = rms_cast(mem_prompt.reshape(nb * MEM_LEN, D_MODEL), norm_mem[0])
    mem_k = matmul(mem_h, w_mem_kv[0], 1024, 1024, 0, D_C)
    mem_v = matmul(mem_h, w_mem_kv[0], 1024, 1024, D_C, D_C)
    yp, conv_p, ffn_p, _, c_p, n_p, m_p = _trunk(
        x_prompt.reshape(nb * seq, D_MODEL), (mem_k, mem_v, 0, 0), None, w, pa, nseq=nb, seq=seq)

    m0 = jnp.pad(state_mlstm_m[0][:, None, :], ((0, 0), (0, SUBLANES - 1), (0, GATE_LANES - H_A)))
    ys, conv_s, ffn_s, vn_s, c_s, n_s, m_s = _trunk(
        x_sample.reshape(db * dseq, D_MODEL), (cache_mem_k[0], cache_mem_v[0], 0, 0),
        (state_conv_a[0], state_mlstm_c[0], state_mlstm_n[0], m0, state_ffn_conv[0]),
        w, pa, nseq=db, seq=dseq)

    return (
        yp.reshape(nb, seq, D_MODEL), ys.reshape(db, dseq, D_MODEL),
        conv_p[None], c_p[None], n_p[None], m_p[None],
        mem_k.reshape(1, nb, MEM_LEN, H_C, DH_C), mem_v.reshape(1, nb, MEM_LEN, H_C, DH_C),
        ffn_p[None],
        conv_s[None], c_s[None], n_s[None], m_s[None],
        vn_s.reshape(1, db, dseq, D_B), ffn_s[None],
    )
```

```python
import functools

import jax
import jax.numpy as jnp
from jax import lax
from jax.experimental import pallas as pl
from jax.experimental.pallas import tpu as pltpu

F32 = jnp.float32
BF16 = jnp.bfloat16

D_MODEL = 2048
H_A = 4
DH_A = 512
QKV_BLOCK = 4
CONV_A = 4
CONV_F = 3
D_B = 1024
G_B = 4
DG_B = 256
H_C = 4
DH_C = 256
D_C = 1024
MEM_LEN = 256
D_FF = 6144
N_IN = 13312
GATE_COL0 = 2 * D_MODEL + 2 * D_B + D_C
EPS = 1e-6
NEG_INF = -1e30

CHUNK = 128
BD_BLOCK = 256
GATE_LANES = 128
SUBLANES = 8
C_RING = 3
V7X_VMEM_BYTES = 64 * 1024 * 1024
VMEM_LIMIT = V7X_VMEM_BYTES - 8 * 1024 * 1024


def _cparams(*sem):
    return pltpu.CompilerParams(dimension_semantics=sem, vmem_limit_bytes=VMEM_LIMIT)


def _resident(shape):
    return pl.BlockSpec(shape, lambda *_: (0,) * len(shape), pipeline_mode=pl.Buffered(1))


def _rms(x):
    return x * lax.rsqrt(jnp.mean(x * x, axis=-1, keepdims=True) + EPS)


def _unpack(tile, i, valid):
    return tile if i == 0 else pltpu.roll(tile, SUBLANES - i * valid, axis=0)


def _packed_tile_conv(tile, hists, w, b, valid):
    nh = hists[0].shape[0]
    row = lax.broadcasted_iota(jnp.int32, (SUBLANES, 1), 0)
    y = b + tile * w[nh:nh + 1, :]
    for j in range(nh):
        s = nh - j
        tap = pltpu.roll(tile, s, axis=0)
        for i, hist in enumerate(hists):
            for t in range(min(s, valid)):
                tap = jnp.where(row == i * valid + t, hist[nh - s + t:nh - s + t + 1, :], tap)
        y = y + tap * w[j:j + 1, :]
    return y


def _last_rows(x, valid, n):
    return pltpu.roll(x, SUBLANES - (valid - n), axis=0)[0:n]


def _pack(parts, valid):
    row = lax.broadcasted_iota(jnp.int32, (SUBLANES, 1), 0)
    out = parts[0]
    for i in range(1, len(parts)):
        out = jnp.where(row < i * valid, out, pltpu.roll(parts[i], i * valid, axis=0))
    return out


def _rms_kernel(x_ref, g_ref, o_ref):
    o_ref[...] = (_rms(x_ref[...]) * g_ref[...]).astype(o_ref.dtype)


def rms_cast(x, g, tile=1024):
    r, d = x.shape
    tile = min(tile, r)
    return pl.pallas_call(
        _rms_kernel,
        grid=(r // tile,),
        in_specs=[pl.BlockSpec((tile, d), lambda i: (i, 0)),
                  pl.BlockSpec((1, d), lambda i: (0, 0))],
        out_specs=pl.BlockSpec((tile, d), lambda i: (i, 0)),
        out_shape=jax.ShapeDtypeStruct((r, d), BF16),
        compiler_params=_cparams("parallel"),
        name="rms_cast",
    )(x, g.reshape(1, d))


def _mm_kernel(a_ref, w_ref, o_ref):
    o_ref[...] = jnp.dot(a_ref[...].astype(BF16), w_ref[...].astype(BF16),
                         preferred_element_type=F32)


def matmul(a, w, bm, bn, col0=0, n=None):
    r, k = a.shape
    n = w.shape[1] if n is None else n
    bm = min(bm, r)
    j0 = col0 // bn
    return pl.pallas_call(
        _mm_kernel,
        grid=(n // bn, r // bm),
        in_specs=[pl.BlockSpec((bm, k), lambda j, i: (i, 0)),
                  pl.BlockSpec((k, bn), lambda j, i: (0, j0 + j))],
        out_specs=pl.BlockSpec((bm, bn), lambda j, i: (i, j)),
        out_shape=jax.ShapeDtypeStruct((r, n), F32),
        compiler_params=_cparams("parallel", "arbitrary"),
        name="matmul",
    )(a, w)


def _log_sigmoid(x):
    return jnp.minimum(x, 0.0) - jnp.log1p(jnp.exp(-jnp.abs(x)))


def _layer_norm_rows(x):
    xc = x - jnp.mean(x, axis=-1, keepdims=True)
    return xc * lax.rsqrt(jnp.mean(xc * xc, axis=-1, keepdims=True) + EPS)


def _branch_a_kernel(*refs, nsub, tq, valid, nchunk, zero_state):
    n_in = 3 if zero_state else 6
    xm_ref, z_ref, st_ref = refs[:3]
    (cw_ref, cb_ref, wq_ref, wk_ref, wv_ref, wif_ref, bif_ref, nh_ref, sk_ref,
     ya_ref, c_ref, n_ref, m_ref) = refs[n_in:n_in + 13]
    lc = CHUNK
    packed = valid < tq
    carried = nchunk > 1
    chunk = pl.program_id(1)
    cs_ref, ext_ref = (refs[n_in + 13], None) if packed else (None, refs[n_in + 13])

    if carried:
        @pl.when(chunk == 0)
        def _init():
            ext_ref[...] = st_ref[...]
            if zero_state:
                c_ref[...] = jnp.zeros(c_ref.shape, F32)
                n_ref[...] = jnp.zeros(n_ref.shape, F32)
                m_ref[...] = jnp.zeros(m_ref.shape, F32)
            else:
                c_ref[...] = refs[3][...]
                n_ref[...] = refs[4][...]
                m_ref[...] = refs[5][...]
        nin_ref, min_ref = n_ref, m_ref
        load_c = lambda i, h: c_ref[i, h]
    else:
        c_hbm, nin_ref, min_ref = refs[3:6]
        ring_ref, sem_ref = refs[n_in + 14:n_in + 16]
        step, nstep = pl.program_id(0), pl.num_programs(0)

        def fetch(s):
            slot = s % C_RING
            return pltpu.make_async_copy(c_hbm.at[pl.ds(s * nsub, nsub)], ring_ref.at[slot],
                                         sem_ref.at[slot])

        @pl.when(step == 0)
        def _prime():
            fetch(0).start()

            @pl.when(nstep > 1)
            def _():
                fetch(1).start()

        @pl.when(step + 2 < nstep)
        def _ahead():
            fetch(step + 2).start()

        fetch(step).wait()
        load_c = lambda i, h: ring_ref[step % C_RING, i, h]

    lane = lax.broadcasted_iota(jnp.int32, (1, GATE_LANES), 1)
    r = lax.broadcasted_iota(jnp.int32, (tq, lc), 0)
    s_idx = lax.broadcasted_iota(jnp.int32, (tq, lc), 1)
    tril = s_idx <= r
    eye = s_idx == r
    row_valid = lax.broadcasted_iota(jnp.int32, (tq, 1), 0) < valid
    lane_valid = lax.broadcasted_iota(jnp.int32, (1, lc), 1) < valid
    scale = DH_A ** -0.5
    cw = cw_ref[...]

    def blockdiag(xb, w_ref):
        nblk = D_MODEL // BD_BLOCK
        return jnp.concatenate(
            [jnp.dot(xb[:, j * BD_BLOCK:(j + 1) * BD_BLOCK], w_ref[j], preferred_element_type=F32)
             for j in range(nblk)], axis=1)

    def pad_rows(x):
        if tq == lc:
            return x
        return jnp.concatenate([x, jnp.zeros((lc - tq, x.shape[1]), x.dtype)], axis=0)

    def front(x_in, y):
        xconv = y * jax.nn.sigmoid(y)
        xcb = xconv.astype(BF16)
        q32 = blockdiag(xcb, wq_ref)
        k32 = blockdiag(xcb, wk_ref)
        v32 = blockdiag(x_in.astype(BF16), wv_ref)
        if_pre = (jnp.dot(q32.astype(BF16), wif_ref[0:D_MODEL, :], preferred_element_type=F32)
                  + jnp.dot(k32.astype(BF16), wif_ref[D_MODEL:2 * D_MODEL, :], preferred_element_type=F32)
                  + jnp.dot(v32.astype(BF16), wif_ref[2 * D_MODEL:3 * D_MODEL, :], preferred_element_type=F32)
                  + bif_ref[...])
        return xconv, q32, k32, v32, if_pre, _log_sigmoid(if_pre)

    if packed:
        tile = xm_ref[0]
        hists = [st_ref[i] for i in range(nsub)]
        shared = front(tile, _packed_tile_conv(tile, hists, cw, cb_ref[...], valid))

    packed_out = []
    for i in range(nsub):
        if packed:
            x_in = _unpack(tile, i, valid)
            z_in = _unpack(z_ref[0], i, valid)
            cs_ref[i] = _last_rows(x_in, valid, CONV_A - 1)
            xconv, q32, k32, v32, if_pre, log_f_all = (_unpack(a, i, valid) for a in shared)
        else:
            x_in = xm_ref[i]
            hist = ext_ref[i] if carried else st_ref[i]
            xe = jnp.concatenate([hist, x_in], axis=0)
            y = cb_ref[...] + x_in * cw[CONV_A - 1:CONV_A, :]
            for j in range(CONV_A - 1):
                shifted = pltpu.roll(xe, CONV_A - 1 - j, axis=0)[SUBLANES:, :]
                y = y + shifted * cw[j:j + 1, :]
            if carried:
                ext_ref[i] = x_in[tq - SUBLANES:tq, :]
            xconv, q32, k32, v32, if_pre, log_f_all = front(x_in, y)
        qb = q32.astype(BF16)
        kpad = pad_rows(k32).astype(BF16)
        vpad = pad_rows(v32).astype(BF16)

        m_all = min_ref[i, 0:1, :]
        m_out = m_all
        heads = []
        for h in range(H_A):
            hs = slice(h * DH_A, (h + 1) * DH_A)
            qh, kh, vh = qb[:, hs], kpad[:, hs], vpad[:, hs]
            li_col = jnp.sum(jnp.where(lane == h, if_pre, 0.0), axis=1, keepdims=True)
            lf_col = jnp.sum(jnp.where(lane == H_A + h, log_f_all, 0.0), axis=1, keepdims=True)
            if packed:
                li_col = jnp.where(row_valid, li_col, NEG_INF)
                lf_col = jnp.where(row_valid, lf_col, 0.0)
            m_prev = jnp.sum(jnp.where(lane == h, m_all, 0.0), axis=1, keepdims=True)

            b_row = jnp.sum(jnp.where(r <= s_idx, lf_col, 0.0), axis=0, keepdims=True)
            li_row = jnp.sum(jnp.where(eye, li_col, 0.0), axis=0, keepdims=True)
            if packed:
                li_row = jnp.where(lane_valid, li_row, NEG_INF)
            b_col = jnp.sum(jnp.where(eye, b_row, 0.0), axis=1, keepdims=True)

            d = jnp.where(tril, b_col - b_row + li_row, NEG_INF)
            inter = m_prev + b_col
            m_t = jnp.maximum(jnp.max(d, axis=1, keepdims=True), inter)
            w = jnp.exp(d - m_t)
            g = jnp.exp(inter - m_t)
            qk = lax.dot_general(qh, kh, (((1,), (1,)), ((), ())), preferred_element_type=F32)
            s = qk * (w * scale)
            c_h = load_c(i, h)
            n_h = nin_ref[i, h:h + 1, :]
            num = (g * jnp.dot(qh, c_h.astype(BF16), preferred_element_type=F32)
                   + jnp.dot(s.astype(BF16), vh, preferred_element_type=F32))
            den = (g * jnp.sum(q32[:, hs] * n_h, axis=1, keepdims=True)
                   + jnp.sum(s, axis=1, keepdims=True))
            ht = num / jnp.maximum(jnp.abs(den), jnp.exp(-m_t))

            b_last = jnp.sum(lf_col, axis=0, keepdims=True)
            a_col = b_last - b_col + li_col
            m_new = jnp.maximum(m_prev + b_last, jnp.max(a_col, axis=0, keepdims=True))
            decay = jnp.exp(m_prev + b_last - m_new)
            kw = k32[:, hs] * (jnp.exp(a_col - m_new) * scale)
            kwt = jnp.transpose(pad_rows(kw)).astype(BF16)
            c_ref[i, h] = decay * c_h + jnp.dot(kwt, vh, preferred_element_type=F32)
            n_ref[i, h:h + 1, :] = decay * n_h + jnp.sum(kw, axis=0, keepdims=True)
            m_out = jnp.where(lane == h, m_new, m_out)

            hn = _layer_norm_rows(ht) * nh_ref[:, hs]
            z_h = z_in[:, hs] if packed else z_ref[i, :, hs]
            ya_h = jax.nn.sigmoid(z_h) * (hn + sk_ref[:, hs] * xconv[:, hs])
            if packed:
                heads.append(ya_h)
            else:
                ya_ref[i, :, hs] = ya_h.astype(ya_ref.dtype)
        m_ref[i] = jnp.broadcast_to(m_out, (SUBLANES, GATE_LANES))
        if packed:
            packed_out.append(heads)
    if packed:
        for h in range(H_A):
            tile = _pack([heads[h] for heads in packed_out], valid)
            ya_ref[0, :, h * DH_A:(h + 1) * DH_A] = tile.astype(ya_ref.dtype)


def branch_a(proj, states, p, *, nseq, seq, out_dtype):
    r = proj.shape[0]
    zero_state = states is None
    if seq >= CHUNK:
        nsub, tq, valid, nchunk = 2, CHUNK, CHUNK, seq // CHUNK
        proj3 = proj.reshape(nseq, seq, N_IN)
        row = lambda col: pl.BlockSpec((nsub, tq, D_MODEL), lambda g, c, col=col: (g, c, col))
        ya_shape = (nseq, seq, D_MODEL)
        assert zero_state
        hist = jnp.zeros((nseq, SUBLANES, D_MODEL), F32)
    else:
        nsub, tq, valid, nchunk = SUBLANES // seq, SUBLANES, seq, 1
        proj3 = proj.reshape(r // SUBLANES, SUBLANES, N_IN)
        row = lambda col: pl.BlockSpec((1, tq, D_MODEL), lambda g, c, col=col: (g, 0, col))
        ya_shape = (r // SUBLANES, SUBLANES, D_MODEL)
        assert not zero_state
        hist = states[0]
    packed = valid < tq
    full = lambda shape: pl.BlockSpec(shape, lambda g, c: (0,) * len(shape))
    c_spec = pl.BlockSpec((nsub, H_A, DH_A, DH_A), lambda g, c: (g, 0, 0, 0))
    n_spec = pl.BlockSpec((nsub, H_A, DH_A), lambda g, c: (g, 0, 0))
    m_spec = pl.BlockSpec((nsub, SUBLANES, GATE_LANES), lambda g, c: (g, 0, 0))
    hist_spec = pl.BlockSpec((nsub,) + hist.shape[1:], lambda g, c: (g, 0, 0))
    in_specs = [row(0), row(1), hist_spec]
    args = [proj3, proj3, hist]
    if not zero_state:
        in_specs += [pl.BlockSpec(memory_space=pl.ANY), n_spec, m_spec]
        args += list(states[1:])
    nblk = D_MODEL // BD_BLOCK
    in_specs += [full((CONV_A, D_MODEL)), full((1, D_MODEL)),
                 full((nblk, BD_BLOCK, BD_BLOCK)), full((nblk, BD_BLOCK, BD_BLOCK)),
                 full((nblk, BD_BLOCK, BD_BLOCK)),
                 full((3 * D_MODEL, GATE_LANES)), full((1, GATE_LANES)),
                 full((1, D_MODEL)), full((1, D_MODEL))]
    args += [p["conv_a_w"], p["conv_a_b"], p["wq_bd"], p["wk_bd"], p["wv_bd"], p["w_if"], p["b_if"],
             p["norm_h_a"], p["skip_a"]]
    kern = functools.partial(_branch_a_kernel, nsub=nsub, tq=tq, valid=valid, nchunk=nchunk,
                             zero_state=zero_state)
    out_specs = [row(0), c_spec, n_spec, m_spec]
    out_shape = [jax.ShapeDtypeStruct(ya_shape, out_dtype),
                 jax.ShapeDtypeStruct((nseq, H_A, DH_A, DH_A), F32),
                 jax.ShapeDtypeStruct((nseq, H_A, DH_A), F32),
                 jax.ShapeDtypeStruct((nseq, SUBLANES, GATE_LANES), F32)]
    scratch = []
    if packed:
        out_specs.append(hist_spec)
        out_shape.append(jax.ShapeDtypeStruct(hist.shape, F32))
        scratch += [pltpu.VMEM((C_RING, nsub, H_A, DH_A, DH_A), F32), pltpu.SemaphoreType.DMA((C_RING,))]
    else:
        scratch.append(pltpu.VMEM((nsub, SUBLANES, D_MODEL), F32))
    outs = pl.pallas_call(
        kern,
        grid=(nseq // nsub, nchunk),
        in_specs=in_specs,
        out_specs=out_specs,
        out_shape=out_shape,
        scratch_shapes=scratch,
        compiler_params=_cparams("arbitrary" if packed else "parallel", "arbitrary"),
        name="branch_a",
    )(*args)
    ya, c_new, n_new, m_new = outs[:4]
    hist_new = outs[4] if packed else None
    return ya.reshape(r, D_MODEL), c_new, n_new, m_new[:, 0, :H_A], hist_new


def _gmlp_kernel(u_ref, v_ref, nv_ref, wmix_ref, bmix_ref, yb_ref, *vn_ref):
    bias = bmix_ref[...]
    lane = lax.broadcasted_iota(jnp.int32, (1, GATE_LANES), 1)
    b_cols = [jnp.sum(jnp.where(lane == g, bias, 0.0), axis=1, keepdims=True) for g in range(G_B)]
    for c in range(u_ref.shape[0] // CHUNK):
        rows = slice(c * CHUNK, (c + 1) * CHUNK)
        vn = _layer_norm_rows(jax.nn.gelu(v_ref[rows, :])) * nv_ref[...]
        if vn_ref:
            vn_ref[0][rows, :] = vn
        vnb = vn.astype(BF16)
        mixed = [jnp.dot(wmix_ref[g], vnb[:, g * DG_B:(g + 1) * DG_B],
                         preferred_element_type=F32) + b_cols[g] for g in range(G_B)]
        yb_ref[rows, :] = (jax.nn.gelu(u_ref[rows, :])
                           * jnp.concatenate(mixed, axis=1)).astype(yb_ref.dtype)


def gmlp(proj, norm_v, w_s, b_s, seq, *, emit_vn, tile=512):
    r = proj.shape[0]
    lc = min(seq, CHUNK)
    per = CHUNK // lc
    tril = jnp.tril(jnp.ones((lc, lc), bool))
    w_small = jnp.where(tril[None], w_s[:, :lc, :lc], 0.0)
    wmix = jnp.einsum("ab,gts->gatbs", jnp.eye(per, dtype=F32), w_small)
    wmix = wmix.reshape(G_B, CHUNK, CHUNK).astype(BF16)
    bmix = jnp.pad(jnp.tile(b_s[:, :lc].T, (per, 1)), ((0, 0), (0, GATE_LANES - G_B)))
    u_col = 2 * D_MODEL // D_B
    row = pl.BlockSpec((tile, D_B), lambda i: (i, 0))
    outs = pl.pallas_call(
        _gmlp_kernel,
        grid=(r // tile,),
        in_specs=[pl.BlockSpec((tile, D_B), lambda i: (i, u_col)),
                  pl.BlockSpec((tile, D_B), lambda i: (i, u_col + 1)),
                  pl.BlockSpec((1, D_B), lambda i: (0, 0)),
                  pl.BlockSpec((G_B, CHUNK, CHUNK), lambda i: (0, 0, 0)),
                  pl.BlockSpec((CHUNK, GATE_LANES), lambda i: (0, 0))],
        out_specs=[row, row] if emit_vn else [row],
        out_shape=[jax.ShapeDtypeStruct((r, D_B), BF16)]
                  + ([jax.ShapeDtypeStruct((r, D_B), F32)] if emit_vn else []),
        compiler_params=_cparams("parallel"),
        name="gmlp",
    )(proj, proj, norm_v.reshape(1, D_B), wmix, bmix)
    return (outs[0], outs[1]) if emit_vn else (outs[0], None)


def _attend(q, k, v):
    outs = []
    for h in range(H_C):
        hs = slice(h * DH_C, (h + 1) * DH_C)
        s = lax.dot_general(q[:, hs], k[:, hs], (((1,), (1,)), ((), ())),
                            preferred_element_type=F32) * (DH_C ** -0.5)
        e = jnp.exp(s - jnp.max(s, axis=-1, keepdims=True))
        prob = e / jnp.sum(e, axis=-1, keepdims=True)
        outs.append(jnp.dot(prob.astype(BF16), v[:, hs], preferred_element_type=F32))
    return jnp.concatenate(outs, axis=1)


def _attend_flat(qs, kfs, vfs):
    nb = len(qs)
    mem = MEM_LEN * H_C
    q_all = jnp.concatenate([q[:, h * DH_C:(h + 1) * DH_C] for q in qs for h in range(H_C)],
                            axis=0).astype(BF16)
    k_all, v_all = jnp.concatenate(kfs, axis=0), jnp.concatenate(vfs, axis=0)
    s = lax.dot_general(q_all, k_all, (((1,), (1,)), ((), ())),
                        preferred_element_type=F32) * (DH_C ** -0.5)
    row = lax.broadcasted_iota(jnp.int32, s.shape, 0)
    col = lax.broadcasted_iota(jnp.int32, s.shape, 1)
    same = ((row // (H_C * SUBLANES) == col // mem) & (row // SUBLANES % H_C == col % H_C))
    s = jnp.where(same, s, NEG_INF)
    e = jnp.exp(s - jnp.max(s, axis=-1, keepdims=True))
    prob = e / jnp.sum(e, axis=-1, keepdims=True)
    o = jnp.dot(prob.astype(BF16), v_all, preferred_element_type=F32)
    return [jnp.concatenate([o[(b * H_C + h) * SUBLANES:(b * H_C + h + 1) * SUBLANES]
                             for h in range(H_C)], axis=1) for b in range(nb)]


def _xattn_kernel(q_ref, k_ref, v_ref, o_ref, *, nseq, rq, valid):
    per = rq // valid
    for t in range(nseq // per):
        tile = q_ref[t * rq:(t + 1) * rq, :]
        if per == 1:
            mem = slice(t * MEM_LEN, (t + 1) * MEM_LEN)
            o = _attend(tile.astype(BF16), k_ref[mem, :].astype(BF16), v_ref[mem, :].astype(BF16))
        else:
            seqs = range(t * per, (t + 1) * per)
            flat = lambda ref, i: ref[i].reshape(MEM_LEN * H_C, DH_C).astype(BF16)
            o = _pack(_attend_flat([_unpack(tile, j, valid) for j in range(per)],
                                   [flat(k_ref, i) for i in seqs], [flat(v_ref, i) for i in seqs]),
                      valid)
        o_ref[t * rq:(t + 1) * rq, :] = o.astype(o_ref.dtype)


def xattn(proj, k_arr, v_arr, k_col, v_col, *, nbatch, nseq, rq, out_dtype):
    r = proj.shape[0]
    seq = r // nbatch
    valid = min(seq, rq)
    rows = nseq * valid
    nq = seq // valid
    assert nseq == 1 or nq == 1
    q_col = (2 * D_MODEL + 2 * D_B) // D_C

    def mem_spec(arr, col):
        if arr.ndim == 4:
            return pl.BlockSpec((nseq, MEM_LEN, H_C, DH_C), lambda g, t: (g, 0, 0, 0))
        return pl.BlockSpec((nseq * MEM_LEN, D_C), lambda g, t: (g, col))

    kern = functools.partial(_xattn_kernel, nseq=nseq, rq=rq, valid=valid)
    return pl.pallas_call(
        kern,
        grid=(nbatch // nseq, nq),
        in_specs=[pl.BlockSpec((rows, D_C), lambda g, t: (g * nq + t, q_col)),
                  mem_spec(k_arr, k_col), mem_spec(v_arr, v_col)],
        out_specs=pl.BlockSpec((rows, D_C), lambda g, t: (g * nq + t, 0)),
        out_shape=jax.ShapeDtypeStruct((r, D_C), out_dtype),
        compiler_params=_cparams("parallel", "arbitrary"),
        name="xattn",
    )(proj, k_arr, v_arr)


def _mix_out_kernel(ya_ref, yb_ref, yc_ref, ga0, ga1, gb0, gb1, gc0, gc1, x_ref,
                    wa_ref, wb_ref, wc_ref, wo_ref, g1_ref, g2_ref, x1_ref, h2_ref):
    def branch(y_ref, w_ref, lo, hi):
        gate = jnp.concatenate([lo[...], hi[...]], axis=1)
        return jax.nn.sigmoid(gate) * jnp.dot(y_ref[...].astype(BF16), w_ref[...],
                                              preferred_element_type=F32)

    merged = (branch(ya_ref, wa_ref, ga0, ga1) + branch(yb_ref, wb_ref, gb0, gb1)
              + branch(yc_ref, wc_ref, gc0, gc1))
    o = jnp.dot(merged.astype(BF16), wo_ref[...], preferred_element_type=F32)
    x1 = x_ref[...] + _rms(o) * g1_ref[...]
    x1_ref[...] = x1
    h2_ref[...] = (_rms(x1) * g2_ref[...]).astype(h2_ref.dtype)


def mix_out(ya, yb, yc, proj, x, w_a, w_b, w_c, w_o, g_post, g_pre, bm=256):
    r = x.shape[0]
    half = D_MODEL // 2
    g0 = GATE_COL0 // half
    gate = lambda k: pl.BlockSpec((bm, half), lambda i, k=k: (i, g0 + k))
    rowspec = lambda d: pl.BlockSpec((bm, d), lambda i: (i, 0))
    vec = pl.BlockSpec((1, D_MODEL), lambda i: (0, 0))
    return pl.pallas_call(
        _mix_out_kernel,
        grid=(r // bm,),
        in_specs=[rowspec(D_MODEL), rowspec(D_B), rowspec(D_C)] + [gate(k) for k in range(6)]
                 + [rowspec(D_MODEL), _resident((D_MODEL, D_MODEL)), _resident((D_B, D_MODEL)),
                    _resident((D_C, D_MODEL)), _resident((D_MODEL, D_MODEL)), vec, vec],
        out_specs=[rowspec(D_MODEL), rowspec(D_MODEL)],
        out_shape=[jax.ShapeDtypeStruct((r, D_MODEL), F32), jax.ShapeDtypeStruct((r, D_MODEL), BF16)],
        compiler_params=_cparams("parallel"),
        name="mix_out",
    )(ya, yb, yc, proj, proj, proj, proj, proj, proj, x, w_a, w_b, w_c, w_o,
      g_post.reshape(1, D_MODEL), g_pre.reshape(1, D_MODEL))


def _gated_act(gate, val):
    return jax.nn.gelu(gate) * val


def _ffn_up_act_kernel(a_ref, wg_ref, wv_ref, cwg_ref, cwv_ref, cbg_ref, cbv_ref,
                       o_ref, sg_ref, sv_ref, ext_ref, *, bm):
    @pl.when(pl.program_id(2) == 0)
    def _():
        ext_ref[:, 0:SUBLANES, :] = jnp.zeros((2, SUBLANES, ext_ref.shape[2]), F32)

    a = a_ref[...]

    def conv(which, w_ref, cw_ref, cb_ref, st_ref):
        u = jnp.dot(a, w_ref[...].astype(BF16), preferred_element_type=F32)
        ext_ref[which, SUBLANES:SUBLANES + bm, :] = u
        cw = cw_ref[...]
        y = cb_ref[...] + u * cw[CONV_F - 1:CONV_F, :]
        for j in range(CONV_F - 1):
            y = y + ext_ref[which, pl.ds(SUBLANES - (CONV_F - 1) + j, bm), :] * cw[j:j + 1, :]
        tail = u[bm - SUBLANES:bm, :]
        ext_ref[which, 0:SUBLANES, :] = tail
        st_ref[0] = tail
        return y

    gate = conv(0, wg_ref, cwg_ref, cbg_ref, sg_ref)
    val = conv(1, wv_ref, cwv_ref, cbv_ref, sv_ref)
    o_ref[...] = _gated_act(gate, val).astype(o_ref.dtype)


def ffn_up_act(h2, w_up, conv_w, conv_b, *, nseq, bm=1024, bn=768):
    r = h2.shape[0]
    tiles = r // nseq // bm
    ncol = D_FF // bn
    wspec = lambda off: pl.BlockSpec((D_MODEL, bn), lambda j, b, t, off=off: (0, off + j))
    cwspec = lambda off: pl.BlockSpec((CONV_F, bn), lambda j, b, t, off=off: (0, off + j))
    cbspec = lambda off: pl.BlockSpec((1, bn), lambda j, b, t, off=off: (0, off + j))
    st_spec = pl.BlockSpec((1, SUBLANES, bn), lambda j, b, t: (b, 0, j))
    kern = functools.partial(_ffn_up_act_kernel, bm=bm)
    return pl.pallas_call(
        kern,
        grid=(ncol, nseq, tiles),
        in_specs=[pl.BlockSpec((bm, D_MODEL), lambda j, b, t: (b * tiles + t, 0)),
                  wspec(0), wspec(ncol), cwspec(0), cwspec(ncol), cbspec(0), cbspec(ncol)],
        out_specs=[pl.BlockSpec((bm, bn), lambda j, b, t: (b * tiles + t, j)), st_spec, st_spec],
        out_shape=[jax.ShapeDtypeStruct((r, D_FF), BF16),
                   jax.ShapeDtypeStruct((nseq, SUBLANES, D_FF), F32),
                   jax.ShapeDtypeStruct((nseq, SUBLANES, D_FF), F32)],
        scratch_shapes=[pltpu.VMEM((2, SUBLANES + bm, bn), F32)],
        compiler_params=_cparams("parallel", "arbitrary", "arbitrary"),
        name="ffn_up_act",
    )(h2, w_up, w_up, conv_w, conv_w, conv_b, conv_b)


def _ffn_act_kernel(g_ref, v_ref, sg_ref, sv_ref, wg_ref, wv_ref, bg_ref, bv_ref,
                    o_ref, ng_ref, nv_ref, *, nseq, valid):
    per = SUBLANES // valid
    wg, wv, bg, bv = wg_ref[...], wv_ref[...], bg_ref[...], bv_ref[...]
    for t in range(nseq // per):
        rows = slice(t * SUBLANES, (t + 1) * SUBLANES)
        seqs = range(t * per, (t + 1) * per)
        g_tile, v_tile = g_ref[rows, :], v_ref[rows, :]
        gate = _packed_tile_conv(g_tile, [sg_ref[i] for i in seqs], wg, bg, valid)
        val = _packed_tile_conv(v_tile, [sv_ref[i] for i in seqs], wv, bv, valid)
        o_ref[rows, :] = _gated_act(gate, val).astype(o_ref.dtype)
        for j, i in enumerate(seqs):
            ng_ref[i] = _last_rows(_unpack(g_tile, j, valid), valid, CONV_F - 1)
            nv_ref[i] = _last_rows(_unpack(v_tile, j, valid), valid, CONV_F - 1)


def ffn_act(up, hist, conv_w, conv_b, *, nbatch, nseq, bc=2048):
    r = up.shape[0]
    valid = r // nbatch
    ncol = D_FF // bc
    nh = CONV_F - 1
    x_spec = lambda off: pl.BlockSpec((nseq * valid, bc), lambda g, j, off=off: (g, off + j))
    s_spec = lambda off: pl.BlockSpec((nseq, nh, bc), lambda g, j, off=off: (g, 0, off + j))
    w_spec = lambda off: pl.BlockSpec((CONV_F, bc), lambda g, j, off=off: (0, off + j))
    b_spec = lambda off: pl.BlockSpec((1, bc), lambda g, j, off=off: (0, off + j))
    kern = functools.partial(_ffn_act_kernel, nseq=nseq, valid=valid)
    return pl.pallas_call(
        kern,
        grid=(nbatch // nseq, ncol),
        in_specs=[x_spec(0), x_spec(ncol), s_spec(0), s_spec(ncol), w_spec(0), w_spec(ncol),
                  b_spec(0), b_spec(ncol)],
        out_specs=[pl.BlockSpec((nseq * valid, bc), lambda g, j: (g, j)), s_spec(0), s_spec(0)],
        out_shape=[jax.ShapeDtypeStruct((r, D_FF), F32),
                   jax.ShapeDtypeStruct((nbatch, nh, D_FF), F32),
                   jax.ShapeDtypeStruct((nbatch, nh, D_FF), F32)],
        compiler_params=_cparams("parallel", "parallel"),
        name="ffn_act",
    )(up, up, hist, hist, conv_w, conv_w, conv_b, conv_b)


def _ffn_down_kernel(a_ref, w_ref, x_ref, g_ref, y_ref):
    f = jnp.dot(a_ref[...].astype(BF16), w_ref[...], preferred_element_type=F32)
    y_ref[...] = x_ref[...] + _rms(f) * g_ref[...]


def ffn_down_post(act, w_down, x1, g, bm=256):
    r = x1.shape[0]
    return pl.pallas_call(
        _ffn_down_kernel,
        grid=(r // bm,),
        in_specs=[pl.BlockSpec((bm, D_FF), lambda i: (i, 0)), _resident((D_FF, D_MODEL)),
                  pl.BlockSpec((bm, D_MODEL), lambda i: (i, 0)),
                  pl.BlockSpec((1, D_MODEL), lambda i: (0, 0))],
        out_specs=pl.BlockSpec((bm, D_MODEL), lambda i: (i, 0)),
        out_shape=jax.ShapeDtypeStruct((r, D_MODEL), F32),
        compiler_params=_cparams("parallel"),
        name="ffn_down_post",
    )(act, w_down, x1, g.reshape(1, D_MODEL))


def _blockdiag_tiles(w):
    nblk = D_MODEL // BD_BLOCK
    rows = w.reshape(nblk, BD_BLOCK, QKV_BLOCK)
    tiled = jnp.tile(rows, (1, 1, BD_BLOCK // QKV_BLOCK))
    idx = jnp.arange(BD_BLOCK) // QKV_BLOCK
    same_block = idx[:, None] == idx[None, :]
    return jnp.where(same_block[None], tiled, 0.0).astype(BF16)


def _prep_params(conv_a_w, conv_a_b, w_q_a, w_k_a, w_v_a, w_if_a, b_if_a, norm_h_a, skip_a):
    pad = GATE_LANES - 2 * H_A
    return {
        "conv_a_w": conv_a_w,
        "conv_a_b": conv_a_b.reshape(1, D_MODEL),
        "wq_bd": _blockdiag_tiles(w_q_a),
        "wk_bd": _blockdiag_tiles(w_k_a),
        "wv_bd": _blockdiag_tiles(w_v_a),
        "w_if": jnp.pad(w_if_a, ((0, 0), (0, pad))).astype(BF16),
        "b_if": jnp.pad(b_if_a, (0, pad)).reshape(1, GATE_LANES),
        "norm_h_a": norm_h_a.reshape(1, D_MODEL),
        "skip_a": skip_a.reshape(1, D_MODEL),
    }


def _trunk(x, mem, states, w, pa, *, nseq, seq):
    prompt = states is None
    k_arr, v_arr, k_col, v_col = mem

    h = rms_cast(x, w["norm_mix_pre"])
    proj = matmul(h, w["w_in"], 1024, 1024)

    yb, vn = gmlp(proj, w["norm_v_b"], w["w_s_b"], w["b_s_b"], seq, emit_vn=not prompt)
    ya, c_new, n_new, m_new, conv_new = branch_a(proj, None if prompt else states[:4], pa,
                                                 nseq=nseq, seq=seq,
                                                 out_dtype=BF16 if prompt else F32)
    if prompt:
        yc = xattn(proj, k_arr, v_arr, k_col, v_col, nbatch=nseq, nseq=1, rq=1024, out_dtype=BF16)
        conv_new = proj.reshape(nseq, seq, N_IN)[:, seq - (CONV_A - 1):, :D_MODEL]
    else:
        yc = xattn(proj, k_arr, v_arr, k_col, v_col, nbatch=nseq, nseq=8, rq=SUBLANES, out_dtype=F32)

    x1, h2 = mix_out(ya, yb, yc, proj, x, w["w_br_a"], w["w_br_b"], w["w_br_c"], w["w_out"],
                     w["norm_mix_post"], w["norm_ffn_pre"])

    conv_b = w["conv_ffn_b"].reshape(1, 2 * D_FF)
    if prompt:
        act, tail_g, tail_v = ffn_up_act(h2, w["w_ffn_up"], w["conv_ffn_w"], conv_b, nseq=nseq)
        ffn_new = jnp.concatenate([tail_g, tail_v], axis=-1)[:, SUBLANES - (CONV_F - 1):]
    else:
        up = matmul(h2, w["w_ffn_up"], 1024, 1024)
        act, new_g, new_v = ffn_act(up, states[4], w["conv_ffn_w"], conv_b, nbatch=nseq, nseq=32)
        ffn_new = jnp.concatenate([new_g, new_v], axis=-1)
    y = ffn_down_post(act, w["w_ffn_down"], x1, w["norm_ffn_post"])
    return y, conv_new, ffn_new, vn, c_new, n_new, m_new


def kernel(x_prompt, x_sample, state_conv_a, state_mlstm_c, state_mlstm_n, state_mlstm_m, cache_mem_k, cache_mem_v, state_ffn_conv, mem_prompt, norm_mix_pre, w_in, conv_a_w, conv_a_b, w_q_a, w_k_a, w_v_a, w_if_a, b_if_a, norm_h_a, skip_a, w_br_a, norm_v_b, w_s_b, b_s_b, w_br_b, norm_mem, w_mem_kv, w_br_c, w_out, norm_mix_post, norm_ffn_pre, w_ffn_up, conv_ffn_w, conv_ffn_b, w_ffn_down, norm_ffn_post):
    nb, seq, _ = x_prompt.shape
    db, dseq, _ = x_sample.shape
    w = {
        "norm_mix_pre": norm_mix_pre[0], "w_in": w_in[0], "norm_v_b": norm_v_b[0],
        "w_s_b": w_s_b[0], "b_s_b": b_s_b[0],
        "w_br_a": w_br_a[0].astype(BF16), "w_br_b": w_br_b[0].astype(BF16),
        "w_br_c": w_br_c[0].astype(BF16), "w_out": w_out[0].astype(BF16),
        "norm_mix_post": norm_mix_post[0], "norm_ffn_pre": norm_ffn_pre[0], "w_ffn_up": w_ffn_up[0],
        "conv_ffn_w": conv_ffn_w[0], "conv_ffn_b": conv_ffn_b[0],
        "w_ffn_down": w_ffn_down[0].astype(BF16), "norm_ffn_post": norm_ffn_post[0],
    }
    pa = _prep_params(conv_a_w[0], conv_a_b[0], w_q_a[0], w_k_a[0], w_v_a[0], w_if_a[0], b_if_a[0],
                      norm_h_a[0], skip_a[0])

    mem_h = rms_cast(mem_prompt.reshape(nb * MEM_LEN, D_MODEL), norm_mem[0])
    mem_k = matmul(mem_h, w_mem_kv[0], 1024, 1024, 0, D_C)
    mem_v = matmul(mem_h, w_mem_kv[0], 1024, 1024, D_C, D_C)
    yp, conv_p, ffn_p, _, c_p, n_p, m_p = _trunk(
        x_prompt.reshape(nb * seq, D_MODEL), (mem_k, mem_v, 0, 0), None, w, pa, nseq=nb, seq=seq)

    m0 = jnp.pad(state_mlstm_m[0][:, None, :], ((0, 0), (0, SUBLANES - 1), (0, GATE_LANES - H_A)))
    ys, conv_s, ffn_s, vn_s, c_s, n_s, m_s = _trunk(
        x_sample.reshape(db * dseq, D_MODEL), (cache_mem_k[0], cache_mem_v[0], 0, 0),
        (state_conv_a[0], state_mlstm_c[0], state_mlstm_n[0], m0, state_ffn_conv[0]),
        w, pa, nseq=db, seq=dseq)

    return (
        yp.reshape(nb, seq, D_MODEL), ys.reshape(db, dseq, D_MODEL),
        conv_p[None], c_p[None], n_p[None], m_p[None],
        mem_k.reshape(1, nb, MEM_LEN, H_C, DH_C), mem_v.reshape(1, nb, MEM_LEN, H_C, DH_C),
        ffn_p[None],
        conv_s[None], c_s[None], n_s[None], m_s[None],
        vn_s.reshape(1, db, dseq, D_B), ffn_s[None],
    )
```

```python
import functools

import jax
import jax.numpy as jnp
from jax import lax
from jax.experimental import pallas as pl
from jax.experimental.pallas import tpu as pltpu

F32 = jnp.float32
BF16 = jnp.bfloat16

D_MODEL = 2048
H_A = 4
DH_A = 512
QKV_BLOCK = 4
CONV_A = 4
CONV_F = 3
D_B = 1024
G_B = 4
DG_B = 256
H_C = 4
DH_C = 256
D_C = 1024
MEM_LEN = 256
D_FF = 6144
N_IN = 13312
GATE_COL0 = 2 * D_MODEL + 2 * D_B + D_C
EPS = 1e-6
NEG_INF = -1e30

CHUNK = 128
BD_BLOCK = 256
GATE_LANES = 128
SUBLANES = 8
C_RING = 3
V7X_VMEM_BYTES = 64 * 1024 * 1024
VMEM_LIMIT = V7X_VMEM_BYTES - 8 * 1024 * 1024


def _cparams(*sem):
    return pltpu.CompilerParams(dimension_semantics=sem, vmem_limit_bytes=VMEM_LIMIT)


def _resident(shape):
    return pl.BlockSpec(shape, lambda *_: (0,) * len(shape), pipeline_mode=pl.Buffered(1))


def _rms(x):
    return x * lax.rsqrt(jnp.mean(x * x, axis=-1, keepdims=True) + EPS)


def _unpack(tile, i, valid):
    return tile if i == 0 else pltpu.roll(tile, SUBLANES - i * valid, axis=0)


def _packed_tile_conv(tile, hists, w, b, valid):
    nh = hists[0].shape[0]
    row = lax.broadcasted_iota(jnp.int32, (SUBLANES, 1), 0)
    y = b + tile * w[nh:nh + 1, :]
    for j in range(nh):
        s = nh - j
        tap = pltpu.roll(tile, s, axis=0)
        for i, hist in enumerate(hists):
            for t in range(min(s, valid)):
                tap = jnp.where(row == i * valid + t, hist[nh - s + t:nh - s + t + 1, :], tap)
        y = y + tap * w[j:j + 1, :]
    return y


def _last_rows(x, valid, n):
    return pltpu.roll(x, SUBLANES - (valid - n), axis=0)[0:n]


def _pack(parts, valid):
    row = lax.broadcasted_iota(jnp.int32, (SUBLANES, 1), 0)
    out = parts[0]
    for i in range(1, len(parts)):
        out = jnp.where(row < i * valid, out, pltpu.roll(parts[i], i * valid, axis=0))
    return out


def _rms_kernel(x_ref, g_ref, o_ref):
    o_ref[...] = (_rms(x_ref[...]) * g_ref[...]).astype(o_ref.dtype)


def rms_cast(x, g, tile=1024):
    r, d = x.shape
    tile = min(tile, r)
    return pl.pallas_call(
        _rms_kernel,
        grid=(r // tile,),
        in_specs=[pl.BlockSpec((tile, d), lambda i: (i, 0)),
                  pl.BlockSpec((1, d), lambda i: (0, 0))],
        out_specs=pl.BlockSpec((tile, d), lambda i: (i, 0)),
        out_shape=jax.ShapeDtypeStruct((r, d), BF16),
        compiler_params=_cparams("parallel"),
        name="rms_cast",
    )(x, g.reshape(1, d))


def _mm_kernel(a_ref, w_ref, o_ref):
    o_ref[...] = jnp.dot(a_ref[...].astype(BF16), w_ref[...].astype(BF16),
                         preferred_element_type=F32)


def matmul(a, w, bm, bn, col0=0, n=None):
    r, k = a.shape
    n = w.shape[1] if n is None else n
    bm = min(bm, r)
    j0 = col0 // bn
    return pl.pallas_call(
        _mm_kernel,
        grid=(n // bn, r // bm),
        in_specs=[pl.BlockSpec((bm, k), lambda j, i: (i, 0)),
                  pl.BlockSpec((k, bn), lambda j, i: (0, j0 + j))],
        out_specs=pl.BlockSpec((bm, bn), lambda j, i: (i, j)),
        out_shape=jax.ShapeDtypeStruct((r, n), F32),
        compiler_params=_cparams("parallel", "arbitrary"),
        name="matmul",
    )(a, w)


def _mm_pair_kernel(a_ref, w_ref, k_ref, v_ref):
    y = jnp.dot(a_ref[...], w_ref[...].astype(BF16), preferred_element_type=F32)

    @pl.when(pl.program_id(0) == 0)
    def _():
        k_ref[...] = y

    @pl.when(pl.program_id(0) == 1)
    def _():
        v_ref[...] = y


def matmul_pair(a, w):
    r, k = a.shape
    n = w.shape[1] // 2
    out = pl.BlockSpec((r, n), lambda j: (0, 0))
    return pl.pallas_call(
        _mm_pair_kernel,
        grid=(2,),
        in_specs=[_resident((r, k)),
                  pl.BlockSpec((k, n), lambda j: (0, j))],
        out_specs=[out, out],
        out_shape=[jax.ShapeDtypeStruct((r, n), F32)] * 2,
        compiler_params=_cparams("arbitrary"),
        name="matmul_pair",
    )(a, w)


def _log_sigmoid(x):
    return jnp.minimum(x, 0.0) - jnp.log1p(jnp.exp(-jnp.abs(x)))


def _layer_norm_rows(x):
    xc = x - jnp.mean(x, axis=-1, keepdims=True)
    return xc * lax.rsqrt(jnp.mean(xc * xc, axis=-1, keepdims=True) + EPS)


def _branch_a_kernel(*refs, nsub, tq, valid, nchunk, zero_state):
    n_in = 3 if zero_state else 6
    xm_ref, z_ref, st_ref = refs[:3]
    (cw_ref, cb_ref, wq_ref, wk_ref, wv_ref, wif_ref, bif_ref, nh_ref, sk_ref,
     ya_ref, c_ref, n_ref, m_ref) = refs[n_in:n_in + 13]
    lc = CHUNK
    packed = valid < tq
    carried = nchunk > 1
    chunk = pl.program_id(1)
    cs_ref, ext_ref = (refs[n_in + 13], None) if packed else (None, refs[n_in + 13])

    if carried:
        @pl.when(chunk == 0)
        def _init():
            ext_ref[...] = st_ref[...]
            if zero_state:
                c_ref[...] = jnp.zeros(c_ref.shape, F32)
                n_ref[...] = jnp.zeros(n_ref.shape, F32)
                m_ref[...] = jnp.zeros(m_ref.shape, F32)
            else:
                c_ref[...] = refs[3][...]
                n_ref[...] = refs[4][...]
                m_ref[...] = refs[5][...]
        nin_ref, min_ref = n_ref, m_ref
        load_c = lambda i, h: c_ref[i, h]
    else:
        c_hbm, nin_ref, min_ref = refs[3:6]
        ring_ref, sem_ref = refs[n_in + 14:n_in + 16]
        step, nstep = pl.program_id(0), pl.num_programs(0)

        def fetch(s):
            slot = s % C_RING
            return pltpu.make_async_copy(c_hbm.at[pl.ds(s * nsub, nsub)], ring_ref.at[slot],
                                         sem_ref.at[slot])

        @pl.when(step == 0)
        def _prime():
            fetch(0).start()

            @pl.when(nstep > 1)
            def _():
                fetch(1).start()

        @pl.when(step + 2 < nstep)
        def _ahead():
            fetch(step + 2).start()

        fetch(step).wait()
        load_c = lambda i, h: ring_ref[step % C_RING, i, h]

    lane = lax.broadcasted_iota(jnp.int32, (1, GATE_LANES), 1)
    r = lax.broadcasted_iota(jnp.int32, (tq, lc), 0)
    s_idx = lax.broadcasted_iota(jnp.int32, (tq, lc), 1)
    tril = s_idx <= r
    eye = s_idx == r
    row_valid = lax.broadcasted_iota(jnp.int32, (tq, 1), 0) < valid
    lane_valid = lax.broadcasted_iota(jnp.int32, (1, lc), 1) < valid
    scale = DH_A ** -0.5
    cw = cw_ref[...]

    def blockdiag(xb, w_ref):
        nblk = D_MODEL // BD_BLOCK
        return jnp.concatenate(
            [jnp.dot(xb[:, j * BD_BLOCK:(j + 1) * BD_BLOCK], w_ref[j], preferred_element_type=F32)
             for j in range(nblk)], axis=1)

    def pad_rows(x):
        if tq == lc:
            return x
        return jnp.concatenate([x, jnp.zeros((lc - tq, x.shape[1]), x.dtype)], axis=0)

    def front(x_in, y):
        xconv = y * jax.nn.sigmoid(y)
        xcb = xconv.astype(BF16)
        q32 = blockdiag(xcb, wq_ref)
        k32 = blockdiag(xcb, wk_ref)
        v32 = blockdiag(x_in.astype(BF16), wv_ref)
        if_pre = (jnp.dot(q32.astype(BF16), wif_ref[0:D_MODEL, :], preferred_element_type=F32)
                  + jnp.dot(k32.astype(BF16), wif_ref[D_MODEL:2 * D_MODEL, :], preferred_element_type=F32)
                  + jnp.dot(v32.astype(BF16), wif_ref[2 * D_MODEL:3 * D_MODEL, :], preferred_element_type=F32)
                  + bif_ref[...])
        return xconv, q32, k32, v32, if_pre, _log_sigmoid(if_pre)

    if packed:
        tile = xm_ref[0]
        hists = [st_ref[i] for i in range(nsub)]
        shared = front(tile, _packed_tile_conv(tile, hists, cw, cb_ref[...], valid))

    packed_out = []
    for i in range(nsub):
        if packed:
            x_in = _unpack(tile, i, valid)
            z_in = _unpack(z_ref[0], i, valid)
            cs_ref[i] = _last_rows(x_in, valid, CONV_A - 1)
            xconv, q32, k32, v32, if_pre, log_f_all = (_unpack(a, i, valid) for a in shared)
        else:
            x_in = xm_ref[i]
            hist = ext_ref[i] if carried else st_ref[i]
            xe = jnp.concatenate([hist, x_in], axis=0)
            y = cb_ref[...] + x_in * cw[CONV_A - 1:CONV_A, :]
            for j in range(CONV_A - 1):
                shifted = pltpu.roll(xe, CONV_A - 1 - j, axis=0)[SUBLANES:, :]
                y = y + shifted * cw[j:j + 1, :]
            if carried:
                ext_ref[i] = x_in[tq - SUBLANES:tq, :]
            xconv, q32, k32, v32, if_pre, log_f_all = front(x_in, y)
        qb = q32.astype(BF16)
        kpad = pad_rows(k32).astype(BF16)
        vpad = pad_rows(v32).astype(BF16)

        m_all = min_ref[i, 0:1, :]
        m_out = m_all
        heads = []
        for h in range(H_A):
            hs = slice(h * DH_A, (h + 1) * DH_A)
            qh, kh, vh = qb[:, hs], kpad[:, hs], vpad[:, hs]
            li_col = jnp.sum(jnp.where(lane == h, if_pre, 0.0), axis=1, keepdims=True)
            lf_col = jnp.sum(jnp.where(lane == H_A + h, log_f_all, 0.0), axis=1, keepdims=True)
            if packed:
                li_col = jnp.where(row_valid, li_col, NEG_INF)
                lf_col = jnp.where(row_valid, lf_col, 0.0)
            m_prev = jnp.sum(jnp.where(lane == h, m_all, 0.0), axis=1, keepdims=True)

            b_row = jnp.sum(jnp.where(r <= s_idx, lf_col, 0.0), axis=0, keepdims=True)
            li_row = jnp.sum(jnp.where(eye, li_col, 0.0), axis=0, keepdims=True)
            if packed:
                li_row = jnp.where(lane_valid, li_row, NEG_INF)
            b_col = jnp.sum(jnp.where(eye, b_row, 0.0), axis=1, keepdims=True)

            d = jnp.where(tril, b_col - b_row + li_row, NEG_INF)
            inter = m_prev + b_col
            m_t = jnp.maximum(jnp.max(d, axis=1, keepdims=True), inter)
            w = jnp.exp(d - m_t)
            g = jnp.exp(inter - m_t)
            qk = lax.dot_general(qh, kh, (((1,), (1,)), ((), ())), preferred_element_type=F32)
            s = qk * (w * scale)
            c_h = load_c(i, h)
            n_h = nin_ref[i, h:h + 1, :]
            num = (g * jnp.dot(qh, c_h.astype(BF16), preferred_element_type=F32)
                   + jnp.dot(s.astype(BF16), vh, preferred_element_type=F32))
            den = (g * jnp.sum(q32[:, hs] * n_h, axis=1, keepdims=True)
                   + jnp.sum(s, axis=1, keepdims=True))
            ht = num / jnp.maximum(jnp.abs(den), jnp.exp(-m_t))

            b_last = jnp.sum(lf_col, axis=0, keepdims=True)
            a_col = b_last - b_col + li_col
            m_new = jnp.maximum(m_prev + b_last, jnp.max(a_col, axis=0, keepdims=True))
            decay = jnp.exp(m_prev + b_last - m_new)
            kw = k32[:, hs] * (jnp.exp(a_col - m_new) * scale)
            kwt = jnp.transpose(pad_rows(kw)).astype(BF16)
            c_ref[i, h] = decay * c_h + jnp.dot(kwt, vh, preferred_element_type=F32)
            n_ref[i, h:h + 1, :] = decay * n_h + jnp.sum(kw, axis=0, keepdims=True)
            m_out = jnp.where(lane == h, m_new, m_out)

            hn = _layer_norm_rows(ht) * nh_ref[:, hs]
            z_h = z_in[:, hs] if packed else z_ref[i, :, hs]
            ya_h = jax.nn.sigmoid(z_h) * (hn + sk_ref[:, hs] * xconv[:, hs])
            if packed:
                heads.append(ya_h)
            else:
                ya_ref[i, :, hs] = ya_h.astype(ya_ref.dtype)
        m_ref[i] = jnp.broadcast_to(m_out, (SUBLANES, GATE_LANES))
        if packed:
            packed_out.append(heads)
    if packed:
        for h in range(H_A):
            tile = _pack([heads[h] for heads in packed_out], valid)
            ya_ref[0, :, h * DH_A:(h + 1) * DH_A] = tile.astype(ya_ref.dtype)


def branch_a(proj, states, p, *, nseq, seq, out_dtype):
    r = proj.shape[0]
    zero_state = states is None
    if seq >= CHUNK:
        nsub, tq, valid, nchunk = 2, CHUNK, CHUNK, seq // CHUNK
        proj3 = proj.reshape(nseq, seq, N_IN)
        row = lambda col: pl.BlockSpec((nsub, tq, D_MODEL), lambda g, c, col=col: (g, c, col))
        ya_shape = (nseq, seq, D_MODEL)
        assert zero_state
        hist = jnp.zeros((nseq, SUBLANES, D_MODEL), F32)
    else:
        nsub, tq, valid, nchunk = SUBLANES // seq, SUBLANES, seq, 1
        proj3 = proj.reshape(r // SUBLANES, SUBLANES, N_IN)
        row = lambda col: pl.BlockSpec((1, tq, D_MODEL), lambda g, c, col=col: (g, 0, col))
        ya_shape = (r // SUBLANES, SUBLANES, D_MODEL)
        assert not zero_state
        hist = states[0]
    packed = valid < tq
    full = lambda shape: pl.BlockSpec(shape, lambda g, c: (0,) * len(shape))
    c_spec = pl.BlockSpec((nsub, H_A, DH_A, DH_A), lambda g, c: (g, 0, 0, 0))
    n_spec = pl.BlockSpec((nsub, H_A, DH_A), lambda g, c: (g, 0, 0))
    m_spec = pl.BlockSpec((nsub, SUBLANES, GATE_LANES), lambda g, c: (g, 0, 0))
    hist_spec = pl.BlockSpec((nsub,) + hist.shape[1:], lambda g, c: (g, 0, 0))
    in_specs = [row(0), row(1), hist_spec]
    args = [proj3, proj3, hist]
    if not zero_state:
        in_specs += [pl.BlockSpec(memory_space=pl.ANY), n_spec, m_spec]
        args += list(states[1:])
    nblk = D_MODEL // BD_BLOCK
    in_specs += [full((CONV_A, D_MODEL)), full((1, D_MODEL)),
                 full((nblk, BD_BLOCK, BD_BLOCK)), full((nblk, BD_BLOCK, BD_BLOCK)),
                 full((nblk, BD_BLOCK, BD_BLOCK)),
                 full((3 * D_MODEL, GATE_LANES)), full((1, GATE_LANES)),
                 full((1, D_MODEL)), full((1, D_MODEL))]
    args += [p["conv_a_w"], p["conv_a_b"], p["wq_bd"], p["wk_bd"], p["wv_bd"], p["w_if"], p["b_if"],
             p["norm_h_a"], p["skip_a"]]
    kern = functools.partial(_branch_a_kernel, nsub=nsub, tq=tq, valid=valid, nchunk=nchunk,
                             zero_state=zero_state)
    out_specs = [row(0), c_spec, n_spec, m_spec]
    out_shape = [jax.ShapeDtypeStruct(ya_shape, out_dtype),
                 jax.ShapeDtypeStruct((nseq, H_A, DH_A, DH_A), F32),
                 jax.ShapeDtypeStruct((nseq, H_A, DH_A), F32),
                 jax.ShapeDtypeStruct((nseq, SUBLANES, GATE_LANES), F32)]
    scratch = []
    if packed:
        out_specs.append(hist_spec)
        out_shape.append(jax.ShapeDtypeStruct(hist.shape, F32))
        scratch += [pltpu.VMEM((C_RING, nsub, H_A, DH_A, DH_A), F32), pltpu.SemaphoreType.DMA((C_RING,))]
    else:
        scratch.append(pltpu.VMEM((nsub, SUBLANES, D_MODEL), F32))
    outs = pl.pallas_call(
        kern,
        grid=(nseq // nsub, nchunk),
        in_specs=in_specs,
        out_specs=out_specs,
        out_shape=out_shape,
        scratch_shapes=scratch,
        compiler_params=_cparams("arbitrary" if packed else "parallel", "arbitrary"),
        name="branch_a",
    )(*args)
    ya, c_new, n_new, m_new = outs[:4]
    hist_new = outs[4] if packed else None
    return ya.reshape(r, D_MODEL), c_new, n_new, m_new[:, 0, :H_A], hist_new


def _gmlp_kernel(u_ref, v_ref, nv_ref, wmix_ref, bmix_ref, yb_ref, *vn_ref):
    bias = bmix_ref[...]
    lane = lax.broadcasted_iota(jnp.int32, (1, GATE_LANES), 1)
    b_cols = [jnp.sum(jnp.where(lane == g, bias, 0.0), axis=1, keepdims=True) for g in range(G_B)]
    for c in range(u_ref.shape[0] // CHUNK):
        rows = slice(c * CHUNK, (c + 1) * CHUNK)
        vn = _layer_norm_rows(jax.nn.gelu(v_ref[rows, :])) * nv_ref[...]
        if vn_ref:
            vn_ref[0][rows, :] = vn
        vnb = vn.astype(BF16)
        mixed = [jnp.dot(wmix_ref[g], vnb[:, g * DG_B:(g + 1) * DG_B],
                         preferred_element_type=F32) + b_cols[g] for g in range(G_B)]
        yb_ref[rows, :] = (jax.nn.gelu(u_ref[rows, :])
                           * jnp.concatenate(mixed, axis=1)).astype(yb_ref.dtype)


def gmlp(proj, norm_v, w_s, b_s, seq, *, emit_vn, tile=512):
    r = proj.shape[0]
    lc = min(seq, CHUNK)
    per = CHUNK // lc
    tril = jnp.tril(jnp.ones((lc, lc), bool))
    w_small = jnp.where(tril[None], w_s[:, :lc, :lc], 0.0)
    wmix = jnp.einsum("ab,gts->gatbs", jnp.eye(per, dtype=F32), w_small)
    wmix = wmix.reshape(G_B, CHUNK, CHUNK).astype(BF16)
    bmix = jnp.pad(jnp.tile(b_s[:, :lc].T, (per, 1)), ((0, 0), (0, GATE_LANES - G_B)))
    u_col = 2 * D_MODEL // D_B
    row = pl.BlockSpec((tile, D_B), lambda i: (i, 0))
    outs = pl.pallas_call(
        _gmlp_kernel,
        grid=(r // tile,),
        in_specs=[pl.BlockSpec((tile, D_B), lambda i: (i, u_col)),
                  pl.BlockSpec((tile, D_B), lambda i: (i, u_col + 1)),
                  pl.BlockSpec((1, D_B), lambda i: (0, 0)),
                  pl.BlockSpec((G_B, CHUNK, CHUNK), lambda i: (0, 0, 0)),
                  pl.BlockSpec((CHUNK, GATE_LANES), lambda i: (0, 0))],
        out_specs=[row, row] if emit_vn else [row],
        out_shape=[jax.ShapeDtypeStruct((r, D_B), BF16)]
                  + ([jax.ShapeDtypeStruct((r, D_B), F32)] if emit_vn else []),
        compiler_params=_cparams("parallel"),
        name="gmlp",
    )(proj, proj, norm_v.reshape(1, D_B), wmix, bmix)
    return (outs[0], outs[1]) if emit_vn else (outs[0], None)


def _attend(q, k, v):
    outs = []
    for h in range(H_C):
        hs = slice(h * DH_C, (h + 1) * DH_C)
        s = lax.dot_general(q[:, hs], k[:, hs], (((1,), (1,)), ((), ())),
                            preferred_element_type=F32) * (DH_C ** -0.5)
        e = jnp.exp(s - jnp.max(s, axis=-1, keepdims=True))
        prob = e / jnp.sum(e, axis=-1, keepdims=True)
        outs.append(jnp.dot(prob.astype(BF16), v[:, hs], preferred_element_type=F32))
    return jnp.concatenate(outs, axis=1)


def _attend_flat(qs, kfs, vfs):
    nb = len(qs)
    mem = MEM_LEN * H_C
    q_all = jnp.concatenate([q[:, h * DH_C:(h + 1) * DH_C] for q in qs for h in range(H_C)],
                            axis=0).astype(BF16)
    k_all, v_all = jnp.concatenate(kfs, axis=0), jnp.concatenate(vfs, axis=0)
    s = lax.dot_general(q_all, k_all, (((1,), (1,)), ((), ())),
                        preferred_element_type=F32) * (DH_C ** -0.5)
    row = lax.broadcasted_iota(jnp.int32, s.shape, 0)
    col = lax.broadcasted_iota(jnp.int32, s.shape, 1)
    same = ((row // (H_C * SUBLANES) == col // mem) & (row // SUBLANES % H_C == col % H_C))
    s = jnp.where(same, s, NEG_INF)
    e = jnp.exp(s - jnp.max(s, axis=-1, keepdims=True))
    prob = e / jnp.sum(e, axis=-1, keepdims=True)
    o = jnp.dot(prob.astype(BF16), v_all, preferred_element_type=F32)
    return [jnp.concatenate([o[(b * H_C + h) * SUBLANES:(b * H_C + h + 1) * SUBLANES]
                             for h in range(H_C)], axis=1) for b in range(nb)]


def _xattn_kernel(q_ref, k_ref, v_ref, o_ref, *, nseq, rq, valid):
    per = rq // valid
    for t in range(nseq // per):
        tile = q_ref[t * rq:(t + 1) * rq, :]
        if per == 1:
            mem = slice(t * MEM_LEN, (t + 1) * MEM_LEN)
            o = _attend(tile.astype(BF16), k_ref[mem, :].astype(BF16), v_ref[mem, :].astype(BF16))
        else:
            seqs = range(t * per, (t + 1) * per)
            flat = lambda ref, i: ref[i].reshape(MEM_LEN * H_C, DH_C).astype(BF16)
            o = _pack(_attend_flat([_unpack(tile, j, valid) for j in range(per)],
                                   [flat(k_ref, i) for i in seqs], [flat(v_ref, i) for i in seqs]),
                      valid)
        o_ref[t * rq:(t + 1) * rq, :] = o.astype(o_ref.dtype)


def xattn(proj, k_arr, v_arr, k_col, v_col, *, nbatch, nseq, rq, out_dtype):
    r = proj.shape[0]
    seq = r // nbatch
    valid = min(seq, rq)
    rows = nseq * valid
    nq = seq // valid
    assert nseq == 1 or nq == 1
    q_col = (2 * D_MODEL + 2 * D_B) // D_C

    def mem_spec(arr, col):
        if arr.ndim == 4:
            return pl.BlockSpec((nseq, MEM_LEN, H_C, DH_C), lambda g, t: (g, 0, 0, 0))
        return pl.BlockSpec((nseq * MEM_LEN, D_C), lambda g, t: (g, col))

    kern = functools.partial(_xattn_kernel, nseq=nseq, rq=rq, valid=valid)
    return pl.pallas_call(
        kern,
        grid=(nbatch // nseq, nq),
        in_specs=[pl.BlockSpec((rows, D_C), lambda g, t: (g * nq + t, q_col)),
                  mem_spec(k_arr, k_col), mem_spec(v_arr, v_col)],
        out_specs=pl.BlockSpec((rows, D_C), lambda g, t: (g * nq + t, 0)),
        out_shape=jax.ShapeDtypeStruct((r, D_C), out_dtype),
        compiler_params=_cparams("parallel", "arbitrary"),
        name="xattn",
    )(proj, k_arr, v_arr)


def _mix_out_kernel(ya_ref, yb_ref, yc_ref, ga0, ga1, gb0, gb1, gc0, gc1, x_ref,
                    wa_ref, wb_ref, wc_ref, wo_ref, g1_ref, g2_ref, x1_ref, h2_ref):
    def branch(y_ref, w_ref, lo, hi):
        gate = jnp.concatenate([lo[...], hi[...]], axis=1)
        return jax.nn.sigmoid(gate) * jnp.dot(y_ref[...].astype(BF16), w_ref[...],
                                              preferred_element_type=F32)

    merged = (branch(ya_ref, wa_ref, ga0, ga1) + branch(yb_ref, wb_ref, gb0, gb1)
              + branch(yc_ref, wc_ref, gc0, gc1))
    o = jnp.dot(merged.astype(BF16), wo_ref[...], preferred_element_type=F32)
    x1 = x_ref[...] + _rms(o) * g1_ref[...]
    x1_ref[...] = x1
    h2_ref[...] = (_rms(x1) * g2_ref[...]).astype(h2_ref.dtype)


def mix_out(ya, yb, yc, proj, x, w_a, w_b, w_c, w_o, g_post, g_pre, bm=256):
    r = x.shape[0]
    half = D_MODEL // 2
    g0 = GATE_COL0 // half
    gate = lambda k: pl.BlockSpec((bm, half), lambda i, k=k: (i, g0 + k))
    rowspec = lambda d: pl.BlockSpec((bm, d), lambda i: (i, 0))
    vec = pl.BlockSpec((1, D_MODEL), lambda i: (0, 0))
    return pl.pallas_call(
        _mix_out_kernel,
        grid=(r // bm,),
        in_specs=[rowspec(D_MODEL), rowspec(D_B), rowspec(D_C)] + [gate(k) for k in range(6)]
                 + [rowspec(D_MODEL), _resident((D_MODEL, D_MODEL)), _resident((D_B, D_MODEL)),
                    _resident((D_C, D_MODEL)), _resident((D_MODEL, D_MODEL)), vec, vec],
        out_specs=[rowspec(D_MODEL), rowspec(D_MODEL)],
        out_shape=[jax.ShapeDtypeStruct((r, D_MODEL), F32), jax.ShapeDtypeStruct((r, D_MODEL), BF16)],
        compiler_params=_cparams("parallel"),
        name="mix_out",
    )(ya, yb, yc, proj, proj, proj, proj, proj, proj, x, w_a, w_b, w_c, w_o,
      g_post.reshape(1, D_MODEL), g_pre.reshape(1, D_MODEL))


def _gated_act(gate, val):
    return jax.nn.gelu(gate) * val


def _ffn_up_act_kernel(a_ref, wg_ref, wv_ref, cwg_ref, cwv_ref, cbg_ref, cbv_ref,
                       o_ref, sg_ref, sv_ref, ext_ref, *, bm):
    @pl.when(pl.program_id(2) == 0)
    def _():
        ext_ref[:, 0:SUBLANES, :] = jnp.zeros((2, SUBLANES, ext_ref.shape[2]), F32)

    a = a_ref[...]

    def conv(which, w_ref, cw_ref, cb_ref, st_ref):
        u = jnp.dot(a, w_ref[...].astype(BF16), preferred_element_type=F32)
        ext_ref[which, SUBLANES:SUBLANES + bm, :] = u
        cw = cw_ref[...]
        y = cb_ref[...] + u * cw[CONV_F - 1:CONV_F, :]
        for j in range(CONV_F - 1):
            y = y + ext_ref[which, pl.ds(SUBLANES - (CONV_F - 1) + j, bm), :] * cw[j:j + 1, :]
        tail = u[bm - SUBLANES:bm, :]
        ext_ref[which, 0:SUBLANES, :] = tail
        st_ref[0] = tail
        return y

    gate = conv(0, wg_ref, cwg_ref, cbg_ref, sg_ref)
    val = conv(1, wv_ref, cwv_ref, cbv_ref, sv_ref)
    o_ref[...] = _gated_act(gate, val).astype(o_ref.dtype)


def ffn_up_act(h2, w_up, conv_w, conv_b, *, nseq, bm=1024, bn=768):
    r = h2.shape[0]
    tiles = r // nseq // bm
    ncol = D_FF // bn
    wspec = lambda off: pl.BlockSpec((D_MODEL, bn), lambda j, b, t, off=off: (0, off + j))
    cwspec = lambda off: pl.BlockSpec((CONV_F, bn), lambda j, b, t, off=off: (0, off + j))
    cbspec = lambda off: pl.BlockSpec((1, bn), lambda j, b, t, off=off: (0, off + j))
    st_spec = pl.BlockSpec((1, SUBLANES, bn), lambda j, b, t: (b, 0, j))
    kern = functools.partial(_ffn_up_act_kernel, bm=bm)
    return pl.pallas_call(
        kern,
        grid=(ncol, nseq, tiles),
        in_specs=[pl.BlockSpec((bm, D_MODEL), lambda j, b, t: (b * tiles + t, 0)),
                  wspec(0), wspec(ncol), cwspec(0), cwspec(ncol), cbspec(0), cbspec(ncol)],
        out_specs=[pl.BlockSpec((bm, bn), lambda j, b, t: (b * tiles + t, j)), st_spec, st_spec],
        out_shape=[jax.ShapeDtypeStruct((r, D_FF), BF16),
                   jax.ShapeDtypeStruct((nseq, SUBLANES, D_FF), F32),
                   jax.ShapeDtypeStruct((nseq, SUBLANES, D_FF), F32)],
        scratch_shapes=[pltpu.VMEM((2, SUBLANES + bm, bn), F32)],
        compiler_params=_cparams("parallel", "arbitrary", "arbitrary"),
        name="ffn_up_act",
    )(h2, w_up, w_up, conv_w, conv_w, conv_b, conv_b)


def _ffn_act_kernel(g_ref, v_ref, sg_ref, sv_ref, wg_ref, wv_ref, bg_ref, bv_ref,
                    o_ref, ng_ref, nv_ref, *, nseq, valid):
    per = SUBLANES // valid
    wg, wv, bg, bv = wg_ref[...], wv_ref[...], bg_ref[...], bv_ref[...]
    for t in range(nseq // per):
        rows = slice(t * SUBLANES, (t + 1) * SUBLANES)
        seqs = range(t * per, (t + 1) * per)
        g_tile, v_tile = g_ref[rows, :], v_ref[rows, :]
        gate = _packed_tile_conv(g_tile, [sg_ref[i] for i in seqs], wg, bg, valid)
        val = _packed_tile_conv(v_tile, [sv_ref[i] for i in seqs], wv, bv, valid)
        o_ref[rows, :] = _gated_act(gate, val).astype(o_ref.dtype)
        for j, i in enumerate(seqs):
            ng_ref[i] = _last_rows(_unpack(g_tile, j, valid), valid, CONV_F - 1)
            nv_ref[i] = _last_rows(_unpack(v_tile, j, valid), valid, CONV_F - 1)


def ffn_act(up, hist, conv_w, conv_b, *, nbatch, nseq, bc=2048):
    r = up.shape[0]
    valid = r // nbatch
    ncol = D_FF // bc
    nh = CONV_F - 1
    x_spec = lambda off: pl.BlockSpec((nseq * valid, bc), lambda g, j, off=off: (g, off + j))
    s_spec = lambda off: pl.BlockSpec((nseq, nh, bc), lambda g, j, off=off: (g, 0, off + j))
    w_spec = lambda off: pl.BlockSpec((CONV_F, bc), lambda g, j, off=off: (0, off + j))
    b_spec = lambda off: pl.BlockSpec((1, bc), lambda g, j, off=off: (0, off + j))
    kern = functools.partial(_ffn_act_kernel, nseq=nseq, valid=valid)
    return pl.pallas_call(
        kern,
        grid=(nbatch // nseq, ncol),
        in_specs=[x_spec(0), x_spec(ncol), s_spec(0), s_spec(ncol), w_spec(0), w_spec(ncol),
                  b_spec(0), b_spec(ncol)],
        out_specs=[pl.BlockSpec((nseq * valid, bc), lambda g, j: (g, j)), s_spec(0), s_spec(0)],
        out_shape=[jax.ShapeDtypeStruct((r, D_FF), F32),
                   jax.ShapeDtypeStruct((nbatch, nh, D_FF), F32),
                   jax.ShapeDtypeStruct((nbatch, nh, D_FF), F32)],
        compiler_params=_cparams("parallel", "parallel"),
        name="ffn_act",
    )(up, up, hist, hist, conv_w, conv_w, conv_b, conv_b)


def _ffn_down_kernel(a_ref, w_ref, x_ref, g_ref, y_ref):
    f = jnp.dot(a_ref[...].astype(BF16), w_ref[...], preferred_element_type=F32)
    y_ref[...] = x_ref[...] + _rms(f) * g_ref[...]


def ffn_down_post(act, w_down, x1, g, bm=256):
    r = x1.shape[0]
    return pl.pallas_call(
        _ffn_down_kernel,
        grid=(r // bm,),
        in_specs=[pl.BlockSpec((bm, D_FF), lambda i: (i, 0)), _resident((D_FF, D_MODEL)),
                  pl.BlockSpec((bm, D_MODEL), lambda i: (i, 0)),
                  pl.BlockSpec((1, D_MODEL), lambda i: (0, 0))],
        out_specs=pl.BlockSpec((bm, D_MODEL), lambda i: (i, 0)),
        out_shape=jax.ShapeDtypeStruct((r, D_MODEL), F32),
        compiler_params=_cparams("parallel"),
        name="ffn_down_post",
    )(act, w_down, x1, g.reshape(1, D_MODEL))


def _blockdiag_tiles(w):
    nblk = D_MODEL // BD_BLOCK
    rows = w.reshape(nblk, BD_BLOCK, QKV_BLOCK)
    tiled = jnp.tile(rows, (1, 1, BD_BLOCK // QKV_BLOCK))
    idx = jnp.arange(BD_BLOCK) // QKV_BLOCK
    same_block = idx[:, None] == idx[None, :]
    return jnp.where(same_block[None], tiled, 0.0).astype(BF16)


def _prep_params(conv_a_w, conv_a_b, w_q_a, w_k_a, w_v_a, w_if_a, b_if_a, norm_h_a, skip_a):
    pad = GATE_LANES - 2 * H_A
    return {
        "conv_a_w": conv_a_w,
        "conv_a_b": conv_a_b.reshape(1, D_MODEL),
        "wq_bd": _blockdiag_tiles(w_q_a),
        "wk_bd": _blockdiag_tiles(w_k_a),
        "wv_bd": _blockdiag_tiles(w_v_a),
        "w_if": jnp.pad(w_if_a, ((0, 0), (0, pad))).astype(BF16),
        "b_if": jnp.pad(b_if_a, (0, pad)).reshape(1, GATE_LANES),
        "norm_h_a": norm_h_a.reshape(1, D_MODEL),
        "skip_a": skip_a.reshape(1, D_MODEL),
    }


def _trunk(x, mem, states, w, pa, *, nseq, seq):
    prompt = states is None
    k_arr, v_arr, k_col, v_col = mem

    h = rms_cast(x, w["norm_mix_pre"])
    proj = matmul(h, w["w_in"], 1024, 1024)

    yb, vn = gmlp(proj, w["norm_v_b"], w["w_s_b"], w["b_s_b"], seq, emit_vn=not prompt)
    ya, c_new, n_new, m_new, conv_new = branch_a(proj, None if prompt else states[:4], pa,
                                                 nseq=nseq, seq=seq,
                                                 out_dtype=BF16 if prompt else F32)
    if prompt:
        yc = xattn(proj, k_arr, v_arr, k_col, v_col, nbatch=nseq, nseq=1, rq=1024, out_dtype=BF16)
        conv_new = proj.reshape(nseq, seq, N_IN)[:, seq - (CONV_A - 1):, :D_MODEL]
    else:
        yc = xattn(proj, k_arr, v_arr, k_col, v_col, nbatch=nseq, nseq=8, rq=SUBLANES, out_dtype=F32)

    x1, h2 = mix_out(ya, yb, yc, proj, x, w["w_br_a"], w["w_br_b"], w["w_br_c"], w["w_out"],
                     w["norm_mix_post"], w["norm_ffn_pre"])

    conv_b = w["conv_ffn_b"].reshape(1, 2 * D_FF)
    if prompt:
        act, tail_g, tail_v = ffn_up_act(h2, w["w_ffn_up"], w["conv_ffn_w"], conv_b, nseq=nseq)
        ffn_new = jnp.concatenate([tail_g, tail_v], axis=-1)[:, SUBLANES - (CONV_F - 1):]
    else:
        up = matmul(h2, w["w_ffn_up"], 1024, 1024)
        act, new_g, new_v = ffn_act(up, states[4], w["conv_ffn_w"], conv_b, nbatch=nseq, nseq=32)
        ffn_new = jnp.concatenate([new_g, new_v], axis=-1)
    y = ffn_down_post(act, w["w_ffn_down"], x1, w["norm_ffn_post"])
    return y, conv_new, ffn_new, vn, c_new, n_new, m_new


def kernel(x_prompt, x_sample, state_conv_a, state_mlstm_c, state_mlstm_n, state_mlstm_m, cache_mem_k, cache_mem_v, state_ffn_conv, mem_prompt, norm_mix_pre, w_in, conv_a_w, conv_a_b, w_q_a, w_k_a, w_v_a, w_if_a, b_if_a, norm_h_a, skip_a, w_br_a, norm_v_b, w_s_b, b_s_b, w_br_b, norm_mem, w_mem_kv, w_br_c, w_out, norm_mix_post, norm_ffn_pre, w_ffn_up, conv_ffn_w, conv_ffn_b, w_ffn_down, norm_ffn_post):
    nb, seq, _ = x_prompt.shape
    db, dseq, _ = x_sample.shape
    w = {
        "norm_mix_pre": norm_mix_pre[0], "w_in": w_in[0], "norm_v_b": norm_v_b[0],
        "w_s_b": w_s_b[0], "b_s_b": b_s_b[0],
        "w_br_a": w_br_a[0].astype(BF16), "w_br_b": w_br_b[0].astype(BF16),
        "w_br_c": w_br_c[0].astype(BF16), "w_out": w_out[0].astype(BF16),
        "norm_mix_post": norm_mix_post[0], "norm_ffn_pre": norm_ffn_pre[0], "w_ffn_up": w_ffn_up[0],
        "conv_ffn_w": conv_ffn_w[0], "conv_ffn_b": conv_ffn_b[0],
        "w_ffn_down": w_ffn_down[0].astype(BF16), "norm_ffn_post": norm_ffn_post[0],
    }
    pa = _prep_params(conv_a_w[0], conv_a_b[0], w_q_a[0], w_k_a[0], w_v_a[0], w_if_a[0], b_if_a[0],
                      norm_h_a[0], skip_a[0])

    mem_h = rms_cast(mem_prompt.reshape(nb * MEM_LEN, D_MODEL), norm_mem[0])
    mem_k, mem_v = matmul_pair(mem_h, w_mem_kv[0])
    yp, conv_p, ffn_p, _, c_p, n_p, m_p = _trunk(
        x_prompt.reshape(nb * seq, D_MODEL), (mem_k, mem_v, 0, 0), None, w, pa, nseq=nb, seq=seq)

    m0 = jnp.pad(state_mlstm_m[0][:, None, :], ((0, 0), (0, SUBLANES - 1), (0, GATE_LANES - H_A)))
    ys, conv_s, ffn_s, vn_s, c_s, n_s, m_s = _trunk(
        x_sample.reshape(db * dseq, D_MODEL), (cache_mem_k[0], cache_mem_v[0], 0, 0),
        (state_conv_a[0], state_mlstm_c[0], state_mlstm_n[0], m0, state_ffn_conv[0]),
        w, pa, nseq=db, seq=dseq)

    return (
        yp.reshape(nb, seq, D_MODEL), ys.reshape(db, dseq, D_MODEL),
        conv_p[None], c_p[None], n_p[None], m_p[None],
        mem_k.reshape(1, nb, MEM_LEN, H_C, DH_C), mem_v.reshape(1, nb, MEM_LEN, H_C, DH_C),
        ffn_p[None],
        conv_s[None], c_s[None], n_s[None], m_s[None],
        vn_s.reshape(1, db, dseq, D_B), ffn_s[None],
    )
```
